```python
import jax
import jax.numpy as jnp
from jax import lax
import numpy as np

D_MODEL = 1024
BATCH = 32
SEQ = 2048
DEPTH = 2

GRID_W = 64
CTX_LEN = 256
N_MIXERS = 2
EPS = 1e-6

GLA_HEADS = 4
GLA_KDIM = D_MODEL // 2
GLA_VDIM = D_MODEL
GLA_KHEAD = GLA_KDIM // GLA_HEADS
GLA_VHEAD = GLA_VDIM // GLA_HEADS
GLA_GATE_RANK = 16
GLA_GATE_NORM = 16.0
GLA_CHUNK = 64

POOL_WINDOWS = (2, 4, 8, 16)
POOL_GROUP = D_MODEL // len(POOL_WINDOWS)

N_EXPERTS = 16
N_EXPERT_GROUPS = 4
EXPERTS_PER_GROUP = N_EXPERTS // N_EXPERT_GROUPS
TOP_K = 2
D_EXPERT = D_MODEL // 2

kernel_name = 'hybrid_gla_pool_moe_prefix_dit'


def _rmsnorm(x, g):
    xf = x.astype(jnp.float32)
    y = xf * lax.rsqrt(jnp.mean(xf * xf, axis=-1, keepdims=True) + EPS)
    return (y * g.astype(jnp.float32)).astype(x.dtype)


def _modulate(h, shift, scale):
    return h * (1 + scale) + shift


def _gla_scan(q, k, v, g, s0, with_outputs):
    bsz, nh, t, _ = q.shape
    dv = v.shape[-1]
    n = t // GLA_CHUNK

    def chunks(a):
        a = a.astype(jnp.float32).reshape(bsz, nh, n, GLA_CHUNK, a.shape[-1])
        return jnp.moveaxis(a, 2, 0)

    lower = jnp.tril(jnp.ones((GLA_CHUNK, GLA_CHUNK), dtype=bool))[:, :, None]

    def step(s, inp):
        qc, kc, vc, gc = inp
        cum = jnp.cumsum(gc, axis=2)
        last = cum[:, :, -1:, :]
        s_new = jnp.exp(last[:, :, 0, :])[..., None] * s + jnp.einsum(
            'bhjd,bhje->bhde', kc * jnp.exp(last - cum), vc)
        if not with_outputs:
            return s_new, None
        diff = cum[:, :, :, None, :] - cum[:, :, None, :, :]
        decay = jnp.exp(jnp.where(lower, diff, -jnp.inf))
        scores = jnp.einsum('bhid,bhjd,bhijd->bhij', qc, kc, decay)
        o = jnp.einsum('bhid,bhde->bhie', qc * jnp.exp(cum), s) + jnp.einsum(
            'bhij,bhje->bhie', scores, vc)
        return s_new, o

    s_fin, o = lax.scan(step, s0, (chunks(q), chunks(k), chunks(v), chunks(g)))
    if not with_outputs:
        return None, s_fin
    o = jnp.moveaxis(o, 0, 2).reshape(bsz, nh, t, dv)
    return o, s_fin


def _gla_project(h, w_in, w_ga, w_gb, b_g):
    bsz, t, _ = h.shape
    proj = h @ w_in
    q, k, v, r = jnp.split(proj, [GLA_KDIM, 2 * GLA_KDIM, 2 * GLA_KDIM + GLA_VDIM], axis=-1)

    def heads(a, d):
        return a.reshape(bsz, t, GLA_HEADS, d).transpose(0, 2, 1, 3)

    q = heads(q, GLA_KHEAD) * (GLA_KHEAD ** -0.5)
    k = heads(k, GLA_KHEAD)
    v = heads(v, GLA_VHEAD)
    low = jnp.einsum('btd,zdr->zbtr', h, w_ga)
    z = jnp.einsum('zbtr,zrk->zbtk', low, w_gb) + b_g[:, None, None, :]
    logdecay = jax.nn.log_sigmoid(z.astype(jnp.float32)) / GLA_GATE_NORM
    logdecay = logdecay.reshape(2, bsz, t, GLA_HEADS, GLA_KHEAD).transpose(0, 1, 3, 2, 4)
    return q, k, v, r, logdecay


def _gla_readout(o, r, norm_g, w_out):
    bsz, _, t, _ = o.shape
    o = o * lax.rsqrt(jnp.mean(o * o, axis=-1, keepdims=True) + EPS) * norm_g.astype(jnp.float32)
    o = o.transpose(0, 2, 1, 3).reshape(bsz, t, GLA_VDIM).astype(r.dtype)
    return (o * jax.nn.silu(r)) @ w_out


def _gla_mixer(h_lat, h_ctx, w_in, w_ga, w_gb, b_g, norm_g, w_out, ctx_out):
    ql, kl, vl, rl, gl = _gla_project(h_lat, w_in, w_ga, w_gb, b_g)
    qc, kc, vc, rc, gc = _gla_project(h_ctx, w_in, w_ga, w_gb, b_g)

    def flip(a):
        return jnp.flip(a, axis=2)

    s0 = jnp.zeros((h_lat.shape[0], GLA_HEADS, GLA_KHEAD, GLA_VHEAD), jnp.float32)
    oc_f, sc_f = _gla_scan(qc, kc, vc, gc[0], s0, ctx_out)
    oc_b, sc_b = _gla_scan(flip(qc), flip(kc), flip(vc), flip(gc[1]), s0, ctx_out)
    ol_f, _ = _gla_scan(ql, kl, vl, gl[0], sc_f, True)
    ol_b, _ = _gla_scan(flip(ql), flip(kl), flip(vl), flip(gl[1]), sc_b, True)
    y_lat = _gla_readout(ol_f + flip(ol_b), rl, norm_g, w_out)
    y_ctx = _gla_readout(oc_f + flip(oc_b), rc, norm_g, w_out) if ctx_out else None
    return y_lat, y_ctx


def _pool_mixer(h, w_pool, b_pool, scale):
    s, l, _ = h.shape
    hf = h.astype(jnp.float32).reshape(s, l, len(POOL_WINDOWS), POOL_GROUP)
    csum = jnp.concatenate([jnp.zeros_like(hf[:, :1]), jnp.cumsum(hf, axis=1)], axis=1)
    pos = np.arange(l)
    outs = []
    for gi, w in enumerate(POOL_WINDOWS):
        lo = np.clip(pos - w // 2, 0, l)
        hi = np.clip(pos - w // 2 + w, 0, l)
        cnt = (hi - lo).astype(np.float32)
        mean = (csum[:, hi, gi] - csum[:, lo, gi]) / cnt[None, :, None]
        outs.append(mean - hf[:, :, gi])
    pooled = jnp.stack(outs, axis=2).astype(h.dtype)
    y = jnp.einsum('slgc,gce->slge', pooled, w_pool) + b_pool
    return y.reshape(s, l, D_MODEL) * scale


def _moe(h, w_router, b_router, w1, w3, w2):
    n = h.shape[0]
    scores = jax.nn.sigmoid((h @ w_router).astype(jnp.float32))
    sel = scores + b_router.astype(jnp.float32)
    grp = sel.reshape(n, N_EXPERT_GROUPS, EXPERTS_PER_GROUP)
    grp_score = jnp.sum(lax.top_k(grp, TOP_K)[0], axis=-1)
    best = jnp.argmax(grp_score, axis=-1)
    in_group = (jnp.arange(N_EXPERTS) // EXPERTS_PER_GROUP)[None, :] == best[:, None]
    _, idx = lax.top_k(jnp.where(in_group, sel, -jnp.inf), TOP_K)
    wts = jnp.take_along_axis(scores, idx, axis=-1)
    wts = wts / jnp.sum(wts, axis=-1, keepdims=True)
    gates = jnp.sum(jax.nn.one_hot(idx, N_EXPERTS, dtype=jnp.float32) * wts[..., None], axis=1)
    gates = gates.astype(h.dtype)
    out = jnp.zeros_like(h)
    for e in range(N_EXPERTS):
        he = jax.nn.silu(h @ w1[e]) * (h @ w3[e])
        out = out + gates[:, e:e + 1] * (he @ w2[e])
    return out


def setup_inputs(seed: int = 0) -> dict:
    key = jax.random.key(seed)
    ks = jax.random.split(key, 24)
    n_gla = len(range(0, DEPTH, N_MIXERS))
    n_pool = len(range(1, DEPTH, N_MIXERS))
    f32 = jnp.float32

    def nrm(k, shape, fan):
        return jax.random.normal(k, shape, f32) * (fan ** -0.5)

    def rnd(k, shape, s):
        return jax.random.normal(k, shape, f32) * s

    return {
        'x': rnd(ks[0], (BATCH, SEQ, D_MODEL), 1.0),
        'c': rnd(ks[1], (BATCH, D_MODEL), 1.0),
        'ctx': rnd(ks[2], (BATCH, CTX_LEN, D_MODEL), 1.0),
        'c_ctx': rnd(ks[3], (D_MODEL,), 1.0),
        'norm1_g': 1.0 + rnd(ks[4], (DEPTH, D_MODEL), 0.02),
        'norm2_g': 1.0 + rnd(ks[5], (DEPTH, D_MODEL), 0.02),
        'w_mod': 0.5 * nrm(ks[6], (DEPTH, D_MODEL, 6 * D_MODEL), D_MODEL),
        'b_mod': rnd(ks[7], (DEPTH, 6 * D_MODEL), 0.02),
        'gla_w_in': nrm(ks[8], (n_gla, D_MODEL, 2 * GLA_KDIM + 2 * GLA_VDIM), D_MODEL),
        'gla_w_gate_a': nrm(ks[9], (n_gla, 2, D_MODEL, GLA_GATE_RANK), D_MODEL),
        'gla_w_gate_b': nrm(ks[10], (n_gla, 2, GLA_GATE_RANK, GLA_KDIM), GLA_GATE_RANK),
        'gla_b_gate': rnd(ks[11], (n_gla, 2, GLA_KDIM), 0.1),
        'gla_norm_g': 1.0 + rnd(ks[12], (n_gla, GLA_VHEAD), 0.02),
        'gla_w_out': nrm(ks[13], (n_gla, GLA_VDIM, D_MODEL), GLA_VDIM),
        'pool_w': nrm(ks[14], (n_pool, len(POOL_WINDOWS), POOL_GROUP, POOL_GROUP), POOL_GROUP),
        'pool_b': rnd(ks[15], (n_pool, len(POOL_WINDOWS), POOL_GROUP), 0.02),
        'pool_scale': 1.0 + rnd(ks[16], (n_pool, D_MODEL), 0.1),
        'w_router': nrm(ks[17], (D_MODEL, N_EXPERTS), D_MODEL),
        'b_router': rnd(ks[18], (N_EXPERTS,), 0.01),
        'w_gate_e': nrm(ks[19], (DEPTH, N_EXPERTS, D_MODEL, D_EXPERT), D_MODEL),
        'w_up_e': nrm(ks[20], (DEPTH, N_EXPERTS, D_MODEL, D_EXPERT), D_MODEL),
        'w_down_e': nrm(ks[21], (DEPTH, N_EXPERTS, D_EXPERT, D_MODEL), D_EXPERT),
        'final_g': 1.0 + rnd(ks[22], (D_MODEL,), 0.02),
    }


def reference(x, c, ctx, c_ctx, norm1_g, norm2_g, w_mod, b_mod, gla_w_in, gla_w_gate_a,
              gla_w_gate_b, gla_b_gate, gla_norm_g, gla_w_out, pool_w, pool_b, pool_scale,
              w_router, b_router, w_gate_e, w_up_e, w_down_e, final_g):
    bsz, t, _ = x.shape
    rows = t // GRID_W
    n_lat = bsz * t
    for i in range(DEPTH):
        mixer = i % N_MIXERS
        slot = i // N_MIXERS
        ctx_next = any(j % N_MIXERS == 0 for j in range(i + 1, DEPTH))
        use_ctx = mixer == 0 or ctx_next

        mod = jax.nn.silu(c) @ w_mod[i] + b_mod[i]
        sh1, sc1, g1, sh2, sc2, g2 = jnp.split(mod[:, None, :], 6, axis=-1)
        h = _modulate(_rmsnorm(x, norm1_g[i]), sh1, sc1)
        if use_ctx:
            mod_c = jax.nn.silu(c_ctx) @ w_mod[i] + b_mod[i]
            csh1, csc1, cg1, csh2, csc2, cg2 = jnp.split(mod_c, 6, axis=-1)
            hc = _modulate(_rmsnorm(ctx, norm1_g[i]), csh1, csc1)

        if mixer == 0:
            y, yc = _gla_mixer(h, hc, gla_w_in[slot], gla_w_gate_a[slot], gla_w_gate_b[slot],
                               gla_b_gate[slot], gla_norm_g[slot], gla_w_out[slot], ctx_next)
        else:
            y = _pool_mixer(h.reshape(bsz * rows, GRID_W, D_MODEL), pool_w[slot], pool_b[slot],
                            pool_scale[slot]).reshape(bsz, t, D_MODEL)
            yc = _pool_mixer(hc, pool_w[slot], pool_b[slot], pool_scale[slot]) if ctx_next else None

        x = x + g1 * y
        h2 = _modulate(_rmsnorm(x, norm2_g[i]), sh2, sc2).reshape(n_lat, D_MODEL)
        if ctx_next:
            ctx = ctx + cg1 * yc
            h2c = _modulate(_rmsnorm(ctx, norm2_g[i]), csh2, csc2).reshape(-1, D_MODEL)
            f = _moe(jnp.concatenate([h2, h2c], axis=0), w_router, b_router,
                     w_gate_e[i], w_up_e[i], w_down_e[i])
            x = x + g2 * f[:n_lat].reshape(bsz, t, D_MODEL)
            ctx = ctx + cg2 * f[n_lat:].reshape(ctx.shape)
        else:
            f = _moe(h2, w_router, b_router, w_gate_e[i], w_up_e[i], w_down_e[i])
            x = x + g2 * f.reshape(bsz, t, D_MODEL)
    return _rmsnorm(x, final_g)
```

```python
import functools

import numpy as np
import jax
import jax.numpy as jnp
from jax import lax
from jax.experimental import pallas as pl
from jax.experimental.pallas import tpu as pltpu

EPS = 1e-6
GRID_W = 64
GLA_HEADS = 4
GLA_GATE_RANK = 16
GLA_GATE_NORM = 16.0
GLA_CHUNK = 64
POOL_WINDOWS = (2, 4, 8, 16)
N_EXPERTS = 16
N_EXPERT_GROUPS = 4
EXPERTS_PER_GROUP = N_EXPERTS // N_EXPERT_GROUPS
TOP_K = 2

F32 = jnp.float32
BF16 = jnp.bfloat16

_NT = (((1,), (1,)), ((), ()))
_TN = (((0,), (0,)), ((), ()))
_VMEM_LIMIT = 56 * 1024 * 1024


def _cparams(sem):
    return pltpu.CompilerParams(dimension_semantics=sem, vmem_limit_bytes=_VMEM_LIMIT)


def _dot(a, b, dims=None):
    if dims is None:
        return jnp.dot(a, b, preferred_element_type=F32)
    return lax.dot_general(a, b, dims, preferred_element_type=F32)


def _split(a):
    hi = a.astype(BF16)
    lo = (a - hi.astype(F32)).astype(BF16)
    return hi, lo


def _dot3(a, b, dims=None):
    ah, al = _split(a)
    bh, bl = _split(b)
    return _dot(ah, bh, dims) + _dot(ah, bl, dims) + _dot(al, bh, dims)


def _sigmoid(x):
    return 1.0 / (1.0 + jnp.exp(-x))


def _silu(x):
    return x * _sigmoid(x)


def _rmsnorm(xf, g):
    return xf * lax.rsqrt(jnp.mean(xf * xf, axis=-1, keepdims=True) + EPS) * g


def _mod_kernel(c_ref, w_ref, b_ref, o_ref):
    o_ref[...] = _dot3(_silu(c_ref[...]), w_ref[...]) + b_ref[...]


def _modulation(c_all, w_mod, b_mod):
    depth, d, d6 = w_mod.shape
    rows = c_all.shape[0]
    tn = 1536
    return pl.pallas_call(
        _mod_kernel,
        out_shape=jax.ShapeDtypeStruct((depth, rows, d6), F32),
        grid=(depth, d6 // tn),
        in_specs=[
            pl.BlockSpec((rows, d), lambda l, j: (0, 0)),
            pl.BlockSpec((None, d, tn), lambda l, j: (l, 0, j)),
            pl.BlockSpec((None, 1, tn), lambda l, j: (l, 0, j)),
        ],
        out_specs=pl.BlockSpec((None, rows, tn), lambda l, j: (l, 0, j)),
        compiler_params=_cparams(("arbitrary", "arbitrary")),
        name="mod",
    )(c_all, w_mod, b_mod.reshape(depth, 1, d6))


def _mod_spec(layer, chunk, d, row_of_batch):
    return pl.BlockSpec((None, None, 1, d), lambda b, t: (layer, row_of_batch(b), 0, chunk))


def _gla_in_kernel(x_ref, sh_ref, sc_ref, ng_ref, win_ref, wga_ref, wgb_ref, bg_ref,
                   q_ref, k_ref, v_ref, r_ref, g_ref, *, kdim, vdim, qscale):
    h = _rmsnorm(x_ref[...], ng_ref[...]) * (1.0 + sc_ref[...]) + sh_ref[...]
    hb = h.astype(BF16)
    proj = _dot(hb, win_ref[...])
    q_ref[...] = (proj[:, :kdim] * qscale).astype(BF16)
    k_ref[...] = proj[:, kdim:2 * kdim].astype(BF16)
    v_ref[...] = proj[:, 2 * kdim:2 * kdim + vdim].astype(BF16)
    r_ref[...] = proj[:, 2 * kdim + vdim:].astype(BF16)
    low = _dot(hb, wga_ref[...])
    z = _dot(low.astype(BF16), wgb_ref[...]) + bg_ref[...]
    logsig = jnp.minimum(z, 0.0) - jnp.log(1.0 + jnp.exp(-jnp.abs(z)))
    g_ref[...] = logsig * (1.0 / GLA_GATE_NORM)


def _gla_in(x, mods4, layer, row_of_batch, ng, win, wga, wgb, bg, tm):
    bsz, t, d = x.shape
    kdim, vdim = d // 2, d
    tm = min(tm, t)
    tok = lambda w: pl.BlockSpec((None, tm, w), lambda b, i: (b, i, 0))
    full = lambda a: pl.BlockSpec(a.shape, lambda b, i: (0,) * a.ndim)
    kern = functools.partial(_gla_in_kernel, kdim=kdim, vdim=vdim,
                             qscale=float((kdim // GLA_HEADS) ** -0.5))
    return pl.pallas_call(
        kern,
        out_shape=(
            jax.ShapeDtypeStruct((bsz, t, kdim), BF16),
            jax.ShapeDtypeStruct((bsz, t, kdim), BF16),
            jax.ShapeDtypeStruct((bsz, t, vdim), BF16),
            jax.ShapeDtypeStruct((bsz, t, vdim), BF16),
            jax.ShapeDtypeStruct((bsz, t, 2 * kdim), F32),
        ),
        grid=(bsz, t // tm),
        in_specs=[
            tok(d),
            _mod_spec(layer, 0, d, row_of_batch),
            _mod_spec(layer, 1, d, row_of_batch),
            full(ng), full(win), full(wga), full(wgb), full(bg),
        ],
        out_specs=(tok(kdim), tok(kdim), tok(vdim), tok(vdim), tok(2 * kdim)),
        compiler_params=_cparams(("arbitrary", "arbitrary")),
        name="gla_in",
    )(x, mods4, mods4, ng, win, wga, wgb, bg)


def _tri(c, upper):
    r = lax.broadcasted_iota(jnp.int32, (c, c), 0)
    s = lax.broadcasted_iota(jnp.int32, (c, c), 1)
    return (s >= r) if upper else (r >= s)


def _decay_terms(g, mask_bf, reverse):
    c = g.shape[0]
    gh, gl = _split(g)
    cum = _dot(mask_bf, gh) + _dot(mask_bf, gl)
    tot = cum[0:1] if reverse else cum[c - 1:c]
    mid = cum[c // 2:c // 2 + 1]
    return cum, tot, mid


def _chunk_update(q, k, v, g, st, mask, mask_bf, reverse, want_out):
    cum, tot, mid = _decay_terms(g, mask_bf, reverse)
    kf = k.astype(F32)
    kl = (kf * jnp.exp(tot - cum)).astype(BF16)
    st_new = st * jnp.exp(tot) + _dot(v, kl, _TN)
    if not want_out:
        return st_new, None
    qf = q.astype(F32)
    qe = (qf * jnp.exp(cum)).astype(BF16)
    qi = (qf * jnp.exp(cum - mid)).astype(BF16)
    ki = (kf * jnp.exp(mid - cum)).astype(BF16)
    s = jnp.where(mask, _dot(qi, ki, _NT), 0.0)
    o = _dot(qe, st.astype(BF16), _NT) + _dot(s.astype(BF16), v)
    return st_new, o


def _gla_state_kernel(k_ref, v_ref, gf_ref, gb_ref, sf_ref, sb_ref, *, chunk):
    t = k_ref.shape[0]
    n = t // chunk
    lo = _tri(chunk, False)
    up = _tri(chunk, True)
    lo_bf, up_bf = lo.astype(BF16), up.astype(BF16)
    sf_ref[...] = jnp.zeros_like(sf_ref)
    sb_ref[...] = jnp.zeros_like(sb_ref)

    def body(c, carry):
        sl = pl.ds(pl.multiple_of(c * chunk, chunk), chunk)
        sf_ref[...], _ = _chunk_update(None, k_ref[sl, :], v_ref[sl, :], gf_ref[sl, :],
                                       sf_ref[...], lo, lo_bf, False, False)
        rl = pl.ds(pl.multiple_of((n - 1 - c) * chunk, chunk), chunk)
        sb_ref[...], _ = _chunk_update(None, k_ref[rl, :], v_ref[rl, :], gb_ref[rl, :],
                                       sb_ref[...], up, up_bf, True, False)
        return carry

    lax.fori_loop(0, n, body, 0)


def _gla_scan_kernel(q_ref, k_ref, v_ref, gf_ref, gb_ref, s0f_ref, s0b_ref, ng_ref,
                     o_ref, of_ref, ob_ref, sf_ref, sb_ref, *, chunk):
    t = q_ref.shape[0]
    n = t // chunk
    lo = _tri(chunk, False)
    up = _tri(chunk, True)
    lo_bf, up_bf = lo.astype(BF16), up.astype(BF16)
    sf_ref[...] = s0f_ref[...]
    sb_ref[...] = s0b_ref[...]

    def body(c, carry):
        sl = pl.ds(pl.multiple_of(c * chunk, chunk), chunk)
        sf_ref[...], of_ref[sl, :] = _chunk_update(
            q_ref[sl, :], k_ref[sl, :], v_ref[sl, :], gf_ref[sl, :],
            sf_ref[...], lo, lo_bf, False, True)
        rl = pl.ds(pl.multiple_of((n - 1 - c) * chunk, chunk), chunk)
        sb_ref[...], ob_ref[rl, :] = _chunk_update(
            q_ref[rl, :], k_ref[rl, :], v_ref[rl, :], gb_ref[rl, :],
            sb_ref[...], up, up_bf, True, True)
        return carry

    lax.fori_loop(0, n, body, 0)

    def norm(c, carry):
        sl = pl.ds(pl.multiple_of(c * chunk, chunk), chunk)
        o = of_ref[sl, :] + ob_ref[sl, :]
        o_ref[sl, :] = _rmsnorm(o, ng_ref[...]).astype(o_ref.dtype)
        return carry

    lax.fori_loop(0, n, norm, 0)


def _gla_state(k, v, g, chunk):
    bsz, t, kdim = k.shape
    vdim = v.shape[-1]
    nh = GLA_HEADS
    dk, dv = kdim // nh, vdim // nh
    st = jax.ShapeDtypeStruct((bsz, nh, dv, dk), F32)
    st_spec = pl.BlockSpec((None, None, dv, dk), lambda b, h: (b, h, 0, 0))
    return pl.pallas_call(
        functools.partial(_gla_state_kernel, chunk=chunk),
        out_shape=(st, st),
        grid=(bsz, nh),
        in_specs=[
            pl.BlockSpec((None, t, dk), lambda b, h: (b, 0, h)),
            pl.BlockSpec((None, t, dv), lambda b, h: (b, 0, h)),
            pl.BlockSpec((None, t, dk), lambda b, h: (b, 0, h)),
            pl.BlockSpec((None, t, dk), lambda b, h: (b, 0, nh + h)),
        ],
        out_specs=(st_spec, st_spec),
        compiler_params=_cparams(("arbitrary", "arbitrary")),
        name="gla_state",
    )(k, v, g, g)


def _gla_scan(q, k, v, g, s0f, s0b, norm_g, chunk):
    bsz, t, kdim = k.shape
    vdim = v.shape[-1]
    nh = GLA_HEADS
    dk, dv = kdim // nh, vdim // nh
    st_spec = pl.BlockSpec((None, None, dv, dk), lambda b, h: (b, h, 0, 0))
    kspec = pl.BlockSpec((None, t, dk), lambda b, h: (b, 0, h))
    vspec = pl.BlockSpec((None, t, dv), lambda b, h: (b, 0, h))
    return pl.pallas_call(
        functools.partial(_gla_scan_kernel, chunk=chunk),
        out_shape=jax.ShapeDtypeStruct((bsz, t, vdim), BF16),
        grid=(bsz, nh),
        in_specs=[
            kspec, kspec, vspec, kspec,
            pl.BlockSpec((None, t, dk), lambda b, h: (b, 0, nh + h)),
            st_spec, st_spec,
            pl.BlockSpec((1, dv), lambda b, h: (0, 0)),
        ],
        out_specs=vspec,
        scratch_shapes=[
            pltpu.VMEM((t, dv), F32), pltpu.VMEM((t, dv), F32),
            pltpu.VMEM((dv, dk), F32), pltpu.VMEM((dv, dk), F32),
        ],
        compiler_params=_cparams(("arbitrary", "arbitrary")),
        name="gla_scan",
    )(q, k, v, g, g, s0f, s0b, norm_g)


def _route(h2, wr_t, br):
    logits = _dot3(wr_t, h2, _NT)
    scores = _sigmoid(logits)
    sel = scores + br
    row = [sel[e:e + 1, :] for e in range(N_EXPERTS)]
    picked = []
    for e in range(N_EXPERTS):
        g0 = (e // EXPERTS_PER_GROUP) * EXPERTS_PER_GROUP
        rank = jnp.zeros_like(row[e], dtype=jnp.int32)
        for j in range(g0, g0 + EXPERTS_PER_GROUP):
            if j == e:
                continue
            ahead = (row[j] >= row[e]) if j < e else (row[j] > row[e])
            rank = rank + ahead.astype(jnp.int32)
        picked.append(rank < TOP_K)
    gscore = []
    for g in range(N_EXPERT_GROUPS):
        acc = jnp.zeros_like(row[0])
        for e in range(g * EXPERTS_PER_GROUP, (g + 1) * EXPERTS_PER_GROUP):
            acc = acc + jnp.where(picked[e], row[e], 0.0)
        gscore.append(acc)
    best = []
    for g in range(N_EXPERT_GROUPS):
        ok = None
        for j in range(N_EXPERT_GROUPS):
            if j == g:
                continue
            c = (gscore[g] > gscore[j]) if j < g else (gscore[g] >= gscore[j])
            ok = c if ok is None else (ok & c)
        best.append(ok)
    raw = [jnp.where(picked[e] & best[e // EXPERTS_PER_GROUP], scores[e:e + 1, :], 0.0)
           for e in range(N_EXPERTS)]
    denom = raw[0]
    for e in range(1, N_EXPERTS):
        denom = denom + raw[e]
    return jnp.concatenate([r / denom for r in raw], axis=0)


def _post0_kernel(o_ref, r_ref, x_ref, g1_ref, sh2_ref, sc2_ref, n2_ref, wout_ref, wr_ref, br_ref,
                  x1_ref, h2_ref, gates_ref):
    a = (o_ref[...].astype(F32) * _silu(r_ref[...].astype(F32))).astype(BF16)
    x1 = x_ref[...] + g1_ref[...] * _dot(a, wout_ref[...])
    x1_ref[...] = x1
    h2 = _rmsnorm(x1, n2_ref[...]) * (1.0 + sc2_ref[...]) + sh2_ref[...]
    h2_ref[...] = h2.astype(h2_ref.dtype)
    gates_ref[...] = _route(h2, wr_ref[...], br_ref[...])


def _post0(o, r, x, mods4, layer, n2, wout, wr_t, br, tm):
    bsz, t, d = x.shape
    tm = min(tm, t)
    rb = lambda b: b
    tok = lambda w: pl.BlockSpec((None, tm, w), lambda b, i: (b, i, 0))
    full = lambda a: pl.BlockSpec(a.shape, lambda b, i: (0,) * a.ndim)
    nt = t // tm
    return pl.pallas_call(
        _post0_kernel,
        out_shape=(
            jax.ShapeDtypeStruct((bsz, t, d), F32),
            jax.ShapeDtypeStruct((bsz, t, d), BF16),
            jax.ShapeDtypeStruct((N_EXPERTS, bsz * t), F32),
        ),
        grid=(bsz, nt),
        in_specs=[
            tok(d), tok(d), tok(d),
            _mod_spec(layer, 2, d, rb), _mod_spec(layer, 3, d, rb), _mod_spec(layer, 4, d, rb),
            full(n2), full(wout), full(wr_t), full(br),
        ],
        out_specs=(tok(d), tok(d), pl.BlockSpec((N_EXPERTS, tm), lambda b, i: (0, b * nt + i))),
        compiler_params=_cparams(("arbitrary", "arbitrary")),
        name="post0",
    )(o, r, x, mods4, mods4, mods4, n2, wout, wr_t, br)


def _moe_dense_kernel(h_ref, gt_ref, w1_ref, w3_ref, w2_ref, f_ref):
    e = pl.program_id(1)

    @pl.when(e == 0)
    def _():
        f_ref[...] = jnp.zeros_like(f_ref)

    h = h_ref[...]
    a = _dot(h, w1_ref[...])
    b = _dot(h, w3_ref[...])
    he = (_silu(a) * b).astype(BF16)
    gt = jnp.transpose(gt_ref[...])
    lane = lax.broadcasted_iota(jnp.int32, gt.shape, 1)
    gcol = jnp.sum(jnp.where(lane == e, gt, 0.0), axis=1, keepdims=True)
    f_ref[...] += gcol * _dot(he, w2_ref[...])


def _moe_dense(h2, gates, w1, w3, w2, tm):
    n, d = h2.shape
    ne, _, de = w1.shape
    tm = min(tm, n)
    return pl.pallas_call(
        _moe_dense_kernel,
        out_shape=jax.ShapeDtypeStruct((n, d), F32),
        grid=(n // tm, ne),
        in_specs=[
            pl.BlockSpec((tm, d), lambda i, e: (i, 0)),
            pl.BlockSpec((ne, tm), lambda i, e: (0, i)),
            pl.BlockSpec((None, d, de), lambda i, e: (e, 0, 0)),
            pl.BlockSpec((None, d, de), lambda i, e: (e, 0, 0)),
            pl.BlockSpec((None, de, d), lambda i, e: (e, 0, 0)),
        ],
        out_specs=pl.BlockSpec((tm, d), lambda i, e: (i, 0)),
        compiler_params=_cparams(("arbitrary", "arbitrary")),
        name="moe_dense",
    )(h2, gates, w1, w3, w2)


def _pool_constants(tm, d):
    ng = len(POOL_WINDOWS)
    cg = d // ng
    pos = np.arange(tm)
    seg, off = pos // GRID_W, pos % GRID_W
    mats = np.zeros((ng, tm, tm), np.float32)
    inv = np.zeros((tm, d), np.float32)
    for gi, w in enumerate(POOL_WINDOWS):
        lo = np.clip(off - w // 2, 0, GRID_W)
        hi = np.clip(off - w // 2 + w, 0, GRID_W)
        same = seg[:, None] == seg[None, :]
        inside = (off[None, :] >= lo[:, None]) & (off[None, :] < hi[:, None])
        mats[gi] = (same & inside).astype(np.float32)
        inv[:, gi * cg:(gi + 1) * cg] = (1.0 / (hi - lo).astype(np.float32))[:, None]
    return jnp.asarray(mats, BF16), jnp.asarray(inv, F32)


def _layer1_kernel(x_ref, f_ref, g2p_ref, sh1_ref, sc1_ref, g1_ref, sh2_ref, sc2_ref,
                   n1_ref, n2_ref, pm_ref, inv_ref, wp_ref, bp_ref, ps_ref, wr_ref, br_ref,
                   x3_ref, h2_ref, gates_ref):
    x2 = x_ref[...] + g2p_ref[...] * f_ref[...]
    h = _rmsnorm(x2, n1_ref[...]) * (1.0 + sc1_ref[...]) + sh1_ref[...]
    hb = h.astype(BF16)
    ng = pm_ref.shape[0]
    cg = h.shape[1] // ng
    ys = []
    for gi in range(ng):
        cs = slice(gi * cg, (gi + 1) * cg)
        wsum = _dot(pm_ref[gi], hb[:, cs])
        pooled = wsum * inv_ref[:, cs] - h[:, cs]
        ys.append(_dot(pooled.astype(BF16), wp_ref[gi]))
    y = (jnp.concatenate(ys, axis=1) + bp_ref[...]) * ps_ref[...]
    x3 = x2 + g1_ref[...] * y
    x3_ref[...] = x3
    h2 = _rmsnorm(x3, n2_ref[...]) * (1.0 + sc2_ref[...]) + sh2_ref[...]
    h2_ref[...] = h2.astype(h2_ref.dtype)
    gates_ref[...] = _route(h2, wr_ref[...], br_ref[...])


def _layer1(x1, f0, mods4, n1, n2, wp, bp, ps, wr_t, br, tm):
    bsz, t, d = x1.shape
    tm = min(tm, t)
    pm, inv = _pool_constants(tm, d)
    rb = lambda b: b
    tok = lambda w: pl.BlockSpec((None, tm, w), lambda b, i: (b, i, 0))
    full = lambda a: pl.BlockSpec(a.shape, lambda b, i: (0,) * a.ndim)
    nt = t // tm
    return pl.pallas_call(
        _layer1_kernel,
        out_shape=(
            jax.ShapeDtypeStruct((bsz, t, d), F32),
            jax.ShapeDtypeStruct((bsz, t, d), BF16),
            jax.ShapeDtypeStruct((N_EXPERTS, bsz * t), F32),
        ),
        grid=(bsz, nt),
        in_specs=[
            tok(d), tok(d),
            _mod_spec(0, 5, d, rb),
            _mod_spec(1, 0, d, rb), _mod_spec(1, 1, d, rb), _mod_spec(1, 2, d, rb),
            _mod_spec(1, 3, d, rb), _mod_spec(1, 4, d, rb),
            full(n1), full(n2), full(pm), full(inv), full(wp), full(bp), full(ps),
            full(wr_t), full(br),
        ],
        out_specs=(tok(d), tok(d), pl.BlockSpec((N_EXPERTS, tm), lambda b, i: (0, b * nt + i))),
        compiler_params=_cparams(("arbitrary", "arbitrary")),
        name="layer1",
    )(x1, f0, mods4, mods4, mods4, mods4, mods4, mods4, n1, n2, pm, inv, wp, bp, ps, wr_t, br)


def _final_kernel(x_ref, f_ref, g2_ref, fg_ref, o_ref):
    o_ref[...] = _rmsnorm(x_ref[...] + g2_ref[...] * f_ref[...], fg_ref[...])


def _final(x3, f1, mods4, fg, tm):
    bsz, t, d = x3.shape
    tm = min(tm, t)
    tok = pl.BlockSpec((None, tm, d), lambda b, i: (b, i, 0))
    return pl.pallas_call(
        _final_kernel,
        out_shape=jax.ShapeDtypeStruct((bsz, t, d), F32),
        grid=(bsz, t // tm),
        in_specs=[tok, tok, _mod_spec(1, 5, d, lambda b: b),
                  pl.BlockSpec((1, d), lambda b, i: (0, 0))],
        out_specs=tok,
        compiler_params=_cparams(("arbitrary", "arbitrary")),
        name="final",
    )(x3, f1, mods4, fg)


def kernel(x, c, ctx, c_ctx, norm1_g, norm2_g, w_mod, b_mod, gla_w_in, gla_w_gate_a, gla_w_gate_b,
           gla_b_gate, gla_norm_g, gla_w_out, pool_w, pool_b, pool_scale, w_router, b_router,
           w_gate_e, w_up_e, w_down_e, final_g):
    bsz, t, d = x.shape
    depth = w_mod.shape[0]
    assert depth == 2 and t % GRID_W == 0
    kdim = d // 2
    n_lat = bsz * t
    row = lambda a: a.reshape(1, -1)

    c_rows = -(-(bsz + 1) // 8) * 8
    c_all = jnp.zeros((c_rows, d), F32).at[:bsz].set(c).at[bsz].set(c_ctx)
    mods = _modulation(c_all, w_mod, b_mod)
    mods4 = mods.reshape(depth, c_rows, 1, 6 * d)

    win = gla_w_in[0].astype(BF16)
    wga = jnp.concatenate([gla_w_gate_a[0, 0], gla_w_gate_a[0, 1]], axis=1).astype(BF16)
    zero = jnp.zeros((GLA_GATE_RANK, kdim), F32)
    wgb = jnp.concatenate([
        jnp.concatenate([gla_w_gate_b[0, 0], zero], axis=1),
        jnp.concatenate([zero, gla_w_gate_b[0, 1]], axis=1)], axis=0).astype(BF16)
    bg = gla_b_gate[0].reshape(1, 2 * kdim)
    n1_0 = row(norm1_g[0])
    q, k, v, r, g = _gla_in(x, mods4, 0, lambda b: b, n1_0, win, wga, wgb, bg, 512)
    _, kc, vc, _, gc = _gla_in(ctx, mods4, 0, lambda b: bsz, n1_0, win, wga, wgb, bg, 256)
    s0f, s0b = _gla_state(kc, vc, gc, GLA_CHUNK)
    o = _gla_scan(q, k, v, g, s0f, s0b, row(gla_norm_g[0]), GLA_CHUNK)

    wr_t = jnp.transpose(w_router)
    br = b_router.reshape(N_EXPERTS, 1)
    x1, h2, gates = _post0(o, r, x, mods4, 0, row(norm2_g[0]), gla_w_out[0].astype(BF16),
                           wr_t, br, 512)
    f0 = _moe_dense(h2.reshape(n_lat, d), gates, w_gate_e[0].astype(BF16),
                    w_up_e[0].astype(BF16), w_down_e[0].astype(BF16), 1024)

    x3, h2, gates = _layer1(x1, f0.reshape(bsz, t, d), mods4, row(norm1_g[1]), row(norm2_g[1]),
                            pool_w[0].astype(BF16), row(pool_b[0]), row(pool_scale[0]),
                            wr_t, br, 256)
    f1 = _moe_dense(h2.reshape(n_lat, d), gates, w_gate_e[1].astype(BF16),
                    w_up_e[1].astype(BF16), w_down_e[1].astype(BF16), 1024)
    return _final(x3, f1.reshape(bsz, t, d), mods4, row(final_g), 512)
```

```python
import functools

import numpy as np
import jax
import jax.numpy as jnp
from jax import lax
from jax.experimental import pallas as pl
from jax.experimental.pallas import tpu as pltpu

EPS = 1e-6
GRID_W = 64
GLA_HEADS = 4
GLA_GATE_RANK = 16
GLA_GATE_NORM = 16.0
GLA_CHUNK = 64
POOL_WINDOWS = (2, 4, 8, 16)
N_EXPERTS = 16
N_EXPERT_GROUPS = 4
EXPERTS_PER_GROUP = N_EXPERTS // N_EXPERT_GROUPS
TOP_K = 2

_PAIRS = tuple((a, b) for a in range(EXPERTS_PER_GROUP) for b in range(a + 1, EXPERTS_PER_GROUP))
_PAIR_BASE = (0, 3, 5)
N_CLASSES = N_EXPERT_GROUPS * len(_PAIRS)
CLASS_ROWS = 32
INFO_LANES = 128
_REC_WA, _REC_WB = 2, 3

TM_PROJ = 512
TM_CTX = 256
TM_POST = 512
TM_POOL = 256
TM_MOE = 256
TM_FINAL = 512
DMA_UNROLL = 8

F32 = jnp.float32
BF16 = jnp.bfloat16

_NT = (((1,), (1,)), ((), ()))
_TN = (((0,), (0,)), ((), ()))
_VMEM_LIMIT = 56 * 1024 * 1024


def _cparams(sem):
    return pltpu.CompilerParams(dimension_semantics=sem, vmem_limit_bytes=_VMEM_LIMIT)


def _dot(a, b, dims=None):
    if dims is None:
        return jnp.dot(a, b, preferred_element_type=F32)
    return lax.dot_general(a, b, dims, preferred_element_type=F32)


def _split(a):
    hi = a.astype(BF16)
    lo = (a - hi.astype(F32)).astype(BF16)
    return hi, lo


def _dot3(a, b, dims=None):
    ah, al = _split(a)
    bh, bl = _split(b)
    return _dot(ah, bh, dims) + _dot(ah, bl, dims) + _dot(al, bh, dims)


def _sigmoid(x):
    return 1.0 / (1.0 + jnp.exp(-x))


def _silu(x):
    return x * _sigmoid(x)


def _rmsnorm(xf, g):
    return xf * lax.rsqrt(jnp.mean(xf * xf, axis=-1, keepdims=True) + EPS) * g


def _mod_kernel(c_ref, w_ref, b_ref, o_ref):
    o_ref[...] = _dot3(_silu(c_ref[...]), w_ref[...]) + b_ref[...]


def _modulation(c_all, w_mod, b_mod):
    depth, d, d6 = w_mod.shape
    rows = c_all.shape[0]
    tn = 1536
    return pl.pallas_call(
        _mod_kernel,
        out_shape=jax.ShapeDtypeStruct((depth, rows, d6), F32),
        grid=(depth, d6 // tn),
        in_specs=[
            pl.BlockSpec((rows, d), lambda l, j: (0, 0)),
            pl.BlockSpec((None, d, tn), lambda l, j: (l, 0, j)),
            pl.BlockSpec((None, 1, tn), lambda l, j: (l, 0, j)),
        ],
        out_specs=pl.BlockSpec((None, rows, tn), lambda l, j: (l, 0, j)),
        compiler_params=_cparams(("arbitrary", "arbitrary")),
        name="mod",
    )(c_all, w_mod, b_mod.reshape(depth, 1, d6))


def _mod_spec(layer, chunk, d, row_of_batch):
    return pl.BlockSpec((None, None, 1, d), lambda b, t, *_: (layer, row_of_batch(b), 0, chunk))


def _gla_in_kernel(x_ref, sh_ref, sc_ref, ng_ref, win_ref, wga_ref, wgb_ref, bg_ref,
                   q_ref, k_ref, v_ref, r_ref, g_ref, *, kdim, vdim, qscale):
    h = _rmsnorm(x_ref[...], ng_ref[...]) * (1.0 + sc_ref[...]) + sh_ref[...]
    hb = h.astype(BF16)
    proj = _dot(hb, win_ref[...])
    q_ref[...] = (proj[:, :kdim] * qscale).astype(BF16)
    k_ref[...] = proj[:, kdim:2 * kdim].astype(BF16)
    v_ref[...] = proj[:, 2 * kdim:2 * kdim + vdim].astype(BF16)
    r_ref[...] = proj[:, 2 * kdim + vdim:].astype(BF16)
    low = _dot(hb, wga_ref[...])
    z = _dot(low.astype(BF16), wgb_ref[...]) + bg_ref[...]
    logsig = jnp.minimum(z, 0.0) - jnp.log(1.0 + jnp.exp(-jnp.abs(z)))
    g_ref[...] = logsig * (1.0 / GLA_GATE_NORM)


def _gla_in(x, mods4, layer, row_of_batch, ng, win, wga, wgb, bg, tm):
    bsz, t, d = x.shape
    kdim, vdim = d // 2, d
    tm = min(tm, t)
    tok = lambda w: pl.BlockSpec((None, tm, w), lambda b, i: (b, i, 0))
    full = lambda a: pl.BlockSpec(a.shape, lambda b, i: (0,) * a.ndim)
    kern = functools.partial(_gla_in_kernel, kdim=kdim, vdim=vdim,
                             qscale=float((kdim // GLA_HEADS) ** -0.5))
    return pl.pallas_call(
        kern,
        out_shape=(
            jax.ShapeDtypeStruct((bsz, t, kdim), BF16),
            jax.ShapeDtypeStruct((bsz, t, kdim), BF16),
            jax.ShapeDtypeStruct((bsz, t, vdim), BF16),
            jax.ShapeDtypeStruct((bsz, t, vdim), BF16),
            jax.ShapeDtypeStruct((bsz, t, 2 * kdim), F32),
        ),
        grid=(bsz, t // tm),
        in_specs=[
            tok(d),
            _mod_spec(layer, 0, d, row_of_batch),
            _mod_spec(layer, 1, d, row_of_batch),
            full(ng), full(win), full(wga), full(wgb), full(bg),
        ],
        out_specs=(tok(kdim), tok(kdim), tok(vdim), tok(vdim), tok(2 * kdim)),
        compiler_params=_cparams(("arbitrary", "arbitrary")),
        name="gla_in",
    )(x, mods4, mods4, ng, win, wga, wgb, bg)


def _tri(c, upper):
    r = lax.broadcasted_iota(jnp.int32, (c, c), 0)
    s = lax.broadcasted_iota(jnp.int32, (c, c), 1)
    return (s >= r) if upper else (r >= s)


def _decay_terms(g, mask_bf, reverse):
    c = g.shape[0]
    gh, gl = _split(g)
    cum = _dot(mask_bf, gh) + _dot(mask_bf, gl)
    tot = cum[0:1] if reverse else cum[c - 1:c]
    mid = cum[c // 2:c // 2 + 1]
    return cum, tot, mid


def _chunk_update(q, k, v, g, st, mask, mask_bf, reverse, want_out):
    cum, tot, mid = _decay_terms(g, mask_bf, reverse)
    kf = k.astype(F32)
    kl = (kf * jnp.exp(tot - cum)).astype(BF16)
    st_new = st * jnp.exp(tot) + _dot(v, kl, _TN)
    if not want_out:
        return st_new, None
    qf = q.astype(F32)
    qe = (qf * jnp.exp(cum)).astype(BF16)
    qi = (qf * jnp.exp(cum - mid)).astype(BF16)
    ki = (kf * jnp.exp(mid - cum)).astype(BF16)
    s = jnp.where(mask, _dot(qi, ki, _NT), 0.0)
    o = _dot(qe, st.astype(BF16), _NT) + _dot(s.astype(BF16), v)
    return st_new, o


def _gla_state_kernel(k_ref, v_ref, gf_ref, gb_ref, sf_ref, sb_ref, *, chunk):
    t = k_ref.shape[0]
    n = t // chunk
    lo = _tri(chunk, False)
    up = _tri(chunk, True)
    lo_bf, up_bf = lo.astype(BF16), up.astype(BF16)
    sf_ref[...] = jnp.zeros_like(sf_ref)
    sb_ref[...] = jnp.zeros_like(sb_ref)

    def body(c, carry):
        sl = pl.ds(pl.multiple_of(c * chunk, chunk), chunk)
        sf_ref[...], _ = _chunk_update(None, k_ref[sl, :], v_ref[sl, :], gf_ref[sl, :],
                                       sf_ref[...], lo, lo_bf, False, False)
        rl = pl.ds(pl.multiple_of((n - 1 - c) * chunk, chunk), chunk)
        sb_ref[...], _ = _chunk_update(None, k_ref[rl, :], v_ref[rl, :], gb_ref[rl, :],
                                       sb_ref[...], up, up_bf, True, False)
        return carry

    lax.fori_loop(0, n, body, 0)


def _gla_scan_kernel(q_ref, k_ref, v_ref, gf_ref, gb_ref, s0f_ref, s0b_ref, ng_ref,
                     o_ref, of_ref, ob_ref, sf_ref, sb_ref, *, chunk):
    t = q_ref.shape[0]
    n = t // chunk
    lo = _tri(chunk, False)
    up = _tri(chunk, True)
    lo_bf, up_bf = lo.astype(BF16), up.astype(BF16)
    sf_ref[...] = s0f_ref[...]
    sb_ref[...] = s0b_ref[...]

    def body(c, carry):
        sl = pl.ds(pl.multiple_of(c * chunk, chunk), chunk)
        sf_ref[...], of_ref[sl, :] = _chunk_update(
            q_ref[sl, :], k_ref[sl, :], v_ref[sl, :], gf_ref[sl, :],
            sf_ref[...], lo, lo_bf, False, True)
        rl = pl.ds(pl.multiple_of((n - 1 - c) * chunk, chunk), chunk)
        sb_ref[...], ob_ref[rl, :] = _chunk_update(
            q_ref[rl, :], k_ref[rl, :], v_ref[rl, :], gb_ref[rl, :],
            sb_ref[...], up, up_bf, True, True)
        return carry

    lax.fori_loop(0, n, body, 0)

    def norm(c, carry):
        sl = pl.ds(pl.multiple_of(c * chunk, chunk), chunk)
        o = of_ref[sl, :] + ob_ref[sl, :]
        o_ref[sl, :] = _rmsnorm(o, ng_ref[...]).astype(o_ref.dtype)
        return carry

    lax.fori_loop(0, n, norm, 0)


def _gla_state(k, v, g, chunk):
    bsz, t, kdim = k.shape
    vdim = v.shape[-1]
    nh = GLA_HEADS
    dk, dv = kdim // nh, vdim // nh
    st = jax.ShapeDtypeStruct((bsz, nh, dv, dk), F32)
    st_spec = pl.BlockSpec((None, None, dv, dk), lambda b, h: (b, h, 0, 0))
    return pl.pallas_call(
        functools.partial(_gla_state_kernel, chunk=chunk),
        out_shape=(st, st),
        grid=(bsz, nh),
        in_specs=[
            pl.BlockSpec((None, t, dk), lambda b, h: (b, 0, h)),
            pl.BlockSpec((None, t, dv), lambda b, h: (b, 0, h)),
            pl.BlockSpec((None, t, dk), lambda b, h: (b, 0, h)),
            pl.BlockSpec((None, t, dk), lambda b, h: (b, 0, nh + h)),
        ],
        out_specs=(st_spec, st_spec),
        compiler_params=_cparams(("arbitrary", "arbitrary")),
        name="gla_state",
    )(k, v, g, g)


def _gla_scan(q, k, v, g, s0f, s0b, norm_g, chunk):
    bsz, t, kdim = k.shape
    vdim = v.shape[-1]
    nh = GLA_HEADS
    dk, dv = kdim // nh, vdim // nh
    st_spec = pl.BlockSpec((None, None, dv, dk), lambda b, h: (b, h, 0, 0))
    kspec = pl.BlockSpec((None, t, dk), lambda b, h: (b, 0, h))
    vspec = pl.BlockSpec((None, t, dv), lambda b, h: (b, 0, h))
    return pl.pallas_call(
        functools.partial(_gla_scan_kernel, chunk=chunk),
        out_shape=jax.ShapeDtypeStruct((bsz, t, vdim), BF16),
        grid=(bsz, nh),
        in_specs=[
            kspec, kspec, vspec, kspec,
            pl.BlockSpec((None, t, dk), lambda b, h: (b, 0, nh + h)),
            st_spec, st_spec,
            pl.BlockSpec((1, dv), lambda b, h: (0, 0)),
        ],
        out_specs=vspec,
        scratch_shapes=[
            pltpu.VMEM((t, dv), F32), pltpu.VMEM((t, dv), F32),
            pltpu.VMEM((dv, dk), F32), pltpu.VMEM((dv, dk), F32),
        ],
        compiler_params=_cparams(("arbitrary", "arbitrary")),
        name="gla_scan",
    )(q, k, v, g, g, s0f, s0b, norm_g)


def _route(h2, wr_t, br, earlier_bf, count_ref):
    tm = h2.shape[0]
    logits = _dot3(wr_t, h2, _NT)
    scores = _sigmoid(logits)
    sel = scores + br
    row = [sel[e:e + 1, :] for e in range(N_EXPERTS)]
    picked = []
    for e in range(N_EXPERTS):
        g0 = (e // EXPERTS_PER_GROUP) * EXPERTS_PER_GROUP
        ahead_count = jnp.zeros(row[e].shape, jnp.int32)
        for j in range(g0, g0 + EXPERTS_PER_GROUP):
            if j == e:
                continue
            ahead = (row[j] >= row[e]) if j < e else (row[j] > row[e])
            ahead_count = ahead_count + ahead.astype(jnp.int32)
        picked.append(ahead_count < TOP_K)
    zero = jnp.zeros_like(row[0])
    gscore, pair, first_w, second_w = [], [], [], []
    for g in range(N_EXPERT_GROUPS):
        acc, pidx, fw, sw = zero, zero, zero, zero
        seen = None
        for a in range(EXPERTS_PER_GROUP):
            e = g * EXPERTS_PER_GROUP + a
            sc = scores[e:e + 1, :]
            acc = acc + jnp.where(picked[e], row[e], 0.0)
            if seen is None:
                is_first = picked[e]
            else:
                is_first = picked[e] & jnp.logical_not(seen)
                is_second = picked[e] & seen
                sw = sw + jnp.where(is_second, sc, 0.0)
                pidx = pidx + jnp.where(is_second, float(a), 0.0)
            fw = fw + jnp.where(is_first, sc, 0.0)
            if a < len(_PAIR_BASE):
                pidx = pidx + jnp.where(is_first, float(_PAIR_BASE[a] - a - 1), 0.0)
            seen = picked[e] if seen is None else (seen | picked[e])
        gscore.append(acc)
        pair.append(pidx)
        first_w.append(fw)
        second_w.append(sw)
    cls, wa, wb = zero, zero, zero
    for g in range(N_EXPERT_GROUPS):
        ok = None
        for j in range(N_EXPERT_GROUPS):
            if j == g:
                continue
            c = (gscore[g] > gscore[j]) if j < g else (gscore[g] >= gscore[j])
            ok = c if ok is None else (ok & c)
        cls = cls + jnp.where(ok, pair[g] + float(len(_PAIRS) * g), 0.0)
        wa = wa + jnp.where(ok, first_w[g], 0.0)
        wb = wb + jnp.where(ok, second_w[g], 0.0)
    denom = wa + wb
    wa = wa / denom
    wb = wb / denom

    cid = lax.broadcasted_iota(jnp.int32, (CLASS_ROWS, tm), 0).astype(F32)
    onehot = (cid == cls).astype(BF16)
    before = _dot(onehot, earlier_bf)
    oh = onehot.astype(F32)
    base = count_ref[...][:, 0:1]
    rank = jnp.sum(oh * (before + base), axis=0, keepdims=True)
    count_ref[...] = count_ref[...] + jnp.sum(oh, axis=1, keepdims=True)
    pad = jnp.zeros((INFO_LANES - 4, tm), F32)
    return jnp.concatenate([cls, rank, wa, wb, pad], axis=0)


def _earlier_matrix(tm):
    i = np.arange(tm)
    return jnp.asarray(i[:, None] < i[None, :], BF16)


def _pack_payload(h2, info):
    return jnp.concatenate([h2, jnp.transpose(info)], axis=1)


def _unpack_payload(pk, d):
    return pk[:, :d].astype(BF16), pk[:, d:]


def _route_outputs(h2, wr_ref, br_ref, earlier_ref, count_ref, pay_ref, info_ref, counts_ref):
    first = (pl.program_id(0) == 0) & (pl.program_id(1) == 0)

    @pl.when(first)
    def _():
        count_ref[...] = jnp.zeros_like(count_ref)

    info = _route(h2, wr_ref[...], br_ref[...], earlier_ref[...], count_ref)
    pay_ref[...] = _pack_payload(h2, info)
    info_ref[...] = info[0:8, :]
    counts_ref[...] = count_ref[...]


def _route_out_shapes(bsz, t, d):
    return (
        jax.ShapeDtypeStruct((bsz, t, d + INFO_LANES), F32),
        jax.ShapeDtypeStruct((8, bsz * t), F32),
        jax.ShapeDtypeStruct((CLASS_ROWS, 128), F32),
    )


def _route_out_specs(tm, nt, d):
    return (
        pl.BlockSpec((None, tm, d + INFO_LANES), lambda b, i, *_: (b, i, 0)),
        pl.BlockSpec((8, tm), lambda b, i, *_: (0, b * nt + i)),
        pl.BlockSpec((CLASS_ROWS, 128), lambda b, i, *_: (0, 0)),
    )


def _plan(info, counts, n, tmoe):
    cls = info[0].astype(jnp.int32)
    rank = info[1].astype(jnp.int32)
    cnt = counts[:N_CLASSES, 0].astype(jnp.int32)
    padded = (cnt + tmoe - 1) // tmoe * tmoe
    ends = jnp.cumsum(padded)
    dest = (ends - padded)[cls] + rank
    nt_max = n // tmoe + N_CLASSES
    src = jnp.zeros((nt_max * tmoe,), jnp.int32).at[dest].set(
        jnp.arange(n, dtype=jnp.int32), unique_indices=True)
    n_used = ends[-1] // tmoe
    tile = jnp.minimum(jnp.arange(nt_max, dtype=jnp.int32), n_used - 1)
    tcls = jnp.sum((ends[None, :] <= (tile * tmoe)[:, None]).astype(jnp.int32), axis=1)
    group, pair = tcls // len(_PAIRS), tcls % len(_PAIRS)
    pa = jnp.asarray([p[0] for p in _PAIRS], jnp.int32)[pair]
    pb = jnp.asarray([p[1] for p in _PAIRS], jnp.int32)[pair]
    ea = group * EXPERTS_PER_GROUP + pa
    eb = group * EXPERTS_PER_GROUP + pb
    return dest, src, ea, eb, n_used.reshape(1).astype(jnp.int32)


def _gather_start(idx_ref, src_hbm, buf, sem, tile, slot, tm):
    base = tile * tm

    def issue(blk, carry):
        for u in range(DMA_UNROLL):
            r = blk * DMA_UNROLL + u
            pltpu.make_async_copy(src_hbm.at[pl.ds(idx_ref[base + r], 1), :],
                                  buf.at[slot, pl.ds(r, 1), :], sem.at[slot]).start()
        return carry

    lax.fori_loop(0, tm // DMA_UNROLL, issue, 0)


def _gather_wait(src_hbm, buf, sem, slot, tm):
    pltpu.make_async_copy(src_hbm.at[pl.ds(0, tm), :], buf.at[slot], sem.at[slot]).wait()


def _gathered_tile(dest_ref, src_hbm, buf, sem, tm):
    nt = pl.num_programs(1)
    lin = pl.program_id(0) * nt + pl.program_id(1)
    total = pl.num_programs(0) * nt
    slot = lin % 2

    @pl.when(lin == 0)
    def _():
        _gather_start(dest_ref, src_hbm, buf, sem, lin, slot, tm)

    @pl.when(lin + 1 < total)
    def _():
        _gather_start(dest_ref, src_hbm, buf, sem, lin + 1, 1 - slot, tm)

    _gather_wait(src_hbm, buf, sem, slot, tm)
    return buf[slot]


def _post0_kernel(o_ref, r_ref, x_ref, g1_ref, sh2_ref, sc2_ref, n2_ref, wout_ref, wr_ref, br_ref,
                  earlier_ref, x1_ref, pay_ref, info_ref, counts_ref, count_ref):
    a = (o_ref[...].astype(F32) * _silu(r_ref[...].astype(F32))).astype(BF16)
    x1 = x_ref[...] + g1_ref[...] * _dot(a, wout_ref[...])
    x1_ref[...] = x1
    h2 = _rmsnorm(x1, n2_ref[...]) * (1.0 + sc2_ref[...]) + sh2_ref[...]
    _route_outputs(h2, wr_ref, br_ref, earlier_ref, count_ref, pay_ref, info_ref, counts_ref)


def _post0(o, r, x, mods4, layer, n2, wout, wr_t, br, tm):
    bsz, t, d = x.shape
    tm = min(tm, t)
    rb = lambda b: b
    tok = lambda w: pl.BlockSpec((None, tm, w), lambda b, i: (b, i, 0))
    full = lambda a: pl.BlockSpec(a.shape, lambda b, i: (0,) * a.ndim)
    nt = t // tm
    earlier = _earlier_matrix(tm)
    return pl.pallas_call(
        _post0_kernel,
        out_shape=(jax.ShapeDtypeStruct((bsz, t, d), F32),) + _route_out_shapes(bsz, t, d),
        grid=(bsz, nt),
        in_specs=[
            tok(d), tok(d), tok(d),
            _mod_spec(layer, 2, d, rb), _mod_spec(layer, 3, d, rb), _mod_spec(layer, 4, d, rb),
            full(n2), full(wout), full(wr_t), full(br), full(earlier),
        ],
        out_specs=(tok(d),) + _route_out_specs(tm, nt, d),
        scratch_shapes=[pltpu.VMEM((CLASS_ROWS, 128), F32)],
        compiler_params=_cparams(("arbitrary", "arbitrary")),
        name="post0",
    )(o, r, x, mods4, mods4, mods4, n2, wout, wr_t, br, earlier)


def _moe_kernel(src_ref, ea_ref, eb_ref, used_ref, p_hbm, w1a, w3a, w2a, w1b, w3b, w2b, f_ref,
                pbuf, psem, *, d, tm):
    del ea_ref, eb_ref
    j = pl.program_id(0)
    used = used_ref[0]
    slot = j % 2

    @pl.when(j == 0)
    def _():
        _gather_start(src_ref, p_hbm, pbuf, psem, j, slot, tm)

    @pl.when(j + 1 < used)
    def _():
        _gather_start(src_ref, p_hbm, pbuf, psem, j + 1, 1 - slot, tm)

    @pl.when(j < used)
    def _():
        _gather_wait(p_hbm, pbuf, psem, slot, tm)
        h, rec = _unpack_payload(pbuf[slot], d)

        def expert(w1, w3, w2):
            he = _silu(_dot(h, w1[...])) * _dot(h, w3[...])
            return _dot(he.astype(BF16), w2[...])

        f_ref[...] = (rec[:, _REC_WA:_REC_WA + 1] * expert(w1a, w3a, w2a)
                      + rec[:, _REC_WB:_REC_WB + 1] * expert(w1b, w3b, w2b))

    @pl.when(j >= used)
    def _():
        f_ref[...] = jnp.zeros_like(f_ref)


def _moe(payload, src, ea, eb, n_used, w1, w3, w2, tm):
    n_sorted = src.shape[0]
    w = payload.shape[1]
    _, d, de = w1.shape
    wa = lambda j, src, ea, eb, used: (ea[j], 0, 0)
    wb = lambda j, src, ea, eb, used: (eb[j], 0, 0)
    return pl.pallas_call(
        functools.partial(_moe_kernel, d=d, tm=tm),
        out_shape=jax.ShapeDtypeStruct((n_sorted, d), F32),
        grid_spec=pltpu.PrefetchScalarGridSpec(
            num_scalar_prefetch=4,
            grid=(n_sorted // tm,),
            in_specs=[
                pl.BlockSpec(memory_space=pl.ANY),
                pl.BlockSpec((None, d, de), wa), pl.BlockSpec((None, d, de), wa),
                pl.BlockSpec((None, de, d), wa),
                pl.BlockSpec((None, d, de), wb), pl.BlockSpec((None, d, de), wb),
                pl.BlockSpec((None, de, d), wb),
            ],
            out_specs=pl.BlockSpec((tm, d), lambda j, *_: (j, 0)),
            scratch_shapes=[pltpu.VMEM((2, tm, w), payload.dtype), pltpu.SemaphoreType.DMA((2,))],
        ),
        compiler_params=_cparams(("arbitrary",)),
        name="moe",
    )(src, ea, eb, n_used, payload, w1, w3, w2, w1, w3, w2)


def _sparse_moe(payload, info, counts, w1, w3, w2):
    bsz, t, w = payload.shape
    n = bsz * t
    tmoe = min(TM_MOE, n)
    dest, src, ea, eb, n_used = _plan(info, counts, n, tmoe)
    return _moe(payload.reshape(n, w), src, ea, eb, n_used, w1, w3, w2, tmoe), dest


def _pool_constants(tm, d):
    ng = len(POOL_WINDOWS)
    cg = d // ng
    pos = np.arange(tm)
    seg, off = pos // GRID_W, pos % GRID_W
    mats = np.zeros((ng, tm, tm), np.float32)
    inv = np.zeros((tm, d), np.float32)
    for gi, w in enumerate(POOL_WINDOWS):
        lo = np.clip(off - w // 2, 0, GRID_W)
        hi = np.clip(off - w // 2 + w, 0, GRID_W)
        same = seg[:, None] == seg[None, :]
        inside = (off[None, :] >= lo[:, None]) & (off[None, :] < hi[:, None])
        mats[gi] = (same & inside).astype(np.float32)
        inv[:, gi * cg:(gi + 1) * cg] = (1.0 / (hi - lo).astype(np.float32))[:, None]
    return jnp.asarray(mats, BF16), jnp.asarray(inv, F32)


def _layer1_kernel(dest_ref, x_ref, fs_hbm, g2p_ref, sh1_ref, sc1_ref, g1_ref, sh2_ref, sc2_ref,
                   n1_ref, n2_ref, pm_ref, inv_ref, wp_ref, bp_ref, ps_ref, wr_ref, br_ref, earlier_ref,
                   x3_ref, pay_ref, info_ref, counts_ref, fbuf, fsem, count_ref, *, tm):
    f = _gathered_tile(dest_ref, fs_hbm, fbuf, fsem, tm)
    x2 = x_ref[...] + g2p_ref[...] * f
    h = _rmsnorm(x2, n1_ref[...]) * (1.0 + sc1_ref[...]) + sh1_ref[...]
    hb = h.astype(BF16)
    ng = pm_ref.shape[0]
    cg = h.shape[1] // ng
    ys = []
    for gi in range(ng):
        cs = slice(gi * cg, (gi + 1) * cg)
        wsum = _dot(pm_ref[gi], hb[:, cs])
        pooled = wsum * inv_ref[:, cs] - h[:, cs]
        ys.append(_dot(pooled.astype(BF16), wp_ref[gi]))
    y = (jnp.concatenate(ys, axis=1) + bp_ref[...]) * ps_ref[...]
    x3 = x2 + g1_ref[...] * y
    x3_ref[...] = x3
    h2 = _rmsnorm(x3, n2_ref[...]) * (1.0 + sc2_ref[...]) + sh2_ref[...]
    _route_outputs(h2, wr_ref, br_ref, earlier_ref, count_ref, pay_ref, info_ref, counts_ref)


def _layer1(x1, f_sorted, dest, mods4, n1, n2, wp, bp, ps, wr_t, br, tm):
    bsz, t, d = x1.shape
    tm = min(tm, t)
    pm, inv = _pool_constants(tm, d)
    earlier = _earlier_matrix(tm)
    rb = lambda b: b
    tok = lambda w: pl.BlockSpec((None, tm, w), lambda b, i, dest: (b, i, 0))
    full = lambda a: pl.BlockSpec(a.shape, lambda b, i, dest: (0,) * a.ndim)
    nt = t // tm
    return pl.pallas_call(
        functools.partial(_layer1_kernel, tm=tm),
        out_shape=(jax.ShapeDtypeStruct((bsz, t, d), F32),) + _route_out_shapes(bsz, t, d),
        grid_spec=pltpu.PrefetchScalarGridSpec(
            num_scalar_prefetch=1,
            grid=(bsz, nt),
            in_specs=[
                tok(d), pl.BlockSpec(memory_space=pl.ANY),
                _mod_spec(0, 5, d, rb),
                _mod_spec(1, 0, d, rb), _mod_spec(1, 1, d, rb), _mod_spec(1, 2, d, rb),
                _mod_spec(1, 3, d, rb), _mod_spec(1, 4, d, rb),
                full(n1), full(n2), full(pm), full(inv), full(wp), full(bp), full(ps),
                full(wr_t), full(br), full(earlier),
            ],
            out_specs=(tok(d),) + _route_out_specs(tm, nt, d),
            scratch_shapes=[pltpu.VMEM((2, tm, d), F32), pltpu.SemaphoreType.DMA((2,)),
                            pltpu.VMEM((CLASS_ROWS, 128), F32)],
        ),
        compiler_params=_cparams(("arbitrary", "arbitrary")),
        name="layer1",
    )(dest, x1, f_sorted, mods4, mods4, mods4, mods4, mods4, mods4, n1, n2, pm, inv, wp, bp, ps,
      wr_t, br, earlier)


def _final_kernel(dest_ref, x_ref, fs_hbm, g2_ref, fg_ref, o_ref, fbuf, fsem, *, tm):
    f = _gathered_tile(dest_ref, fs_hbm, fbuf, fsem, tm)
    o_ref[...] = _rmsnorm(x_ref[...] + g2_ref[...] * f, fg_ref[...])


def _final(x3, f_sorted, dest, mods4, fg, tm):
    bsz, t, d = x3.shape
    tm = min(tm, t)
    tok = pl.BlockSpec((None, tm, d), lambda b, i, dest: (b, i, 0))
    return pl.pallas_call(
        functools.partial(_final_kernel, tm=tm),
        out_shape=jax.ShapeDtypeStruct((bsz, t, d), F32),
        grid_spec=pltpu.PrefetchScalarGridSpec(
            num_scalar_prefetch=1,
            grid=(bsz, t // tm),
            in_specs=[tok, pl.BlockSpec(memory_space=pl.ANY), _mod_spec(1, 5, d, lambda b: b),
                      pl.BlockSpec((1, d), lambda b, i, dest: (0, 0))],
            out_specs=tok,
            scratch_shapes=[pltpu.VMEM((2, tm, d), F32), pltpu.SemaphoreType.DMA((2,))],
        ),
        compiler_params=_cparams(("arbitrary", "arbitrary")),
        name="final",
    )(dest, x3, f_sorted, mods4, fg)


def kernel(x, c, ctx, c_ctx, norm1_g, norm2_g, w_mod, b_mod, gla_w_in, gla_w_gate_a, gla_w_gate_b,
           gla_b_gate, gla_norm_g, gla_w_out, pool_w, pool_b, pool_scale, w_router, b_router,
           w_gate_e, w_up_e, w_down_e, final_g):
    bsz, t, d = x.shape
    depth = w_mod.shape[0]
    assert depth == 2 and t % GRID_W == 0
    kdim = d // 2
    row = lambda a: a.reshape(1, -1)

    c_rows = -(-(bsz + 1) // 8) * 8
    c_all = jnp.zeros((c_rows, d), F32).at[:bsz].set(c).at[bsz].set(c_ctx)
    mods = _modulation(c_all, w_mod, b_mod)
    mods4 = mods.reshape(depth, c_rows, 1, 6 * d)

    win = gla_w_in[0].astype(BF16)
    wga = jnp.concatenate([gla_w_gate_a[0, 0], gla_w_gate_a[0, 1]], axis=1).astype(BF16)
    zero = jnp.zeros((GLA_GATE_RANK, kdim), F32)
    wgb = jnp.concatenate([
        jnp.concatenate([gla_w_gate_b[0, 0], zero], axis=1),
        jnp.concatenate([zero, gla_w_gate_b[0, 1]], axis=1)], axis=0).astype(BF16)
    bg = gla_b_gate[0].reshape(1, 2 * kdim)
    n1_0 = row(norm1_g[0])
    q, k, v, r, g = _gla_in(x, mods4, 0, lambda b: b, n1_0, win, wga, wgb, bg, TM_PROJ)
    _, kc, vc, _, gc = _gla_in(ctx, mods4, 0, lambda b: bsz, n1_0, win, wga, wgb, bg, TM_CTX)
    s0f, s0b = _gla_state(kc, vc, gc, GLA_CHUNK)
    o = _gla_scan(q, k, v, g, s0f, s0b, row(gla_norm_g[0]), GLA_CHUNK)

    wr_t = jnp.transpose(w_router)
    br = b_router.reshape(N_EXPERTS, 1)
    x1, payload, info, counts = _post0(o, r, x, mods4, 0, row(norm2_g[0]),
                                       gla_w_out[0].astype(BF16), wr_t, br, TM_POST)
    f0, dest0 = _sparse_moe(payload, info, counts, w_gate_e[0].astype(BF16),
                            w_up_e[0].astype(BF16), w_down_e[0].astype(BF16))

    x3, payload, info, counts = _layer1(x1, f0, dest0, mods4, row(norm1_g[1]), row(norm2_g[1]),
                                        pool_w[0].astype(BF16), row(pool_b[0]), row(pool_scale[0]),
                                        wr_t, br, TM_POOL)
    f1, dest1 = _sparse_moe(payload, info, counts, w_gate_e[1].astype(BF16),
                            w_up_e[1].astype(BF16), w_down_e[1].astype(BF16))
    return _final(x3, f1, dest1, mods4, row(final_g), TM_FINAL)
```

```python
import functools

import numpy as np
import jax
import jax.numpy as jnp
from jax import lax
from jax.experimental import pallas as pl
from jax.experimental.pallas import tpu as pltpu

EPS = 1e-6
GRID_W = 64
GLA_HEADS = 4
GLA_GATE_RANK = 16
GLA_GATE_NORM = 16.0
GLA_CHUNK = 64
GLA_BLOCK = 256
POOL_WINDOWS = (2, 4, 8, 16)
N_EXPERTS = 16
N_EXPERT_GROUPS = 4
EXPERTS_PER_GROUP = N_EXPERTS // N_EXPERT_GROUPS
TOP_K = 2

_PAIRS = tuple((a, b) for a in range(EXPERTS_PER_GROUP) for b in range(a + 1, EXPERTS_PER_GROUP))
_PAIR_BASE = (0, 3, 5)
N_CLASSES = N_EXPERT_GROUPS * len(_PAIRS)
CLASS_ROWS = 32
INFO_LANES = 128
_REC_WA, _REC_WB = 2, 3

TM_PROJ = 512
TM_CTX = 256
TM_POST = 512
TM_POOL = 256
TM_MOE = 256
TM_FINAL = 512
DMA_UNROLL = 8

F32 = jnp.float32
BF16 = jnp.bfloat16

_NT = (((1,), (1,)), ((), ()))
_TN = (((0,), (0,)), ((), ()))
_VMEM_LIMIT = 56 * 1024 * 1024


def _cparams(sem):
    return pltpu.CompilerParams(dimension_semantics=sem, vmem_limit_bytes=_VMEM_LIMIT)


def _dot(a, b, dims=None):
    if dims is None:
        return jnp.dot(a, b, preferred_element_type=F32)
    return lax.dot_general(a, b, dims, preferred_element_type=F32)


def _split(a):
    hi = a.astype(BF16)
    lo = (a - hi.astype(F32)).astype(BF16)
    return hi, lo


def _dot3(a, b, dims=None):
    ah, al = _split(a)
    bh, bl = _split(b)
    return _dot(ah, bh, dims) + _dot(ah, bl, dims) + _dot(al, bh, dims)


def _sigmoid(x):
    return 1.0 / (1.0 + jnp.exp(-x))


def _silu(x):
    return x * _sigmoid(x)


def _rmsnorm(xf, g):
    return xf * lax.rsqrt(jnp.mean(xf * xf, axis=-1, keepdims=True) + EPS) * g


def _mod_kernel(c_ref, w_ref, b_ref, o_ref):
    o_ref[...] = _dot3(_silu(c_ref[...]), w_ref[...]) + b_ref[...]


def _modulation(c_all, w_mod, b_mod):
    depth, d, d6 = w_mod.shape
    rows = c_all.shape[0]
    tn = 1536
    return pl.pallas_call(
        _mod_kernel,
        out_shape=jax.ShapeDtypeStruct((depth, rows, d6), F32),
        grid=(depth, d6 // tn),
        in_specs=[
            pl.BlockSpec((rows, d), lambda l, j: (0, 0)),
            pl.BlockSpec((None, d, tn), lambda l, j: (l, 0, j)),
            pl.BlockSpec((None, 1, tn), lambda l, j: (l, 0, j)),
        ],
        out_specs=pl.BlockSpec((None, rows, tn), lambda l, j: (l, 0, j)),
        compiler_params=_cparams(("arbitrary", "arbitrary")),
        name="mod",
    )(c_all, w_mod, b_mod.reshape(depth, 1, d6))


def _mod_spec(layer, chunk, d, row_of_batch):
    return pl.BlockSpec((None, None, 1, d), lambda b, t, *_: (layer, row_of_batch(b), 0, chunk))


def _gla_in_kernel(x_ref, sh_ref, sc_ref, ng_ref, win_ref, wga_ref, wgb_ref, bg_ref,
                   q_ref, k_ref, v_ref, r_ref, g_ref, *, kdim, vdim, qscale):
    h = _rmsnorm(x_ref[...], ng_ref[...]) * (1.0 + sc_ref[...]) + sh_ref[...]
    hb = h.astype(BF16)
    proj = _dot(hb, win_ref[...])
    q_ref[...] = (proj[:, :kdim] * qscale).astype(BF16)
    k_ref[...] = proj[:, kdim:2 * kdim].astype(BF16)
    v_ref[...] = proj[:, 2 * kdim:2 * kdim + vdim].astype(BF16)
    r_ref[...] = proj[:, 2 * kdim + vdim:].astype(BF16)
    low = _dot(hb, wga_ref[...])
    z = _dot(low.astype(BF16), wgb_ref[...]) + bg_ref[...]
    logsig = jnp.minimum(z, 0.0) - jnp.log(1.0 + jnp.exp(-jnp.abs(z)))
    g_ref[...] = logsig * (1.0 / GLA_GATE_NORM)


def _gla_in(x, mods4, layer, row_of_batch, ng, win, wga, wgb, bg, tm):
    bsz, t, d = x.shape
    kdim, vdim = d // 2, d
    tm = min(tm, t)
    tok = lambda w: pl.BlockSpec((None, tm, w), lambda b, i: (b, i, 0))
    full = lambda a: pl.BlockSpec(a.shape, lambda b, i: (0,) * a.ndim)
    kern = functools.partial(_gla_in_kernel, kdim=kdim, vdim=vdim,
                             qscale=float((kdim // GLA_HEADS) ** -0.5))
    return pl.pallas_call(
        kern,
        out_shape=(
            jax.ShapeDtypeStruct((bsz, t, kdim), BF16),
            jax.ShapeDtypeStruct((bsz, t, kdim), BF16),
            jax.ShapeDtypeStruct((bsz, t, vdim), BF16),
            jax.ShapeDtypeStruct((bsz, t, vdim), BF16),
            jax.ShapeDtypeStruct((bsz, t, 2 * kdim), F32),
        ),
        grid=(bsz, t // tm),
        in_specs=[
            tok(d),
            _mod_spec(layer, 0, d, row_of_batch),
            _mod_spec(layer, 1, d, row_of_batch),
            full(ng), full(win), full(wga), full(wgb), full(bg),
        ],
        out_specs=(tok(kdim), tok(kdim), tok(vdim), tok(vdim), tok(2 * kdim)),
        compiler_params=_cparams(("arbitrary", "arbitrary")),
        name="gla_in",
    )(x, mods4, mods4, ng, win, wga, wgb, bg)


def _block_masks(rows, chunk):
    i = np.arange(rows)
    same = (i[:, None] // chunk) == (i[None, :] // chunk)
    lower = same & (i[:, None] >= i[None, :])
    upper = same & (i[:, None] <= i[None, :])
    return jnp.asarray(lower, BF16), jnp.asarray(upper, BF16)


def _per_chunk_row(a, chunk, r):
    rows, w = a.shape
    parts = [jnp.broadcast_to(a[c * chunk + r:c * chunk + r + 1, :], (chunk, w))
             for c in range(rows // chunk)]
    return jnp.concatenate(parts, axis=0)


def _block_terms(q, k, v, g, bd, reverse, chunk, want_out):
    gh, gl = _split(g)
    cum = _dot(bd, gh) + _dot(bd, gl)
    tot = _per_chunk_row(cum, chunk, 0 if reverse else chunk - 1)
    kf = k.astype(F32)
    kl = (kf * jnp.exp(tot - cum)).astype(BF16)
    etot = jnp.exp(tot)
    if not want_out:
        return kl, etot, None, None
    mid = _per_chunk_row(cum, chunk, chunk // 2)
    qf = q.astype(F32)
    qe = (qf * jnp.exp(cum)).astype(BF16)
    qi = (qf * jnp.exp(cum - mid)).astype(BF16)
    ki = (kf * jnp.exp(mid - cum)).astype(BF16)
    s = jnp.where(bd > 0, _dot(qi, ki, _NT), 0.0)
    return kl, etot, qe, _dot(s.astype(BF16), v)


def _store_chunk_states(v, kl, etot, ds_ref, e_ref, first_chunk, chunk):
    for c in range(v.shape[0] // chunk):
        rows = slice(c * chunk, (c + 1) * chunk)
        ds_ref[first_chunk + c] = _dot(v[rows, :], kl[rows, :], _TN)
        e_ref[first_chunk + c] = etot[c * chunk:c * chunk + 8, :]


def _gla_state_kernel(k_ref, v_ref, gf_ref, gb_ref, lo_ref, up_ref, sf_ref, sb_ref,
                      dsf_ref, dsb_ref, ef_ref, eb_ref, *, chunk, block):
    t = k_ref.shape[0]
    n = t // chunk
    per = block // chunk

    def terms(i, carry):
        rows = pl.ds(pl.multiple_of(i * block, block), block)
        k, v = k_ref[rows, :], v_ref[rows, :]
        kl, etot, _, _ = _block_terms(None, k, v, gf_ref[rows, :], lo_ref[...], False, chunk, False)
        _store_chunk_states(v, kl, etot, dsf_ref, ef_ref, i * per, chunk)
        kl, etot, _, _ = _block_terms(None, k, v, gb_ref[rows, :], up_ref[...], True, chunk, False)
        _store_chunk_states(v, kl, etot, dsb_ref, eb_ref, i * per, chunk)
        return carry

    lax.fori_loop(0, t // block, terms, 0)

    def scan(c, carry):
        sf, sb = carry
        r = n - 1 - c
        return (sf * ef_ref[c][0:1, :] + dsf_ref[c], sb * eb_ref[r][0:1, :] + dsb_ref[r])

    zero = jnp.zeros(sf_ref.shape, F32)
    sf, sb = lax.fori_loop(0, n, scan, (zero, zero))
    sf_ref[...] = sf
    sb_ref[...] = sb


def _gla_scan_kernel(q_ref, k_ref, v_ref, gf_ref, gb_ref, s0f_ref, s0b_ref, ng_ref, lo_ref, up_ref,
                     o_ref, oi_ref, qef_ref, qeb_ref, dsf_ref, dsb_ref, ef_ref, eb_ref,
                     scf_ref, scb_ref, *, chunk, block):
    t = q_ref.shape[0]
    n = t // chunk
    per = block // chunk

    def terms(i, carry):
        rows = pl.ds(pl.multiple_of(i * block, block), block)
        q, k, v = q_ref[rows, :], k_ref[rows, :], v_ref[rows, :]
        kl, etot, qe, of = _block_terms(q, k, v, gf_ref[rows, :], lo_ref[...], False, chunk, True)
        _store_chunk_states(v, kl, etot, dsf_ref, ef_ref, i * per, chunk)
        qef_ref[rows, :] = qe
        kl, etot, qe, ob = _block_terms(q, k, v, gb_ref[rows, :], up_ref[...], True, chunk, True)
        _store_chunk_states(v, kl, etot, dsb_ref, eb_ref, i * per, chunk)
        qeb_ref[rows, :] = qe
        oi_ref[rows, :] = of + ob
        return carry

    lax.fori_loop(0, t // block, terms, 0)

    def scan(c, carry):
        sf, sb = carry
        r = n - 1 - c
        scf_ref[c] = sf.astype(BF16)
        scb_ref[r] = sb.astype(BF16)
        return (sf * ef_ref[c][0:1, :] + dsf_ref[c], sb * eb_ref[r][0:1, :] + dsb_ref[r])

    lax.fori_loop(0, n, scan, (s0f_ref[...], s0b_ref[...]))

    def finish(i, carry):
        for c in range(per):
            rows = pl.ds(pl.multiple_of(i * block + c * chunk, chunk), chunk)
            o = (oi_ref[rows, :] + _dot(qef_ref[rows, :], scf_ref[i * per + c], _NT)
                 + _dot(qeb_ref[rows, :], scb_ref[i * per + c], _NT))
            o_ref[rows, :] = _rmsnorm(o, ng_ref[...]).astype(o_ref.dtype)
        return carry

    lax.fori_loop(0, t // block, finish, 0)


def _gla_state(k, v, g, chunk, block):
    bsz, t, kdim = k.shape
    vdim = v.shape[-1]
    nh = GLA_HEADS
    dk, dv = kdim // nh, vdim // nh
    block = min(block, t)
    n = t // chunk
    lo, up = _block_masks(block, chunk)
    st = jax.ShapeDtypeStruct((bsz, nh, dv, dk), F32)
    st_spec = pl.BlockSpec((None, None, dv, dk), lambda b, h: (b, h, 0, 0))
    mask_spec = pl.BlockSpec((block, block), lambda b, h: (0, 0))
    return pl.pallas_call(
        functools.partial(_gla_state_kernel, chunk=chunk, block=block),
        out_shape=(st, st),
        grid=(bsz, nh),
        in_specs=[
            pl.BlockSpec((None, t, dk), lambda b, h: (b, 0, h)),
            pl.BlockSpec((None, t, dv), lambda b, h: (b, 0, h)),
            pl.BlockSpec((None, t, dk), lambda b, h: (b, 0, h)),
            pl.BlockSpec((None, t, dk), lambda b, h: (b, 0, nh + h)),
            mask_spec, mask_spec,
        ],
        out_specs=(st_spec, st_spec),
        scratch_shapes=[
            pltpu.VMEM((n, dv, dk), F32), pltpu.VMEM((n, dv, dk), F32),
            pltpu.VMEM((n, 8, dk), F32), pltpu.VMEM((n, 8, dk), F32),
        ],
        compiler_params=_cparams(("arbitrary", "arbitrary")),
        name="gla_state",
    )(k, v, g, g, lo, up)


def _gla_scan(q, k, v, g, s0f, s0b, norm_g, chunk, block):
    bsz, t, kdim = k.shape
    vdim = v.shape[-1]
    nh = GLA_HEADS
    dk, dv = kdim // nh, vdim // nh
    block = min(block, t)
    n = t // chunk
    lo, up = _block_masks(block, chunk)
    st_spec = pl.BlockSpec((None, None, dv, dk), lambda b, h: (b, h, 0, 0))
    kspec = pl.BlockSpec((None, t, dk), lambda b, h: (b, 0, h))
    vspec = pl.BlockSpec((None, t, dv), lambda b, h: (b, 0, h))
    mask_spec = pl.BlockSpec((block, block), lambda b, h: (0, 0))
    return pl.pallas_call(
        functools.partial(_gla_scan_kernel, chunk=chunk, block=block),
        out_shape=jax.ShapeDtypeStruct((bsz, t, vdim), BF16),
        grid=(bsz, nh),
        in_specs=[
            kspec, kspec, vspec, kspec,
            pl.BlockSpec((None, t, dk), lambda b, h: (b, 0, nh + h)),
            st_spec, st_spec,
            pl.BlockSpec((1, dv), lambda b, h: (0, 0)),
            mask_spec, mask_spec,
        ],
        out_specs=vspec,
        scratch_shapes=[
            pltpu.VMEM((t, dv), F32),
            pltpu.VMEM((t, dk), BF16), pltpu.VMEM((t, dk), BF16),
            pltpu.VMEM((n, dv, dk), F32), pltpu.VMEM((n, dv, dk), F32),
            pltpu.VMEM((n, 8, dk), F32), pltpu.VMEM((n, 8, dk), F32),
            pltpu.VMEM((n, dv, dk), BF16), pltpu.VMEM((n, dv, dk), BF16),
        ],
        compiler_params=_cparams(("arbitrary", "arbitrary")),
        name="gla_scan",
    )(q, k, v, g, g, s0f, s0b, norm_g, lo, up)


def _route(h2, wr_t, br, earlier_bf, count_ref):
    tm = h2.shape[0]
    logits = _dot3(wr_t, h2, _NT)
    scores = _sigmoid(logits)
    sel = scores + br
    row = [sel[e:e + 1, :] for e in range(N_EXPERTS)]
    picked = []
    for e in range(N_EXPERTS):
        g0 = (e // EXPERTS_PER_GROUP) * EXPERTS_PER_GROUP
        ahead_count = jnp.zeros(row[e].shape, jnp.int32)
        for j in range(g0, g0 + EXPERTS_PER_GROUP):
            if j == e:
                continue
            ahead = (row[j] >= row[e]) if j < e else (row[j] > row[e])
            ahead_count = ahead_count + ahead.astype(jnp.int32)
        picked.append(ahead_count < TOP_K)
    zero = jnp.zeros_like(row[0])
    gscore, pair, first_w, second_w = [], [], [], []
    for g in range(N_EXPERT_GROUPS):
        acc, pidx, fw, sw = zero, zero, zero, zero
        seen = None
        for a in range(EXPERTS_PER_GROUP):
            e = g * EXPERTS_PER_GROUP + a
            sc = scores[e:e + 1, :]
            acc = acc + jnp.where(picked[e], row[e], 0.0)
            if seen is None:
                is_first = picked[e]
            else:
                is_first = picked[e] & jnp.logical_not(seen)
                is_second = picked[e] & seen
                sw = sw + jnp.where(is_second, sc, 0.0)
                pidx = pidx + jnp.where(is_second, float(a), 0.0)
            fw = fw + jnp.where(is_first, sc, 0.0)
            if a < len(_PAIR_BASE):
                pidx = pidx + jnp.where(is_first, float(_PAIR_BASE[a] - a - 1), 0.0)
            seen = picked[e] if seen is None else (seen | picked[e])
        gscore.append(acc)
        pair.append(pidx)
        first_w.append(fw)
        second_w.append(sw)
    cls, wa, wb = zero, zero, zero
    for g in range(N_EXPERT_GROUPS):
        ok = None
        for j in range(N_EXPERT_GROUPS):
            if j == g:
                continue
            c = (gscore[g] > gscore[j]) if j < g else (gscore[g] >= gscore[j])
            ok = c if ok is None else (ok & c)
        cls = cls + jnp.where(ok, pair[g] + float(len(_PAIRS) * g), 0.0)
        wa = wa + jnp.where(ok, first_w[g], 0.0)
        wb = wb + jnp.where(ok, second_w[g], 0.0)
    denom = wa + wb
    wa = wa / denom
    wb = wb / denom

    cid = lax.broadcasted_iota(jnp.int32, (CLASS_ROWS, tm), 0).astype(F32)
    onehot = (cid == cls).astype(BF16)
    before = _dot(onehot, earlier_bf)
    oh = onehot.astype(F32)
    base = count_ref[...][:, 0:1]
    rank = jnp.sum(oh * (before + base), axis=0, keepdims=True)
    count_ref[...] = count_ref[...] + jnp.sum(oh, axis=1, keepdims=True)
    pad = jnp.zeros((INFO_LANES - 4, tm), F32)
    return jnp.concatenate([cls, rank, wa, wb, pad], axis=0)


def _earlier_matrix(tm):
    i = np.arange(tm)
    return jnp.asarray(i[:, None] < i[None, :], BF16)


def _pack_payload(h2, info):
    return jnp.concatenate([h2, jnp.transpose(info)], axis=1)


def _unpack_payload(pk, d):
    return pk[:, :d].astype(BF16), pk[:, d:]


def _route_outputs(h2, wr_ref, br_ref, earlier_ref, count_ref, pay_ref, info_ref, counts_ref):
    first = (pl.program_id(0) == 0) & (pl.program_id(1) == 0)

    @pl.when(first)
    def _():
        count_ref[...] = jnp.zeros_like(count_ref)

    info = _route(h2, wr_ref[...], br_ref[...], earlier_ref[...], count_ref)
    pay_ref[...] = _pack_payload(h2, info)
    info_ref[...] = info[0:8, :]
    counts_ref[...] = count_ref[...]


def _route_out_shapes(bsz, t, d):
    return (
        jax.ShapeDtypeStruct((bsz, t, d + INFO_LANES), F32),
        jax.ShapeDtypeStruct((8, bsz * t), F32),
        jax.ShapeDtypeStruct((CLASS_ROWS, 128), F32),
    )


def _route_out_specs(tm, nt, d):
    return (
        pl.BlockSpec((None, tm, d + INFO_LANES), lambda b, i, *_: (b, i, 0)),
        pl.BlockSpec((8, tm), lambda b, i, *_: (0, b * nt + i)),
        pl.BlockSpec((CLASS_ROWS, 128), lambda b, i, *_: (0, 0)),
    )


def _plan(info, counts, n, tmoe):
    cls = info[0].astype(jnp.int32)
    rank = info[1].astype(jnp.int32)
    cnt = counts[:N_CLASSES, 0].astype(jnp.int32)
    padded = (cnt + tmoe - 1) // tmoe * tmoe
    ends = jnp.cumsum(padded)
    dest = (ends - padded)[cls] + rank
    nt_max = n // tmoe + N_CLASSES
    src = jnp.zeros((nt_max * tmoe,), jnp.int32).at[dest].set(
        jnp.arange(n, dtype=jnp.int32), unique_indices=True)
    n_used = ends[-1] // tmoe
    tile = jnp.minimum(jnp.arange(nt_max, dtype=jnp.int32), n_used - 1)
    tcls = jnp.sum((ends[None, :] <= (tile * tmoe)[:, None]).astype(jnp.int32), axis=1)
    group, pair = tcls // len(_PAIRS), tcls % len(_PAIRS)
    pa = jnp.asarray([p[0] for p in _PAIRS], jnp.int32)[pair]
    pb = jnp.asarray([p[1] for p in _PAIRS], jnp.int32)[pair]
    ea = group * EXPERTS_PER_GROUP + pa
    eb = group * EXPERTS_PER_GROUP + pb
    return dest, src, ea, eb, n_used.reshape(1).astype(jnp.int32)


def _gather_start(idx_ref, src_hbm, buf, sem, tile, slot, tm):
    base = tile * tm

    def issue(blk, carry):
        for u in range(DMA_UNROLL):
            r = blk * DMA_UNROLL + u
            pltpu.make_async_copy(src_hbm.at[pl.ds(idx_ref[base + r], 1), :],
                                  buf.at[slot, pl.ds(r, 1), :], sem.at[slot]).start()
        return carry

    lax.fori_loop(0, tm // DMA_UNROLL, issue, 0)


def _gather_wait(src_hbm, buf, sem, slot, tm):
    pltpu.make_async_copy(src_hbm.at[pl.ds(0, tm), :], buf.at[slot], sem.at[slot]).wait()


def _gathered_tile(dest_ref, src_hbm, buf, sem, tm):
    nt = pl.num_programs(1)
    lin = pl.program_id(0) * nt + pl.program_id(1)
    total = pl.num_programs(0) * nt
    slot = lin % 2

    @pl.when(lin == 0)
    def _():
        _gather_start(dest_ref, src_hbm, buf, sem, lin, slot, tm)

    @pl.when(lin + 1 < total)
    def _():
        _gather_start(dest_ref, src_hbm, buf, sem, lin + 1, 1 - slot, tm)

    _gather_wait(src_hbm, buf, sem, slot, tm)
    return buf[slot]


def _post0_kernel(o_ref, r_ref, x_ref, g1_ref, sh2_ref, sc2_ref, n2_ref, wout_ref, wr_ref, br_ref,
                  earlier_ref, x1_ref, pay_ref, info_ref, counts_ref, count_ref):
    a = (o_ref[...].astype(F32) * _silu(r_ref[...].astype(F32))).astype(BF16)
    x1 = x_ref[...] + g1_ref[...] * _dot(a, wout_ref[...])
    x1_ref[...] = x1
    h2 = _rmsnorm(x1, n2_ref[...]) * (1.0 + sc2_ref[...]) + sh2_ref[...]
    _route_outputs(h2, wr_ref, br_ref, earlier_ref, count_ref, pay_ref, info_ref, counts_ref)


def _post0(o, r, x, mods4, layer, n2, wout, wr_t, br, tm):
    bsz, t, d = x.shape
    tm = min(tm, t)
    rb = lambda b: b
    tok = lambda w: pl.BlockSpec((None, tm, w), lambda b, i: (b, i, 0))
    full = lambda a: pl.BlockSpec(a.shape, lambda b, i: (0,) * a.ndim)
    nt = t // tm
    earlier = _earlier_matrix(tm)
    return pl.pallas_call(
        _post0_kernel,
        out_shape=(jax.ShapeDtypeStruct((bsz, t, d), F32),) + _route_out_shapes(bsz, t, d),
        grid=(bsz, nt),
        in_specs=[
            tok(d), tok(d), tok(d),
            _mod_spec(layer, 2, d, rb), _mod_spec(layer, 3, d, rb), _mod_spec(layer, 4, d, rb),
            full(n2), full(wout), full(wr_t), full(br), full(earlier),
        ],
        out_specs=(tok(d),) + _route_out_specs(tm, nt, d),
        scratch_shapes=[pltpu.VMEM((CLASS_ROWS, 128), F32)],
        compiler_params=_cparams(("arbitrary", "arbitrary")),
        name="post0",
    )(o, r, x, mods4, mods4, mods4, n2, wout, wr_t, br, earlier)


def _moe_kernel(src_ref, ea_ref, eb_ref, used_ref, p_hbm, w1a, w3a, w2a, w1b, w3b, w2b, f_ref,
                pbuf, psem, *, d, tm):
    del ea_ref, eb_ref
    j = pl.program_id(0)
    used = used_ref[0]
    slot = j % 2

    @pl.when(j == 0)
    def _():
        _gather_start(src_ref, p_hbm, pbuf, psem, j, slot, tm)

    @pl.when(j + 1 < used)
    def _():
        _gather_start(src_ref, p_hbm, pbuf, psem, j + 1, 1 - slot, tm)

    @pl.when(j < used)
    def _():
        _gather_wait(p_hbm, pbuf, psem, slot, tm)
        h, rec = _unpack_payload(pbuf[slot], d)

        def expert(w1, w3, w2):
            he = _silu(_dot(h, w1[...])) * _dot(h, w3[...])
            return _dot(he.astype(BF16), w2[...])

        f_ref[...] = (rec[:, _REC_WA:_REC_WA + 1] * expert(w1a, w3a, w2a)
                      + rec[:, _REC_WB:_REC_WB + 1] * expert(w1b, w3b, w2b))

    @pl.when(j >= used)
    def _():
        f_ref[...] = jnp.zeros_like(f_ref)


def _moe(payload, src, ea, eb, n_used, w1, w3, w2, tm):
    n_sorted = src.shape[0]
    w = payload.shape[1]
    _, d, de = w1.shape
    wa = lambda j, src, ea, eb, used: (ea[j], 0, 0)
    wb = lambda j, src, ea, eb, used: (eb[j], 0, 0)
    return pl.pallas_call(
        functools.partial(_moe_kernel, d=d, tm=tm),
        out_shape=jax.ShapeDtypeStruct((n_sorted, d), F32),
        grid_spec=pltpu.PrefetchScalarGridSpec(
            num_scalar_prefetch=4,
            grid=(n_sorted // tm,),
            in_specs=[
                pl.BlockSpec(memory_space=pl.ANY),
                pl.BlockSpec((None, d, de), wa), pl.BlockSpec((None, d, de), wa),
                pl.BlockSpec((None, de, d), wa),
                pl.BlockSpec((None, d, de), wb), pl.BlockSpec((None, d, de), wb),
                pl.BlockSpec((None, de, d), wb),
            ],
            out_specs=pl.BlockSpec((tm, d), lambda j, *_: (j, 0)),
            scratch_shapes=[pltpu.VMEM((2, tm, w), payload.dtype), pltpu.SemaphoreType.DMA((2,))],
        ),
        compiler_params=_cparams(("arbitrary",)),
        name="moe",
    )(src, ea, eb, n_used, payload, w1, w3, w2, w1, w3, w2)


def _sparse_moe(payload, info, counts, w1, w3, w2):
    bsz, t, w = payload.shape
    n = bsz * t
    tmoe = min(TM_MOE, n)
    dest, src, ea, eb, n_used = _plan(info, counts, n, tmoe)
    return _moe(payload.reshape(n, w), src, ea, eb, n_used, w1, w3, w2, tmoe), dest


def _pool_constants(tm, d):
    ng = len(POOL_WINDOWS)
    cg = d // ng
    pos = np.arange(tm)
    seg, off = pos // GRID_W, pos % GRID_W
    mats = np.zeros((ng, tm, tm), np.float32)
    inv = np.zeros((tm, d), np.float32)
    for gi, w in enumerate(POOL_WINDOWS):
        lo = np.clip(off - w // 2, 0, GRID_W)
        hi = np.clip(off - w // 2 + w, 0, GRID_W)
        same = seg[:, None] == seg[None, :]
        inside = (off[None, :] >= lo[:, None]) & (off[None, :] < hi[:, None])
        mats[gi] = (same & inside).astype(np.float32)
        inv[:, gi * cg:(gi + 1) * cg] = (1.0 / (hi - lo).astype(np.float32))[:, None]
    return jnp.asarray(mats, BF16), jnp.asarray(inv, F32)


def _layer1_kernel(dest_ref, x_ref, fs_hbm, g2p_ref, sh1_ref, sc1_ref, g1_ref, sh2_ref, sc2_ref,
                   n1_ref, n2_ref, pm_ref, inv_ref, wp_ref, bp_ref, ps_ref, wr_ref, br_ref, earlier_ref,
                   x3_ref, pay_ref, info_ref, counts_ref, fbuf, fsem, count_ref, *, tm):
    f = _gathered_tile(dest_ref, fs_hbm, fbuf, fsem, tm)
    x2 = x_ref[...] + g2p_ref[...] * f
    h = _rmsnorm(x2, n1_ref[...]) * (1.0 + sc1_ref[...]) + sh1_ref[...]
    hb = h.astype(BF16)
    ng = pm_ref.shape[0]
    cg = h.shape[1] // ng
    ys = []
    for gi in range(ng):
        cs = slice(gi * cg, (gi + 1) * cg)
        wsum = _dot(pm_ref[gi], hb[:, cs])
        pooled = wsum * inv_ref[:, cs] - h[:, cs]
        ys.append(_dot(pooled.astype(BF16), wp_ref[gi]))
    y = (jnp.concatenate(ys, axis=1) + bp_ref[...]) * ps_ref[...]
    x3 = x2 + g1_ref[...] * y
    x3_ref[...] = x3
    h2 = _rmsnorm(x3, n2_ref[...]) * (1.0 + sc2_ref[...]) + sh2_ref[...]
    _route_outputs(h2, wr_ref, br_ref, earlier_ref, count_ref, pay_ref, info_ref, counts_ref)


def _layer1(x1, f_sorted, dest, mods4, n1, n2, wp, bp, ps, wr_t, br, tm):
    bsz, t, d = x1.shape
    tm = min(tm, t)
    pm, inv = _pool_constants(tm, d)
    earlier = _earlier_matrix(tm)
    rb = lambda b: b
    tok = lambda w: pl.BlockSpec((None, tm, w), lambda b, i, dest: (b, i, 0))
    full = lambda a: pl.BlockSpec(a.shape, lambda b, i, dest: (0,) * a.ndim)
    nt = t // tm
    return pl.pallas_call(
        functools.partial(_layer1_kernel, tm=tm),
        out_shape=(jax.ShapeDtypeStruct((bsz, t, d), F32),) + _route_out_shapes(bsz, t, d),
        grid_spec=pltpu.PrefetchScalarGridSpec(
            num_scalar_prefetch=1,
            grid=(bsz, nt),
            in_specs=[
                tok(d), pl.BlockSpec(memory_space=pl.ANY),
                _mod_spec(0, 5, d, rb),
                _mod_spec(1, 0, d, rb), _mod_spec(1, 1, d, rb), _mod_spec(1, 2, d, rb),
                _mod_spec(1, 3, d, rb), _mod_spec(1, 4, d, rb),
                full(n1), full(n2), full(pm), full(inv), full(wp), full(bp), full(ps),
                full(wr_t), full(br), full(earlier),
            ],
            out_specs=(tok(d),) + _route_out_specs(tm, nt, d),
            scratch_shapes=[pltpu.VMEM((2, tm, d), F32), pltpu.SemaphoreType.DMA((2,)),
                            pltpu.VMEM((CLASS_ROWS, 128), F32)],
        ),
        compiler_params=_cparams(("arbitrary", "arbitrary")),
        name="layer1",
    )(dest, x1, f_sorted, mods4, mods4, mods4, mods4, mods4, mods4, n1, n2, pm, inv, wp, bp, ps,
      wr_t, br, earlier)


def _final_kernel(dest_ref, x_ref, fs_hbm, g2_ref, fg_ref, o_ref, fbuf, fsem, *, tm):
    f = _gathered_tile(dest_ref, fs_hbm, fbuf, fsem, tm)
    o_ref[...] = _rmsnorm(x_ref[...] + g2_ref[...] * f, fg_ref[...])


def _final(x3, f_sorted, dest, mods4, fg, tm):
    bsz, t, d = x3.shape
    tm = min(tm, t)
    tok = pl.BlockSpec((None, tm, d), lambda b, i, dest: (b, i, 0))
    return pl.pallas_call(
        functools.partial(_final_kernel, tm=tm),
        out_shape=jax.ShapeDtypeStruct((bsz, t, d), F32),
        grid_spec=pltpu.PrefetchScalarGridSpec(
            num_scalar_prefetch=1,
            grid=(bsz, t // tm),
            in_specs=[tok, pl.BlockSpec(memory_space=pl.ANY), _mod_spec(1, 5, d, lambda b: b),
                      pl.BlockSpec((1, d), lambda b, i, dest: (0, 0))],
            out_specs=tok,
            scratch_shapes=[pltpu.VMEM((2, tm, d), F32), pltpu.SemaphoreType.DMA((2,))],
        ),
        compiler_params=_cparams(("arbitrary", "arbitrary")),
        name="final",
    )(dest, x3, f_sorted, mods4, fg)


def kernel(x, c, ctx, c_ctx, norm1_g, norm2_g, w_mod, b_mod, gla_w_in, gla_w_gate_a, gla_w_gate_b,
           gla_b_gate, gla_norm_g, gla_w_out, pool_w, pool_b, pool_scale, w_router, b_router,
           w_gate_e, w_up_e, w_down_e, final_g):
    bsz, t, d = x.shape
    depth = w_mod.shape[0]
    assert depth == 2 and t % GRID_W == 0
    kdim = d // 2
    row = lambda a: a.reshape(1, -1)

    c_rows = -(-(bsz + 1) // 8) * 8
    c_all = jnp.zeros((c_rows, d), F32).at[:bsz].set(c).at[bsz].set(c_ctx)
    mods = _modulation(c_all, w_mod, b_mod)
    mods4 = mods.reshape(depth, c_rows, 1, 6 * d)

    win = gla_w_in[0].astype(BF16)
    wga = jnp.concatenate([gla_w_gate_a[0, 0], gla_w_gate_a[0, 1]], axis=1).astype(BF16)
    zero = jnp.zeros((GLA_GATE_RANK, kdim), F32)
    wgb = jnp.concatenate([
        jnp.concatenate([gla_w_gate_b[0, 0], zero], axis=1),
        jnp.concatenate([zero, gla_w_gate_b[0, 1]], axis=1)], axis=0).astype(BF16)
    bg = gla_b_gate[0].reshape(1, 2 * kdim)
    n1_0 = row(norm1_g[0])
    q, k, v, r, g = _gla_in(x, mods4, 0, lambda b: b, n1_0, win, wga, wgb, bg, TM_PROJ)
    _, kc, vc, _, gc = _gla_in(ctx, mods4, 0, lambda b: bsz, n1_0, win, wga, wgb, bg, TM_CTX)
    s0f, s0b = _gla_state(kc, vc, gc, GLA_CHUNK, GLA_BLOCK)
    o = _gla_scan(q, k, v, g, s0f, s0b, row(gla_norm_g[0]), GLA_CHUNK, GLA_BLOCK)

    wr_t = jnp.transpose(w_router)
    br = b_router.reshape(N_EXPERTS, 1)
    x1, payload, info, counts = _post0(o, r, x, mods4, 0, row(norm2_g[0]),
                                       gla_w_out[0].astype(BF16), wr_t, br, TM_POST)
    f0, dest0 = _sparse_moe(payload, info, counts, w_gate_e[0].astype(BF16),
                            w_up_e[0].astype(BF16), w_down_e[0].astype(BF16))

    x3, payload, info, counts = _layer1(x1, f0, dest0, mods4, row(norm1_g[1]), row(norm2_g[1]),
                                        pool_w[0].astype(BF16), row(pool_b[0]), row(pool_scale[0]),
                                        wr_t, br, TM_POOL)
    f1, dest1 = _sparse_moe(payload, info, counts, w_gate_e[1].astype(BF16),
                            w_up_e[1].astype(BF16), w_down_e[1].astype(BF16))
    return _final(x3, f1, dest1, mods4, row(final_g), TM_FINAL)
```

```python
import functools

import numpy as np
import jax
import jax.numpy as jnp
from jax import lax
from jax.experimental import pallas as pl
from jax.experimental.pallas import tpu as pltpu

EPS = 1e-6
GRID_W = 64
GLA_HEADS = 4
GLA_GATE_RANK = 16
GLA_GATE_NORM = 16.0
GLA_CHUNK = 64
GLA_BLOCK = 256
POOL_WINDOWS = (2, 4, 8, 16)
N_EXPERTS = 16
N_EXPERT_GROUPS = 4
EXPERTS_PER_GROUP = N_EXPERTS // N_EXPERT_GROUPS
TOP_K = 2

_PAIRS = tuple((a, b) for a in range(EXPERTS_PER_GROUP) for b in range(a + 1, EXPERTS_PER_GROUP))
_PAIR_BASE = (0, 3, 5)
N_CLASSES = N_EXPERT_GROUPS * len(_PAIRS)
CLASS_ROWS = 32
INFO_ROWS = 8
LANES = 128

TM_PROJ = 512
TM_CTX = 256
TM_POST = 512
TM_POOL = 256
TM_MOE = 256
TM_FINAL = 512
DMA_UNROLL = 8

F32 = jnp.float32
BF16 = jnp.bfloat16

_NT = (((1,), (1,)), ((), ()))
_TN = (((0,), (0,)), ((), ()))
_VMEM_LIMIT = 56 * 1024 * 1024


def _cparams(sem):
    return pltpu.CompilerParams(dimension_semantics=sem, vmem_limit_bytes=_VMEM_LIMIT)


def _dot(a, b, dims=None):
    if dims is None:
        return jnp.dot(a, b, preferred_element_type=F32)
    return lax.dot_general(a, b, dims, preferred_element_type=F32)


def _split(a):
    hi = a.astype(BF16)
    lo = (a - hi.astype(F32)).astype(BF16)
    return hi, lo


def _dot3(a, b, dims=None):
    ah, al = _split(a)
    bh, bl = _split(b)
    return _dot(ah, bh, dims) + _dot(ah, bl, dims) + _dot(al, bh, dims)


def _sigmoid(x):
    return 1.0 / (1.0 + jnp.exp(-x))


def _silu(x):
    return x * _sigmoid(x)


def _rmsnorm(xf, g):
    return xf * lax.rsqrt(jnp.mean(xf * xf, axis=-1, keepdims=True) + EPS) * g


def _mod_kernel(c_ref, w_ref, b_ref, o_ref):
    o_ref[...] = _dot3(_silu(c_ref[...]), w_ref[...]) + b_ref[...]


def _modulation(c_all, w_mod, b_mod):
    depth, d, d6 = w_mod.shape
    rows = c_all.shape[0]
    tn = 1536
    return pl.pallas_call(
        _mod_kernel,
        out_shape=jax.ShapeDtypeStruct((depth, rows, d6), F32),
        grid=(depth, d6 // tn),
        in_specs=[
            pl.BlockSpec((rows, d), lambda l, j: (0, 0)),
            pl.BlockSpec((None, d, tn), lambda l, j: (l, 0, j)),
            pl.BlockSpec((None, 1, tn), lambda l, j: (l, 0, j)),
        ],
        out_specs=pl.BlockSpec((None, rows, tn), lambda l, j: (l, 0, j)),
        compiler_params=_cparams(("arbitrary", "arbitrary")),
        name="mod",
    )(c_all, w_mod, b_mod.reshape(depth, 1, d6))


def _mod_spec(layer, chunk, d, row_of_batch):
    return pl.BlockSpec((None, None, 1, d), lambda b, t, *_: (layer, row_of_batch(b), 0, chunk))


def _gla_in_kernel(x_ref, sh_ref, sc_ref, ng_ref, win_ref, wga_ref, wgb_ref, bg_ref,
                   q_ref, k_ref, v_ref, r_ref, g_ref, *, kdim, vdim, qscale):
    h = _rmsnorm(x_ref[...], ng_ref[...]) * (1.0 + sc_ref[...]) + sh_ref[...]
    hb = h.astype(BF16)
    proj = _dot(hb, win_ref[...])
    q_ref[...] = (proj[:, :kdim] * qscale).astype(BF16)
    k_ref[...] = proj[:, kdim:2 * kdim].astype(BF16)
    v_ref[...] = proj[:, 2 * kdim:2 * kdim + vdim].astype(BF16)
    r_ref[...] = proj[:, 2 * kdim + vdim:].astype(BF16)
    low = _dot(hb, wga_ref[...])
    z = _dot(low.astype(BF16), wgb_ref[...]) + bg_ref[...]
    logsig = jnp.minimum(z, 0.0) - jnp.log(1.0 + jnp.exp(-jnp.abs(z)))
    g_ref[...] = logsig * (1.0 / GLA_GATE_NORM)


def _gla_in(x, mods4, layer, row_of_batch, ng, win, wga, wgb, bg, tm):
    bsz, t, d = x.shape
    kdim, vdim = d // 2, d
    tm = min(tm, t)
    tok = lambda w: pl.BlockSpec((None, tm, w), lambda b, i: (b, i, 0))
    full = lambda a: pl.BlockSpec(a.shape, lambda b, i: (0,) * a.ndim)
    kern = functools.partial(_gla_in_kernel, kdim=kdim, vdim=vdim,
                             qscale=float((kdim // GLA_HEADS) ** -0.5))
    return pl.pallas_call(
        kern,
        out_shape=(
            jax.ShapeDtypeStruct((bsz, t, kdim), BF16),
            jax.ShapeDtypeStruct((bsz, t, kdim), BF16),
            jax.ShapeDtypeStruct((bsz, t, vdim), BF16),
            jax.ShapeDtypeStruct((bsz, t, vdim), BF16),
            jax.ShapeDtypeStruct((bsz, t, 2 * kdim), F32),
        ),
        grid=(bsz, t // tm),
        in_specs=[
            tok(d),
            _mod_spec(layer, 0, d, row_of_batch),
            _mod_spec(layer, 1, d, row_of_batch),
            full(ng), full(win), full(wga), full(wgb), full(bg),
        ],
        out_specs=(tok(kdim), tok(kdim), tok(vdim), tok(vdim), tok(2 * kdim)),
        compiler_params=_cparams(("arbitrary", "arbitrary")),
        name="gla_in",
    )(x, mods4, mods4, ng, win, wga, wgb, bg)


def _block_masks(rows, chunk):
    i = np.arange(rows)
    same = (i[:, None] // chunk) == (i[None, :] // chunk)
    lower = same & (i[:, None] >= i[None, :])
    upper = same & (i[:, None] <= i[None, :])
    return jnp.asarray(lower, BF16), jnp.asarray(upper, BF16)


def _per_chunk_row(a, chunk, r):
    rows, w = a.shape
    parts = [jnp.broadcast_to(a[c * chunk + r:c * chunk + r + 1, :], (chunk, w))
             for c in range(rows // chunk)]
    return jnp.concatenate(parts, axis=0)


def _block_terms(q, k, v, g, bd, reverse, chunk, want_out):
    gh, gl = _split(g)
    cum = _dot(bd, gh) + _dot(bd, gl)
    tot = _per_chunk_row(cum, chunk, 0 if reverse else chunk - 1)
    kf = k.astype(F32)
    kl = (kf * jnp.exp(tot - cum)).astype(BF16)
    etot = jnp.exp(tot)
    if not want_out:
        return kl, etot, None, None
    mid = _per_chunk_row(cum, chunk, chunk // 2)
    qf = q.astype(F32)
    qe = (qf * jnp.exp(cum)).astype(BF16)
    qi = (qf * jnp.exp(cum - mid)).astype(BF16)
    ki = (kf * jnp.exp(mid - cum)).astype(BF16)
    s = jnp.where(bd > 0, _dot(qi, ki, _NT), 0.0)
    return kl, etot, qe, _dot(s.astype(BF16), v)


def _store_chunk_states(v, kl, etot, ds_ref, e_ref, first_chunk, chunk):
    for c in range(v.shape[0] // chunk):
        rows = slice(c * chunk, (c + 1) * chunk)
        ds_ref[first_chunk + c] = _dot(v[rows, :], kl[rows, :], _TN)
        e_ref[first_chunk + c] = etot[c * chunk:c * chunk + 8, :]


def _gla_state_kernel(k_ref, v_ref, gf_ref, gb_ref, lo_ref, up_ref, sf_ref, sb_ref,
                      dsf_ref, dsb_ref, ef_ref, eb_ref, *, chunk, block):
    t = k_ref.shape[0]
    n = t // chunk
    per = block // chunk

    def terms(i, carry):
        rows = pl.ds(pl.multiple_of(i * block, block), block)
        k, v = k_ref[rows, :], v_ref[rows, :]
        kl, etot, _, _ = _block_terms(None, k, v, gf_ref[rows, :], lo_ref[...], False, chunk, False)
        _store_chunk_states(v, kl, etot, dsf_ref, ef_ref, i * per, chunk)
        kl, etot, _, _ = _block_terms(None, k, v, gb_ref[rows, :], up_ref[...], True, chunk, False)
        _store_chunk_states(v, kl, etot, dsb_ref, eb_ref, i * per, chunk)
        return carry

    lax.fori_loop(0, t // block, terms, 0)

    def scan(c, carry):
        sf, sb = carry
        r = n - 1 - c
        return (sf * ef_ref[c][0:1, :] + dsf_ref[c], sb * eb_ref[r][0:1, :] + dsb_ref[r])

    zero = jnp.zeros(sf_ref.shape, F32)
    sf, sb = lax.fori_loop(0, n, scan, (zero, zero))
    sf_ref[...] = sf
    sb_ref[...] = sb


def _gla_scan_kernel(q_ref, k_ref, v_ref, gf_ref, gb_ref, s0f_ref, s0b_ref, ng_ref, lo_ref, up_ref,
                     o_ref, oi_ref, qef_ref, qeb_ref, dsf_ref, dsb_ref, ef_ref, eb_ref,
                     scf_ref, scb_ref, *, chunk, block):
    t = q_ref.shape[0]
    n = t // chunk
    per = block // chunk

    def terms(i, carry):
        rows = pl.ds(pl.multiple_of(i * block, block), block)
        q, k, v = q_ref[rows, :], k_ref[rows, :], v_ref[rows, :]
        kl, etot, qe, of = _block_terms(q, k, v, gf_ref[rows, :], lo_ref[...], False, chunk, True)
        _store_chunk_states(v, kl, etot, dsf_ref, ef_ref, i * per, chunk)
        qef_ref[rows, :] = qe
        kl, etot, qe, ob = _block_terms(q, k, v, gb_ref[rows, :], up_ref[...], True, chunk, True)
        _store_chunk_states(v, kl, etot, dsb_ref, eb_ref, i * per, chunk)
        qeb_ref[rows, :] = qe
        oi_ref[rows, :] = of + ob
        return carry

    lax.fori_loop(0, t // block, terms, 0)

    def scan(c, carry):
        sf, sb = carry
        r = n - 1 - c
        scf_ref[c] = sf.astype(BF16)
        scb_ref[r] = sb.astype(BF16)
        return (sf * ef_ref[c][0:1, :] + dsf_ref[c], sb * eb_ref[r][0:1, :] + dsb_ref[r])

    lax.fori_loop(0, n, scan, (s0f_ref[...], s0b_ref[...]))

    def finish(i, carry):
        for c in range(per):
            rows = pl.ds(pl.multiple_of(i * block + c * chunk, chunk), chunk)
            o = (oi_ref[rows, :] + _dot(qef_ref[rows, :], scf_ref[i * per + c], _NT)
                 + _dot(qeb_ref[rows, :], scb_ref[i * per + c], _NT))
            o_ref[rows, :] = _rmsnorm(o, ng_ref[...]).astype(o_ref.dtype)
        return carry

    lax.fori_loop(0, t // block, finish, 0)


def _gla_state(k, v, g, chunk, block):
    bsz, t, kdim = k.shape
    vdim = v.shape[-1]
    nh = GLA_HEADS
    dk, dv = kdim // nh, vdim // nh
    block = min(block, t)
    n = t // chunk
    lo, up = _block_masks(block, chunk)
    st = jax.ShapeDtypeStruct((bsz, nh, dv, dk), F32)
    st_spec = pl.BlockSpec((None, None, dv, dk), lambda b, h: (b, h, 0, 0))
    mask_spec = pl.BlockSpec((block, block), lambda b, h: (0, 0))
    return pl.pallas_call(
        functools.partial(_gla_state_kernel, chunk=chunk, block=block),
        out_shape=(st, st),
        grid=(bsz, nh),
        in_specs=[
            pl.BlockSpec((None, t, dk), lambda b, h: (b, 0, h)),
            pl.BlockSpec((None, t, dv), lambda b, h: (b, 0, h)),
            pl.BlockSpec((None, t, dk), lambda b, h: (b, 0, h)),
            pl.BlockSpec((None, t, dk), lambda b, h: (b, 0, nh + h)),
            mask_spec, mask_spec,
        ],
        out_specs=(st_spec, st_spec),
        scratch_shapes=[
            pltpu.VMEM((n, dv, dk), F32), pltpu.VMEM((n, dv, dk), F32),
            pltpu.VMEM((n, 8, dk), F32), pltpu.VMEM((n, 8, dk), F32),
        ],
        compiler_params=_cparams(("arbitrary", "arbitrary")),
        name="gla_state",
    )(k, v, g, g, lo, up)


def _gla_scan(q, k, v, g, s0f, s0b, norm_g, chunk, block):
    bsz, t, kdim = k.shape
    vdim = v.shape[-1]
    nh = GLA_HEADS
    dk, dv = kdim // nh, vdim // nh
    block = min(block, t)
    n = t // chunk
    lo, up = _block_masks(block, chunk)
    st_spec = pl.BlockSpec((None, None, dv, dk), lambda b, h: (b, h, 0, 0))
    kspec = pl.BlockSpec((None, t, dk), lambda b, h: (b, 0, h))
    vspec = pl.BlockSpec((None, t, dv), lambda b, h: (b, 0, h))
    mask_spec = pl.BlockSpec((block, block), lambda b, h: (0, 0))
    return pl.pallas_call(
        functools.partial(_gla_scan_kernel, chunk=chunk, block=block),
        out_shape=jax.ShapeDtypeStruct((bsz, t, vdim), BF16),
        grid=(bsz, nh),
        in_specs=[
            kspec, kspec, vspec, kspec,
            pl.BlockSpec((None, t, dk), lambda b, h: (b, 0, nh + h)),
            st_spec, st_spec,
            pl.BlockSpec((1, dv), lambda b, h: (0, 0)),
            mask_spec, mask_spec,
        ],
        out_specs=vspec,
        scratch_shapes=[
            pltpu.VMEM((t, dv), F32),
            pltpu.VMEM((t, dk), BF16), pltpu.VMEM((t, dk), BF16),
            pltpu.VMEM((n, dv, dk), F32), pltpu.VMEM((n, dv, dk), F32),
            pltpu.VMEM((n, 8, dk), F32), pltpu.VMEM((n, 8, dk), F32),
            pltpu.VMEM((n, dv, dk), BF16), pltpu.VMEM((n, dv, dk), BF16),
        ],
        compiler_params=_cparams(("arbitrary", "arbitrary")),
        name="gla_scan",
    )(q, k, v, g, g, s0f, s0b, norm_g, lo, up)


def _route(h2, wr_t, br, earlier_bf, count_ref):
    tm = h2.shape[0]
    logits = _dot3(wr_t, h2, _NT)
    scores = _sigmoid(logits)
    sel = scores + br
    row = [sel[e:e + 1, :] for e in range(N_EXPERTS)]
    picked = []
    for e in range(N_EXPERTS):
        g0 = (e // EXPERTS_PER_GROUP) * EXPERTS_PER_GROUP
        ahead_count = jnp.zeros(row[e].shape, jnp.int32)
        for j in range(g0, g0 + EXPERTS_PER_GROUP):
            if j == e:
                continue
            ahead = (row[j] >= row[e]) if j < e else (row[j] > row[e])
            ahead_count = ahead_count + ahead.astype(jnp.int32)
        picked.append(ahead_count < TOP_K)
    zero = jnp.zeros_like(row[0])
    gscore, pair, first_w, second_w = [], [], [], []
    for g in range(N_EXPERT_GROUPS):
        acc, pidx, fw, sw = zero, zero, zero, zero
        seen = None
        for a in range(EXPERTS_PER_GROUP):
            e = g * EXPERTS_PER_GROUP + a
            sc = scores[e:e + 1, :]
            acc = acc + jnp.where(picked[e], row[e], 0.0)
            if seen is None:
                is_first = picked[e]
            else:
                is_first = picked[e] & jnp.logical_not(seen)
                is_second = picked[e] & seen
                sw = sw + jnp.where(is_second, sc, 0.0)
                pidx = pidx + jnp.where(is_second, float(a), 0.0)
            fw = fw + jnp.where(is_first, sc, 0.0)
            if a < len(_PAIR_BASE):
                pidx = pidx + jnp.where(is_first, float(_PAIR_BASE[a] - a - 1), 0.0)
            seen = picked[e] if seen is None else (seen | picked[e])
        gscore.append(acc)
        pair.append(pidx)
        first_w.append(fw)
        second_w.append(sw)
    cls, wa, wb = zero, zero, zero
    for g in range(N_EXPERT_GROUPS):
        ok = None
        for j in range(N_EXPERT_GROUPS):
            if j == g:
                continue
            c = (gscore[g] > gscore[j]) if j < g else (gscore[g] >= gscore[j])
            ok = c if ok is None else (ok & c)
        cls = cls + jnp.where(ok, pair[g] + float(len(_PAIRS) * g), 0.0)
        wa = wa + jnp.where(ok, first_w[g], 0.0)
        wb = wb + jnp.where(ok, second_w[g], 0.0)
    denom = wa + wb
    wa = wa / denom
    wb = wb / denom

    cid = lax.broadcasted_iota(jnp.int32, (CLASS_ROWS, tm), 0).astype(F32)
    onehot = (cid == cls).astype(BF16)
    before = _dot(onehot, earlier_bf)
    oh = onehot.astype(F32)
    base = count_ref[...][:, 0:1]
    rank = jnp.sum(oh * (before + base), axis=0, keepdims=True)
    count_ref[...] = count_ref[...] + jnp.sum(oh, axis=1, keepdims=True)
    pad = jnp.zeros((INFO_ROWS - 4, tm), F32)
    return jnp.concatenate([cls, rank, wa, wb, pad], axis=0)


def _earlier_matrix(tm):
    i = np.arange(tm)
    return jnp.asarray(i[:, None] < i[None, :], BF16)


def _store_tiled_rows(ref, x):
    tm, d = x.shape
    per = d // LANES
    for c in range(per):
        ref[pl.ds(c, tm, stride=per), :] = x[:, c * LANES:(c + 1) * LANES]


def _load_tiled_rows(ref, tm):
    per = ref.shape[0] // tm
    return jnp.concatenate([ref[pl.ds(c, tm, stride=per), :] for c in range(per)], axis=1)


def _route_outputs(h2, wr_ref, br_ref, earlier_ref, count_ref, pay_ref, info_ref, counts_ref):
    first = (pl.program_id(0) == 0) & (pl.program_id(1) == 0)

    @pl.when(first)
    def _():
        count_ref[...] = jnp.zeros_like(count_ref)

    info = _route(h2, wr_ref[...], br_ref[...], earlier_ref[...], count_ref)
    _store_tiled_rows(pay_ref, h2)
    info_ref[...] = info
    counts_ref[...] = count_ref[...]


def _route_out_shapes(bsz, t, d):
    return (
        jax.ShapeDtypeStruct((bsz * t * (d // LANES), LANES), F32),
        jax.ShapeDtypeStruct((INFO_ROWS, bsz * t), F32),
        jax.ShapeDtypeStruct((CLASS_ROWS, 128), F32),
    )


def _route_out_specs(tm, nt, d):
    return (
        pl.BlockSpec((tm * (d // LANES), LANES), lambda b, i, *_: (b * nt + i, 0)),
        pl.BlockSpec((INFO_ROWS, tm), lambda b, i, *_: (0, b * nt + i)),
        pl.BlockSpec((CLASS_ROWS, 128), lambda b, i, *_: (0, 0)),
    )


def _plan(info, counts, n, tmoe):
    cls = info[0].astype(jnp.int32)
    rank = info[1].astype(jnp.int32)
    cnt = counts[:N_CLASSES, 0].astype(jnp.int32)
    padded = (cnt + tmoe - 1) // tmoe * tmoe
    ends = jnp.cumsum(padded)
    dest = (ends - padded)[cls] + rank
    n_tiles = n // tmoe + N_CLASSES
    rows = jnp.stack([jnp.arange(n, dtype=F32), info[2], info[3], jnp.zeros((n,), F32)], axis=1)
    init = jnp.zeros((n_tiles * tmoe, 4), F32).at[:, 0].set(-1.0)
    srt = init.at[dest].set(rows, unique_indices=True)
    token = srt[:, 0].astype(jnp.int32)
    is_pad = token < 0
    dst = jnp.where(is_pad, n - 1 + jnp.cumsum(is_pad.astype(jnp.int32)), token)
    n_used = ends[-1] // tmoe
    tile = jnp.minimum(jnp.arange(n_tiles + 1, dtype=jnp.int32), n_used - 1)
    tcls = jnp.sum((ends[None, :] <= (tile * tmoe)[:, None]).astype(jnp.int32), axis=1)
    group, pair = tcls // len(_PAIRS), tcls % len(_PAIRS)
    pa = jnp.asarray([p[0] for p in _PAIRS], jnp.int32)[pair]
    pb = jnp.asarray([p[1] for p in _PAIRS], jnp.int32)[pair]
    ea = group * EXPERTS_PER_GROUP + pa
    eb = group * EXPERTS_PER_GROUP + pb
    return dst, srt, ea, eb, n_used.reshape(1).astype(jnp.int32)


def _post0_kernel(o_ref, r_ref, x_ref, g1_ref, sh2_ref, sc2_ref, n2_ref, wout_ref, wr_ref, br_ref,
                  earlier_ref, x1_ref, pay_ref, info_ref, counts_ref, count_ref):
    a = (o_ref[...].astype(F32) * _silu(r_ref[...].astype(F32))).astype(BF16)
    x1 = x_ref[...] + g1_ref[...] * _dot(a, wout_ref[...])
    x1_ref[...] = x1
    h2 = _rmsnorm(x1, n2_ref[...]) * (1.0 + sc2_ref[...]) + sh2_ref[...]
    _route_outputs(h2, wr_ref, br_ref, earlier_ref, count_ref, pay_ref, info_ref, counts_ref)


def _post0(o, r, x, mods4, layer, n2, wout, wr_t, br, tm):
    bsz, t, d = x.shape
    tm = min(tm, t)
    rb = lambda b: b
    tok = lambda w: pl.BlockSpec((None, tm, w), lambda b, i: (b, i, 0))
    full = lambda a: pl.BlockSpec(a.shape, lambda b, i: (0,) * a.ndim)
    nt = t // tm
    earlier = _earlier_matrix(tm)
    return pl.pallas_call(
        _post0_kernel,
        out_shape=(jax.ShapeDtypeStruct((bsz, t, d), F32),) + _route_out_shapes(bsz, t, d),
        grid=(bsz, nt),
        in_specs=[
            tok(d), tok(d), tok(d),
            _mod_spec(layer, 2, d, rb), _mod_spec(layer, 3, d, rb), _mod_spec(layer, 4, d, rb),
            full(n2), full(wout), full(wr_t), full(br), full(earlier),
        ],
        out_specs=(tok(d),) + _route_out_specs(tm, nt, d),
        scratch_shapes=[pltpu.VMEM((CLASS_ROWS, 128), F32)],
        compiler_params=_cparams(("arbitrary", "arbitrary")),
        name="post0",
    )(o, r, x, mods4, mods4, mods4, n2, wout, wr_t, br, earlier)


def _moe_kernel(dst_ref, ea_ref, eb_ref, used_ref, p_hbm, gw_ref, wa_ref, wb_ref, f_hbm,
                pbuf0, pbuf1, obuf0, obuf1, gsem, ssem, *, d, tm, n_tokens):
    del ea_ref, eb_ref
    j = pl.program_id(0)
    n_tiles = pl.num_programs(0) - 1
    used = used_ref[0]
    per = d // LANES
    pbufs, obufs = (pbuf0, pbuf1), (obuf0, obuf1)

    def group(i):
        return pl.ds(pl.multiple_of(i * per, per), per)

    def gather_row(tile, r, p, pred):
        i = dst_ref[tile * tm + r]
        i = jnp.where(i < n_tokens, i, 0)

        @pl.when(pred)
        def _():
            pltpu.make_async_copy(p_hbm.at[group(i), :], pbufs[p].at[group(r), :], gsem.at[p]).start()

    def scatter_row(tile, r, p, pred):
        i = dst_ref[tile * tm + r]

        @pl.when(pred)
        def _():
            pltpu.make_async_copy(obufs[p].at[group(r), :], f_hbm.at[group(i), :], ssem.at[p]).start()

    def rolled(row_fn, tile, p):
        def body(blk, carry):
            for u in range(DMA_UNROLL):
                row_fn(tile, blk * DMA_UNROLL + u, p, True)
            return carry
        lax.fori_loop(0, tm // DMA_UNROLL, body, 0)

    def gather_wait(p):
        pltpu.make_async_copy(p_hbm.at[pl.ds(0, tm * per), :], pbufs[p], gsem.at[p]).wait()

    def scatter_wait(p):
        pltpu.make_async_copy(obufs[p], f_hbm.at[pl.ds(0, tm * per), :], ssem.at[p]).wait()

    @pl.when(j == 0)
    def _():
        rolled(gather_row, 0, 0)

    def step(p):
        prev = jnp.maximum(j - 1, 0)

        @pl.when(j >= 2)
        def _():
            scatter_wait(p)

        @pl.when(j < used)
        def _():
            gather_wait(p)
            has_next = j + 1 < used
            has_prev = j >= 1
            nxt = jnp.minimum(j + 1, n_tiles - 1)
            for r in range(tm):
                gather_row(nxt, r, 1 - p, has_next)
            h = _load_tiled_rows(pbufs[p], tm).astype(BF16)
            gw = gw_ref[...]

            def expert(w_ref):
                de = w_ref.shape[2]
                he = (_silu(_dot(h, w_ref[0])) * _dot(h, w_ref[1])).astype(BF16)
                return jnp.concatenate([_dot(he, w_ref[2, :de, :]), _dot(he, w_ref[2, de:, :])], axis=1)

            y = gw[:, 1:2] * expert(wa_ref) + gw[:, 2:3] * expert(wb_ref)
            _store_tiled_rows(obufs[p], y)
            for r in range(tm):
                scatter_row(prev, r, 1 - p, has_prev)

        @pl.when((j >= used) & (j < n_tiles))
        def _():
            obufs[p][...] = jnp.zeros(obufs[p].shape, F32)
            rolled(scatter_row, prev, 1 - p)

        @pl.when(j == n_tiles)
        def _():
            rolled(scatter_row, prev, 1 - p)
            scatter_wait(1 - p)

    @pl.when(j % 2 == 0)
    def _():
        step(0)

    @pl.when(j % 2 == 1)
    def _():
        step(1)


def _moe(payload, dst, srt, ea, eb, n_used, w_all, tm, n):
    n_sorted = dst.shape[0]
    n_tiles = n_sorted // tm
    _, _, d, de = w_all.shape
    per = d // LANES
    wa = lambda j, dst, ea, eb, used: (ea[j], 0, 0, 0)
    wb = lambda j, dst, ea, eb, used: (eb[j], 0, 0, 0)
    return pl.pallas_call(
        functools.partial(_moe_kernel, d=d, tm=tm, n_tokens=n),
        out_shape=jax.ShapeDtypeStruct((n_sorted * per, LANES), F32),
        grid_spec=pltpu.PrefetchScalarGridSpec(
            num_scalar_prefetch=4,
            grid=(n_tiles + 1,),
            in_specs=[
                pl.BlockSpec(memory_space=pl.ANY),
                pl.BlockSpec((tm, srt.shape[1]), lambda j, *_: (jnp.minimum(j, n_tiles - 1), 0)),
                pl.BlockSpec((None, 3, d, de), wa), pl.BlockSpec((None, 3, d, de), wb),
            ],
            out_specs=pl.BlockSpec(memory_space=pl.ANY),
            scratch_shapes=[pltpu.VMEM((tm * per, LANES), F32), pltpu.VMEM((tm * per, LANES), F32),
                            pltpu.VMEM((tm * per, LANES), F32), pltpu.VMEM((tm * per, LANES), F32),
                            pltpu.SemaphoreType.DMA((2,)), pltpu.SemaphoreType.DMA((2,))],
        ),
        compiler_params=_cparams(("arbitrary",)),
        name="moe",
    )(dst, ea, eb, n_used, payload, srt, w_all, w_all)


def _expert_weights(w1, w3, w2):
    de, d = w2.shape[1:]
    assert d == 2 * de
    w2s = jnp.concatenate([w2[:, :, :de], w2[:, :, de:]], axis=1)
    return jnp.stack([w1, w3, w2s], axis=1).astype(BF16)


def _sparse_moe(payload, info, counts, w_all):
    n = info.shape[1]
    tmoe = min(TM_MOE, n)
    dst, srt, ea, eb, n_used = _plan(info, counts, n, tmoe)
    return _moe(payload, dst, srt, ea, eb, n_used, w_all, tmoe, n)


def _pool_constants(tm, d):
    ng = len(POOL_WINDOWS)
    cg = d // ng
    pos = np.arange(tm)
    seg, off = pos // GRID_W, pos % GRID_W
    mats = np.zeros((ng, tm, tm), np.float32)
    inv = np.zeros((tm, d), np.float32)
    for gi, w in enumerate(POOL_WINDOWS):
        lo = np.clip(off - w // 2, 0, GRID_W)
        hi = np.clip(off - w // 2 + w, 0, GRID_W)
        same = seg[:, None] == seg[None, :]
        inside = (off[None, :] >= lo[:, None]) & (off[None, :] < hi[:, None])
        mats[gi] = (same & inside).astype(np.float32)
        inv[:, gi * cg:(gi + 1) * cg] = (1.0 / (hi - lo).astype(np.float32))[:, None]
    return jnp.asarray(mats, BF16), jnp.asarray(inv, F32)


def _layer1_kernel(x_ref, f_ref, g2p_ref, sh1_ref, sc1_ref, g1_ref, sh2_ref, sc2_ref,
                   n1_ref, n2_ref, pm_ref, inv_ref, wp_ref, bp_ref, ps_ref, wr_ref, br_ref, earlier_ref,
                   x3_ref, pay_ref, info_ref, counts_ref, count_ref):
    x2 = x_ref[...] + g2p_ref[...] * _load_tiled_rows(f_ref, x_ref.shape[0])
    h = _rmsnorm(x2, n1_ref[...]) * (1.0 + sc1_ref[...]) + sh1_ref[...]
    hb = h.astype(BF16)
    ng = pm_ref.shape[0]
    cg = h.shape[1] // ng
    ys = []
    for gi in range(ng):
        cs = slice(gi * cg, (gi + 1) * cg)
        wsum = _dot(pm_ref[gi], hb[:, cs])
        pooled = wsum * inv_ref[:, cs] - h[:, cs]
        ys.append(_dot(pooled.astype(BF16), wp_ref[gi]))
    y = (jnp.concatenate(ys, axis=1) + bp_ref[...]) * ps_ref[...]
    x3 = x2 + g1_ref[...] * y
    x3_ref[...] = x3
    h2 = _rmsnorm(x3, n2_ref[...]) * (1.0 + sc2_ref[...]) + sh2_ref[...]
    _route_outputs(h2, wr_ref, br_ref, earlier_ref, count_ref, pay_ref, info_ref, counts_ref)


def _layer1(x1, f, mods4, n1, n2, wp, bp, ps, wr_t, br, tm):
    bsz, t, d = x1.shape
    tm = min(tm, t)
    pm, inv = _pool_constants(tm, d)
    earlier = _earlier_matrix(tm)
    rb = lambda b: b
    tok = lambda w: pl.BlockSpec((None, tm, w), lambda b, i: (b, i, 0))
    full = lambda a: pl.BlockSpec(a.shape, lambda b, i: (0,) * a.ndim)
    nt = t // tm
    return pl.pallas_call(
        _layer1_kernel,
        out_shape=(jax.ShapeDtypeStruct((bsz, t, d), F32),) + _route_out_shapes(bsz, t, d),
        grid=(bsz, nt),
        in_specs=[
            tok(d), pl.BlockSpec((tm * (d // LANES), LANES), lambda b, i: (b * nt + i, 0)),
            _mod_spec(0, 5, d, rb),
            _mod_spec(1, 0, d, rb), _mod_spec(1, 1, d, rb), _mod_spec(1, 2, d, rb),
            _mod_spec(1, 3, d, rb), _mod_spec(1, 4, d, rb),
            full(n1), full(n2), full(pm), full(inv), full(wp), full(bp), full(ps),
            full(wr_t), full(br), full(earlier),
        ],
        out_specs=(tok(d),) + _route_out_specs(tm, nt, d),
        scratch_shapes=[pltpu.VMEM((CLASS_ROWS, 128), F32)],
        compiler_params=_cparams(("arbitrary", "arbitrary")),
        name="layer1",
    )(x1, f, mods4, mods4, mods4, mods4, mods4, mods4, n1, n2, pm, inv, wp, bp, ps, wr_t, br, earlier)


def _final_kernel(x_ref, f_ref, g2_ref, fg_ref, o_ref):
    f = _load_tiled_rows(f_ref, x_ref.shape[0])
    o_ref[...] = _rmsnorm(x_ref[...] + g2_ref[...] * f, fg_ref[...])


def _final(x3, f, mods4, fg, tm):
    bsz, t, d = x3.shape
    tm = min(tm, t)
    nt = t // tm
    tok = pl.BlockSpec((None, tm, d), lambda b, i: (b, i, 0))
    return pl.pallas_call(
        _final_kernel,
        out_shape=jax.ShapeDtypeStruct((bsz, t, d), F32),
        grid=(bsz, nt),
        in_specs=[tok, pl.BlockSpec((tm * (d // LANES), LANES), lambda b, i: (b * nt + i, 0)),
                  _mod_spec(1, 5, d, lambda b: b), pl.BlockSpec((1, d), lambda b, i: (0, 0))],
        out_specs=tok,
        compiler_params=_cparams(("arbitrary", "arbitrary")),
        name="final",
    )(x3, f, mods4, fg)


def kernel(x, c, ctx, c_ctx, norm1_g, norm2_g, w_mod, b_mod, gla_w_in, gla_w_gate_a, gla_w_gate_b,
           gla_b_gate, gla_norm_g, gla_w_out, pool_w, pool_b, pool_scale, w_router, b_router,
           w_gate_e, w_up_e, w_down_e, final_g):
    bsz, t, d = x.shape
    depth = w_mod.shape[0]
    assert depth == 2 and t % GRID_W == 0
    kdim = d // 2
    row = lambda a: a.reshape(1, -1)

    c_rows = -(-(bsz + 1) // 8) * 8
    c_all = jnp.zeros((c_rows, d), F32).at[:bsz].set(c).at[bsz].set(c_ctx)
    mods = _modulation(c_all, w_mod, b_mod)
    mods4 = mods.reshape(depth, c_rows, 1, 6 * d)

    win = gla_w_in[0].astype(BF16)
    wga = jnp.concatenate([gla_w_gate_a[0, 0], gla_w_gate_a[0, 1]], axis=1).astype(BF16)
    zero = jnp.zeros((GLA_GATE_RANK, kdim), F32)
    wgb = jnp.concatenate([
        jnp.concatenate([gla_w_gate_b[0, 0], zero], axis=1),
        jnp.concatenate([zero, gla_w_gate_b[0, 1]], axis=1)], axis=0).astype(BF16)
    bg = gla_b_gate[0].reshape(1, 2 * kdim)
    n1_0 = row(norm1_g[0])
    q, k, v, r, g = _gla_in(x, mods4, 0, lambda b: b, n1_0, win, wga, wgb, bg, TM_PROJ)
    _, kc, vc, _, gc = _gla_in(ctx, mods4, 0, lambda b: bsz, n1_0, win, wga, wgb, bg, TM_CTX)
    s0f, s0b = _gla_state(kc, vc, gc, GLA_CHUNK, GLA_BLOCK)
    o = _gla_scan(q, k, v, g, s0f, s0b, row(gla_norm_g[0]), GLA_CHUNK, GLA_BLOCK)

    wr_t = jnp.transpose(w_router)
    br = b_router.reshape(N_EXPERTS, 1)
    x1, payload, info, counts = _post0(o, r, x, mods4, 0, row(norm2_g[0]),
                                       gla_w_out[0].astype(BF16), wr_t, br, TM_POST)
    f0 = _sparse_moe(payload, info, counts, _expert_weights(w_gate_e[0], w_up_e[0], w_down_e[0]))

    x3, payload, info, counts = _layer1(x1, f0, mods4, row(norm1_g[1]), row(norm2_g[1]),
                                        pool_w[0].astype(BF16), row(pool_b[0]), row(pool_scale[0]),
                                        wr_t, br, TM_POOL)
    f1 = _sparse_moe(payload, info, counts, _expert_weights(w_gate_e[1], w_up_e[1], w_down_e[1]))
    return _final(x3, f1, mods4, row(final_g), TM_FINAL)
```

```python
import functools

import numpy as np
import jax
import jax.numpy as jnp
from jax import lax
from jax.experimental import pallas as pl
from jax.experimental.pallas import tpu as pltpu

EPS = 1e-6
GRID_W = 64
GLA_HEADS = 4
GLA_GATE_RANK = 16
GLA_GATE_NORM = 16.0
GLA_CHUNK = 128
GLA_BLOCK = 256
GLA_BLOCKS_PER_TRIP = 4
GLA_FINISH_BLOCKS_PER_TRIP = 4
POOL_WINDOWS = (2, 4, 8, 16)
N_EXPERTS = 16
N_EXPERT_GROUPS = 4
EXPERTS_PER_GROUP = N_EXPERTS // N_EXPERT_GROUPS
TOP_K = 2

_PAIRS = tuple((a, b) for a in range(EXPERTS_PER_GROUP) for b in range(a + 1, EXPERTS_PER_GROUP))
_PAIR_BASE = (0, 3, 5)
N_CLASSES = N_EXPERT_GROUPS * len(_PAIRS)
CLASS_ROWS = 32
INFO_ROWS = 8
LANES = 128

TM_PROJ = 512
TM_CTX = 256
TM_POST = 512
TM_POOL = 512
TM_MOE = 256
TM_FINAL = 512
DMA_UNROLL = 8

F32 = jnp.float32
BF16 = jnp.bfloat16

_NT = (((1,), (1,)), ((), ()))
_TN = (((0,), (0,)), ((), ()))
_VMEM_LIMIT = 56 * 1024 * 1024


def _cparams(sem):
    return pltpu.CompilerParams(dimension_semantics=sem, vmem_limit_bytes=_VMEM_LIMIT)


def _dot(a, b, dims=None):
    if dims is None:
        return jnp.dot(a, b, preferred_element_type=F32)
    return lax.dot_general(a, b, dims, preferred_element_type=F32)


def _split(a):
    hi = a.astype(BF16)
    lo = (a - hi.astype(F32)).astype(BF16)
    return hi, lo


def _dot3(a, b, dims=None):
    ah, al = _split(a)
    bh, bl = _split(b)
    return _dot(ah, bh, dims) + _dot(ah, bl, dims) + _dot(al, bh, dims)


def _sigmoid(x):
    return 1.0 / (1.0 + jnp.exp(-x))


def _silu(x):
    return x * _sigmoid(x)


def _rmsnorm(xf, g):
    return xf * lax.rsqrt(jnp.mean(xf * xf, axis=-1, keepdims=True) + EPS) * g


def _mod_kernel(c_ref, w_ref, b_ref, o_ref):
    o_ref[...] = _dot3(_silu(c_ref[...]), w_ref[...]) + b_ref[...]


def _modulation(c_all, w_mod, b_mod):
    depth, d, d6 = w_mod.shape
    rows = c_all.shape[0]
    tn = 1536
    return pl.pallas_call(
        _mod_kernel,
        out_shape=jax.ShapeDtypeStruct((depth, rows, d6), F32),
        grid=(depth, d6 // tn),
        in_specs=[
            pl.BlockSpec((rows, d), lambda l, j: (0, 0)),
            pl.BlockSpec((None, d, tn), lambda l, j: (l, 0, j)),
            pl.BlockSpec((None, 1, tn), lambda l, j: (l, 0, j)),
        ],
        out_specs=pl.BlockSpec((None, rows, tn), lambda l, j: (l, 0, j)),
        compiler_params=_cparams(("arbitrary", "arbitrary")),
        name="mod",
    )(c_all, w_mod, b_mod.reshape(depth, 1, d6))


def _mod_spec(layer, chunk, d, row_of_batch):
    return pl.BlockSpec((None, None, 1, d), lambda b, t, *_: (layer, row_of_batch(b), 0, chunk))


def _gla_in_kernel(x_ref, sh_ref, sc_ref, ng_ref, win_ref, wga_ref, wgb_ref, bg_ref,
                   q_ref, k_ref, v_ref, r_ref, g_ref, *, kdim, vdim, qscale):
    h = _rmsnorm(x_ref[...], ng_ref[...]) * (1.0 + sc_ref[...]) + sh_ref[...]
    hb = h.astype(BF16)
    proj = _dot(hb, win_ref[...])
    q_ref[...] = (proj[:, :kdim] * qscale).astype(BF16)
    k_ref[...] = proj[:, kdim:2 * kdim].astype(BF16)
    v_ref[...] = proj[:, 2 * kdim:2 * kdim + vdim].astype(BF16)
    r_ref[...] = proj[:, 2 * kdim + vdim:].astype(BF16)
    low = _dot(hb, wga_ref[...])
    z = _dot(low.astype(BF16), wgb_ref[...]) + bg_ref[...]
    logsig = jnp.minimum(z, 0.0) - jnp.log(1.0 + jnp.exp(-jnp.abs(z)))
    g_ref[...] = logsig * (1.0 / GLA_GATE_NORM)


def _gla_in(x, mods4, layer, row_of_batch, ng, win, wga, wgb, bg, tm):
    bsz, t, d = x.shape
    kdim, vdim = d // 2, d
    tm = min(tm, t)
    tok = lambda w: pl.BlockSpec((None, tm, w), lambda b, i: (b, i, 0))
    full = lambda a: pl.BlockSpec(a.shape, lambda b, i: (0,) * a.ndim)
    kern = functools.partial(_gla_in_kernel, kdim=kdim, vdim=vdim,
                             qscale=float((kdim // GLA_HEADS) ** -0.5))
    return pl.pallas_call(
        kern,
        out_shape=(
            jax.ShapeDtypeStruct((bsz, t, kdim), BF16),
            jax.ShapeDtypeStruct((bsz, t, kdim), BF16),
            jax.ShapeDtypeStruct((bsz, t, vdim), BF16),
            jax.ShapeDtypeStruct((bsz, t, vdim), BF16),
            jax.ShapeDtypeStruct((bsz, t, 2 * kdim), F32),
        ),
        grid=(bsz, t // tm),
        in_specs=[
            tok(d),
            _mod_spec(layer, 0, d, row_of_batch),
            _mod_spec(layer, 1, d, row_of_batch),
            full(ng), full(win), full(wga), full(wgb), full(bg),
        ],
        out_specs=(tok(kdim), tok(kdim), tok(vdim), tok(vdim), tok(2 * kdim)),
        compiler_params=_cparams(("arbitrary", "arbitrary")),
        name="gla_in",
    )(x, mods4, mods4, ng, win, wga, wgb, bg)


def _block_masks(rows, chunk):
    i = np.arange(rows)
    same = (i[:, None] // chunk) == (i[None, :] // chunk)
    lower = same & (i[:, None] >= i[None, :])
    upper = same & (i[:, None] <= i[None, :])
    return jnp.asarray(lower, BF16), jnp.asarray(upper, BF16)


def _per_chunk_row(a, chunk, r):
    rows, w = a.shape
    parts = [jnp.broadcast_to(a[c * chunk + r:c * chunk + r + 1, :], (chunk, w))
             for c in range(rows // chunk)]
    return jnp.concatenate(parts, axis=0)


def _block_terms(q, k, v, g2, lo, up, chunk, want_out):
    dk = g2.shape[1] // 2
    gh, gl = _split(g2)
    pre = _dot(lo, gh) + _dot(lo, gl)
    tot = _per_chunk_row(pre, chunk, chunk - 1)
    fwd = lax.broadcasted_iota(jnp.int32, g2.shape, 1) < dk
    cum = jnp.where(fwd, pre, tot - pre + g2)
    kf = k.astype(F32)
    kf2 = jnp.concatenate([kf, kf], axis=1)
    kl = (kf2 * jnp.exp(tot - cum)).astype(BF16)
    etot = jnp.exp(tot)
    if not want_out:
        return kl, etot, None, None
    mid = _per_chunk_row(cum, chunk, chunk // 2)
    qf = q.astype(F32)
    qf2 = jnp.concatenate([qf, qf], axis=1)
    qe = (qf2 * jnp.exp(cum)).astype(BF16)
    qi = (qf2 * jnp.exp(cum - mid)).astype(BF16)
    ki = (kf2 * jnp.exp(mid - cum)).astype(BF16)
    s = (jnp.where(lo > 0, _dot(qi[:, :dk], ki[:, :dk], _NT), 0.0)
         + jnp.where(up > 0, _dot(qi[:, dk:], ki[:, dk:], _NT), 0.0))
    return kl, etot, qe, _dot(s.astype(BF16), v)


def _store_chunk_states(v, kl, etot, ds_ref, e_ref, first_chunk, chunk):
    for c in range(v.shape[0] // chunk):
        rows = slice(c * chunk, (c + 1) * chunk)
        ds_ref[first_chunk + c] = _dot(v[rows, :], kl[rows, :], _TN)
        e_ref[first_chunk + c] = etot[c * chunk:c * chunk + 8, :]


def _scan_states(s0, ds_ref, e_ref, sc_ref, n, dk):
    def body(c, carry):
        sf, sb = carry
        r = n - 1 - c
        if sc_ref is not None:
            sc_ref[c, :, :dk] = sf.astype(BF16)
            sc_ref[r, :, dk:] = sb.astype(BF16)
        return (sf * e_ref[c, 0:1, :dk] + ds_ref[c, :, :dk],
                sb * e_ref[r, 0:1, dk:] + ds_ref[r, :, dk:])

    return lax.fori_loop(0, n, body, (s0[:, :dk], s0[:, dk:]))


def _gla_state_kernel(k_ref, v_ref, g_ref, lo_ref, up_ref, s_ref, ds_ref, e_ref, *, chunk, block):
    t = k_ref.shape[0]
    dk = k_ref.shape[1]
    per = block // chunk

    def terms(i, carry):
        rows = pl.ds(pl.multiple_of(i * block, block), block)
        v = v_ref[rows, :]
        kl, etot, _, _ = _block_terms(None, k_ref[rows, :], v, g_ref[rows, :], lo_ref[...],
                                      up_ref[...], chunk, False)
        _store_chunk_states(v, kl, etot, ds_ref, e_ref, i * per, chunk)
        return carry

    lax.fori_loop(0, t // block, terms, 0)
    sf, sb = _scan_states(jnp.zeros(s_ref.shape, F32), ds_ref, e_ref, None, t // chunk, dk)
    s_ref[:, :dk] = sf
    s_ref[:, dk:] = sb


def _gla_scan_kernel(q_ref, k_ref, v_ref, g_ref, s0_ref, ng_ref, lo_ref, up_ref,
                     o_ref, oi_ref, qe_ref, ds_ref, e_ref, sc_ref, *, chunk, block):
    t = q_ref.shape[0]
    dk = q_ref.shape[1]
    per = block // chunk

    group = GLA_BLOCKS_PER_TRIP if (t // block) % GLA_BLOCKS_PER_TRIP == 0 else 1

    def terms(i, carry):
        for u in range(group):
            blk = i * group + u
            rows = pl.ds(pl.multiple_of(blk * block, block), block)
            v = v_ref[rows, :]
            kl, etot, qe, oi = _block_terms(q_ref[rows, :], k_ref[rows, :], v, g_ref[rows, :],
                                            lo_ref[...], up_ref[...], chunk, True)
            _store_chunk_states(v, kl, etot, ds_ref, e_ref, blk * per, chunk)
            qe_ref[rows, :] = qe
            oi_ref[rows, :] = oi
        return carry

    lax.fori_loop(0, t // (block * group), terms, 0)
    _scan_states(s0_ref[...], ds_ref, e_ref, sc_ref, t // chunk, dk)

    fgroup = GLA_FINISH_BLOCKS_PER_TRIP if (t // block) % GLA_FINISH_BLOCKS_PER_TRIP == 0 else 1

    def finish(i, carry):
        for u in range(fgroup):
            blk = i * fgroup + u
            rows = pl.ds(pl.multiple_of(blk * block, block), block)
            inter = [_dot(qe_ref[pl.ds(pl.multiple_of(blk * block + c * chunk, chunk), chunk), :],
                          sc_ref[blk * per + c], _NT) for c in range(per)]
            o = oi_ref[rows, :] + jnp.concatenate(inter, axis=0)
            o_ref[rows, :] = _rmsnorm(o, ng_ref[...]).astype(o_ref.dtype)
        return carry

    lax.fori_loop(0, t // (block * fgroup), finish, 0)


def _gla_state(k, v, g, chunk, block):
    bsz, t, kdim = k.shape
    vdim = v.shape[-1]
    nh = GLA_HEADS
    dk, dv = kdim // nh, vdim // nh
    block = min(block, t)
    n = t // chunk
    lo, up = _block_masks(block, chunk)
    mask_spec = pl.BlockSpec((block, block), lambda b, h: (0, 0))
    return pl.pallas_call(
        functools.partial(_gla_state_kernel, chunk=chunk, block=block),
        out_shape=jax.ShapeDtypeStruct((bsz, nh, dv, 2 * dk), F32),
        grid=(bsz, nh),
        in_specs=[
            pl.BlockSpec((None, t, dk), lambda b, h: (b, 0, h)),
            pl.BlockSpec((None, t, dv), lambda b, h: (b, 0, h)),
            pl.BlockSpec((None, t, 2 * dk), lambda b, h: (b, 0, h)),
            mask_spec, mask_spec,
        ],
        out_specs=pl.BlockSpec((None, None, dv, 2 * dk), lambda b, h: (b, h, 0, 0)),
        scratch_shapes=[pltpu.VMEM((n, dv, 2 * dk), F32), pltpu.VMEM((n, 8, 2 * dk), F32)],
        compiler_params=_cparams(("arbitrary", "arbitrary")),
        name="gla_state",
    )(k, v, g, lo, up)


def _gla_scan(q, k, v, g, s0, norm_g, chunk, block):
    bsz, t, kdim = k.shape
    vdim = v.shape[-1]
    nh = GLA_HEADS
    dk, dv = kdim // nh, vdim // nh
    block = min(block, t)
    n = t // chunk
    lo, up = _block_masks(block, chunk)
    kspec = pl.BlockSpec((None, t, dk), lambda b, h: (b, 0, h))
    vspec = pl.BlockSpec((None, t, dv), lambda b, h: (b, 0, h))
    mask_spec = pl.BlockSpec((block, block), lambda b, h: (0, 0))
    return pl.pallas_call(
        functools.partial(_gla_scan_kernel, chunk=chunk, block=block),
        out_shape=jax.ShapeDtypeStruct((bsz, t, vdim), BF16),
        grid=(bsz, nh),
        in_specs=[
            kspec, kspec, vspec,
            pl.BlockSpec((None, t, 2 * dk), lambda b, h: (b, 0, h)),
            pl.BlockSpec((None, None, dv, 2 * dk), lambda b, h: (b, h, 0, 0)),
            pl.BlockSpec((1, dv), lambda b, h: (0, 0)),
            mask_spec, mask_spec,
        ],
        out_specs=vspec,
        scratch_shapes=[
            pltpu.VMEM((t, dv), F32),
            pltpu.VMEM((t, 2 * dk), BF16),
            pltpu.VMEM((n, dv, 2 * dk), F32),
            pltpu.VMEM((n, 8, 2 * dk), F32),
            pltpu.VMEM((n, dv, 2 * dk), BF16),
        ],
        compiler_params=_cparams(("arbitrary", "arbitrary")),
        name="gla_scan",
    )(q, k, v, g, s0, norm_g, lo, up)


def _route(h2, wr_t, br, earlier_bf, count_ref):
    tm = h2.shape[0]
    logits = _dot3(wr_t, h2, _NT)
    scores = _sigmoid(logits)
    sel = scores + br
    row = [sel[e:e + 1, :] for e in range(N_EXPERTS)]
    picked = []
    for e in range(N_EXPERTS):
        g0 = (e // EXPERTS_PER_GROUP) * EXPERTS_PER_GROUP
        ahead_count = jnp.zeros(row[e].shape, jnp.int32)
        for j in range(g0, g0 + EXPERTS_PER_GROUP):
            if j == e:
                continue
            ahead = (row[j] >= row[e]) if j < e else (row[j] > row[e])
            ahead_count = ahead_count + ahead.astype(jnp.int32)
        picked.append(ahead_count < TOP_K)
    zero = jnp.zeros_like(row[0])
    gscore, pair, first_w, second_w = [], [], [], []
    for g in range(N_EXPERT_GROUPS):
        acc, pidx, fw, sw = zero, zero, zero, zero
        seen = None
        for a in range(EXPERTS_PER_GROUP):
            e = g * EXPERTS_PER_GROUP + a
            sc = scores[e:e + 1, :]
            acc = acc + jnp.where(picked[e], row[e], 0.0)
            if seen is None:
                is_first = picked[e]
            else:
                is_first = picked[e] & jnp.logical_not(seen)
                is_second = picked[e] & seen
                sw = sw + jnp.where(is_second, sc, 0.0)
                pidx = pidx + jnp.where(is_second, float(a), 0.0)
            fw = fw + jnp.where(is_first, sc, 0.0)
            if a < len(_PAIR_BASE):
                pidx = pidx + jnp.where(is_first, float(_PAIR_BASE[a] - a - 1), 0.0)
            seen = picked[e] if seen is None else (seen | picked[e])
        gscore.append(acc)
        pair.append(pidx)
        first_w.append(fw)
        second_w.append(sw)
    cls, wa, wb = zero, zero, zero
    for g in range(N_EXPERT_GROUPS):
        ok = None
        for j in range(N_EXPERT_GROUPS):
            if j == g:
                continue
            c = (gscore[g] > gscore[j]) if j < g else (gscore[g] >= gscore[j])
            ok = c if ok is None else (ok & c)
        cls = cls + jnp.where(ok, pair[g] + float(len(_PAIRS) * g), 0.0)
        wa = wa + jnp.where(ok, first_w[g], 0.0)
        wb = wb + jnp.where(ok, second_w[g], 0.0)
    denom = wa + wb
    wa = wa / denom
    wb = wb / denom

    cid = lax.broadcasted_iota(jnp.int32, (CLASS_ROWS, tm), 0).astype(F32)
    onehot = (cid == cls).astype(BF16)
    before = _dot(onehot, earlier_bf)
    oh = onehot.astype(F32)
    base = count_ref[...][:, 0:1]
    rank = jnp.sum(oh * (before + base), axis=0, keepdims=True)
    count_ref[...] = count_ref[...] + jnp.sum(oh, axis=1, keepdims=True)
    pad = jnp.zeros((INFO_ROWS - 4, tm), F32)
    return jnp.concatenate([cls, rank, wa, wb, pad], axis=0)


def _earlier_matrix(tm):
    i = np.arange(tm)
    return jnp.asarray(i[:, None] < i[None, :], BF16)


def _store_tiled_rows(ref, x):
    tm, d = x.shape
    per = d // LANES
    for c in range(per):
        ref[pl.ds(c, tm, stride=per), :] = x[:, c * LANES:(c + 1) * LANES]


def _load_tiled_rows(ref, tm):
    per = ref.shape[0] // tm
    return jnp.concatenate([ref[pl.ds(c, tm, stride=per), :] for c in range(per)], axis=1)


def _route_outputs(h2, wr_ref, br_ref, earlier_ref, count_ref, pay_ref, info_ref, counts_ref):
    first = (pl.program_id(0) == 0) & (pl.program_id(1) == 0)

    @pl.when(first)
    def _():
        count_ref[...] = jnp.zeros_like(count_ref)

    info = _route(h2, wr_ref[...], br_ref[...], earlier_ref[...], count_ref)
    _store_tiled_rows(pay_ref, h2)
    info_ref[...] = info
    counts_ref[...] = count_ref[...]


def _route_out_shapes(bsz, t, d):
    return (
        jax.ShapeDtypeStruct((bsz * t * (d // LANES), LANES), F32),
        jax.ShapeDtypeStruct((INFO_ROWS, bsz * t), F32),
        jax.ShapeDtypeStruct((CLASS_ROWS, 128), F32),
    )


def _route_out_specs(tm, nt, d):
    return (
        pl.BlockSpec((tm * (d // LANES), LANES), lambda b, i, *_: (b * nt + i, 0)),
        pl.BlockSpec((INFO_ROWS, tm), lambda b, i, *_: (0, b * nt + i)),
        pl.BlockSpec((CLASS_ROWS, 128), lambda b, i, *_: (0, 0)),
    )


def _plan(info, counts, n, tmoe):
    cls = info[0].astype(jnp.int32)
    rank = info[1].astype(jnp.int32)
    cnt = counts[:N_CLASSES, 0].astype(jnp.int32)
    padded = (cnt + tmoe - 1) // tmoe * tmoe
    ends = jnp.cumsum(padded)
    dest = (ends - padded)[cls] + rank
    n_tiles = n // tmoe + N_CLASSES
    rows = jnp.stack([jnp.arange(n, dtype=F32), info[2], info[3], jnp.zeros((n,), F32)], axis=1)
    init = jnp.zeros((n_tiles * tmoe, 4), F32).at[:, 0].set(-1.0)
    srt = init.at[dest].set(rows, unique_indices=True)
    token = srt[:, 0].astype(jnp.int32)
    is_pad = token < 0
    dst = jnp.where(is_pad, n - 1 + jnp.cumsum(is_pad.astype(jnp.int32)), token)
    n_used = ends[-1] // tmoe
    tile = jnp.minimum(jnp.arange(n_tiles + 1, dtype=jnp.int32), n_used - 1)
    tcls = jnp.sum((ends[None, :] <= (tile * tmoe)[:, None]).astype(jnp.int32), axis=1)
    group, pair = tcls // len(_PAIRS), tcls % len(_PAIRS)
    pa = jnp.asarray([p[0] for p in _PAIRS], jnp.int32)[pair]
    pb = jnp.asarray([p[1] for p in _PAIRS], jnp.int32)[pair]
    ea = group * EXPERTS_PER_GROUP + pa
    eb = group * EXPERTS_PER_GROUP + pb
    return dst, srt, ea, eb, n_used.reshape(1).astype(jnp.int32)


def _post0_kernel(o_ref, r_ref, x_ref, g1_ref, sh2_ref, sc2_ref, n2_ref, wout_ref, wr_ref, br_ref,
                  earlier_ref, x1_ref, pay_ref, info_ref, counts_ref, count_ref):
    a = (o_ref[...].astype(F32) * _silu(r_ref[...].astype(F32))).astype(BF16)
    x1 = x_ref[...] + g1_ref[...] * _dot(a, wout_ref[...])
    x1_ref[...] = x1
    h2 = _rmsnorm(x1, n2_ref[...]) * (1.0 + sc2_ref[...]) + sh2_ref[...]
    _route_outputs(h2, wr_ref, br_ref, earlier_ref, count_ref, pay_ref, info_ref, counts_ref)


def _post0(o, r, x, mods4, layer, n2, wout, wr_t, br, tm):
    bsz, t, d = x.shape
    tm = min(tm, t)
    rb = lambda b: b
    tok = lambda w: pl.BlockSpec((None, tm, w), lambda b, i: (b, i, 0))
    full = lambda a: pl.BlockSpec(a.shape, lambda b, i: (0,) * a.ndim)
    nt = t // tm
    earlier = _earlier_matrix(tm)
    return pl.pallas_call(
        _post0_kernel,
        out_shape=(jax.ShapeDtypeStruct((bsz, t, d), F32),) + _route_out_shapes(bsz, t, d),
        grid=(bsz, nt),
        in_specs=[
            tok(d), tok(d), tok(d),
            _mod_spec(layer, 2, d, rb), _mod_spec(layer, 3, d, rb), _mod_spec(layer, 4, d, rb),
            full(n2), full(wout), full(wr_t), full(br), full(earlier),
        ],
        out_specs=(tok(d),) + _route_out_specs(tm, nt, d),
        scratch_shapes=[pltpu.VMEM((CLASS_ROWS, 128), F32)],
        compiler_params=_cparams(("arbitrary", "arbitrary")),
        name="post0",
    )(o, r, x, mods4, mods4, mods4, n2, wout, wr_t, br, earlier)


def _moe_kernel(dst_ref, ea_ref, eb_ref, used_ref, p_hbm, gw_ref, wa_ref, wb_ref, f_hbm,
                pbuf0, pbuf1, obuf0, obuf1, gsem, ssem, *, d, tm, n_tokens):
    del ea_ref, eb_ref
    j = pl.program_id(0)
    n_tiles = pl.num_programs(0) - 1
    used = used_ref[0]
    per = d // LANES
    pbufs, obufs = (pbuf0, pbuf1), (obuf0, obuf1)

    def group(i):
        return pl.ds(pl.multiple_of(i * per, per), per)

    def gather_row(tile, r, p, pred, lane):
        i = dst_ref[tile * tm + r]
        i = jnp.where(i < n_tokens, i, 0)

        @pl.when(pred)
        def _():
            pltpu.make_async_copy(p_hbm.at[group(i), :], pbufs[p].at[group(r), :],
                                  gsem.at[p]).start(priority=lane)

    def scatter_row(tile, r, p, pred, lane):
        i = dst_ref[tile * tm + r]

        @pl.when(pred)
        def _():
            pltpu.make_async_copy(obufs[p].at[group(r), :], f_hbm.at[group(i), :],
                                  ssem.at[p]).start(priority=lane)

    def rolled(row_fn, tile, p):
        def body(blk, carry):
            for u in range(DMA_UNROLL):
                row_fn(tile, blk * DMA_UNROLL + u, p, True, u % 2)
            return carry
        lax.fori_loop(0, tm // DMA_UNROLL, body, 0)

    def gather_wait(p):
        pltpu.make_async_copy(p_hbm.at[pl.ds(0, tm * per), :], pbufs[p], gsem.at[p]).wait()

    def scatter_wait(p):
        pltpu.make_async_copy(obufs[p], f_hbm.at[pl.ds(0, tm * per), :], ssem.at[p]).wait()

    @pl.when(j == 0)
    def _():
        rolled(gather_row, 0, 0)

    def step(p, beside_matmuls):
        prev = jnp.maximum(j - 1, 0)

        @pl.when(j >= 2)
        def _():
            scatter_wait(p)

        @pl.when(j < used)
        def _():
            gather_wait(p)
            has_next = j + 1 < used
            has_prev = j >= 1
            nxt = jnp.minimum(j + 1, n_tiles - 1)
            if beside_matmuls:
                for r in range(tm):
                    gather_row(nxt, r, 1 - p, has_next, r % 2)
            else:
                @pl.when(j >= 0)
                def _():
                    for r in range(tm):
                        gather_row(nxt, r, 1 - p, has_next, r % 2)
                        scatter_row(prev, r, 1 - p, has_prev, r % 2)
            h = _load_tiled_rows(pbufs[p], tm).astype(BF16)
            gw = gw_ref[...]

            def expert(w_ref):
                de = w_ref.shape[2]
                he = (_silu(_dot(h, w_ref[0])) * _dot(h, w_ref[1])).astype(BF16)
                return jnp.concatenate([_dot(he, w_ref[2, :de, :]), _dot(he, w_ref[2, de:, :])], axis=1)

            y = gw[:, 1:2] * expert(wa_ref) + gw[:, 2:3] * expert(wb_ref)
            _store_tiled_rows(obufs[p], y)
            if beside_matmuls:
                for r in range(tm):
                    scatter_row(prev, r, 1 - p, has_prev, r % 2)

        @pl.when((j >= used) & (j < n_tiles))
        def _():
            obufs[p][...] = jnp.zeros(obufs[p].shape, F32)
            rolled(scatter_row, prev, 1 - p)

        @pl.when(j == n_tiles)
        def _():
            rolled(scatter_row, prev, 1 - p)
            scatter_wait(1 - p)

    @pl.when(j % 2 == 0)
    def _():
        step(0, True)

    @pl.when(j % 2 == 1)
    def _():
        step(1, False)


def _moe(payload, dst, srt, ea, eb, n_used, w_all, tm, n):
    n_sorted = dst.shape[0]
    n_tiles = n_sorted // tm
    _, _, d, de = w_all.shape
    per = d // LANES
    wa = lambda j, dst, ea, eb, used: (ea[j], 0, 0, 0)
    wb = lambda j, dst, ea, eb, used: (eb[j], 0, 0, 0)
    return pl.pallas_call(
        functools.partial(_moe_kernel, d=d, tm=tm, n_tokens=n),
        out_shape=jax.ShapeDtypeStruct((n_sorted * per, LANES), F32),
        grid_spec=pltpu.PrefetchScalarGridSpec(
            num_scalar_prefetch=4,
            grid=(n_tiles + 1,),
            in_specs=[
                pl.BlockSpec(memory_space=pl.ANY),
                pl.BlockSpec((tm, srt.shape[1]), lambda j, *_: (jnp.minimum(j, n_tiles - 1), 0)),
                pl.BlockSpec((None, 3, d, de), wa), pl.BlockSpec((None, 3, d, de), wb),
            ],
            out_specs=pl.BlockSpec(memory_space=pl.ANY),
            scratch_shapes=[pltpu.VMEM((tm * per, LANES), F32), pltpu.VMEM((tm * per, LANES), F32),
                            pltpu.VMEM((tm * per, LANES), F32), pltpu.VMEM((tm * per, LANES), F32),
                            pltpu.SemaphoreType.DMA((2,)), pltpu.SemaphoreType.DMA((2,))],
        ),
        compiler_params=_cparams(("arbitrary",)),
        name="moe",
    )(dst, ea, eb, n_used, payload, srt, w_all, w_all)


def _expert_weights(w1, w3, w2):
    de, d = w2.shape[1:]
    assert d == 2 * de
    w2s = jnp.concatenate([w2[:, :, :de], w2[:, :, de:]], axis=1)
    return jnp.stack([w1, w3, w2s], axis=1).astype(BF16)


def _sparse_moe(payload, info, counts, w_all):
    n = info.shape[1]
    tmoe = min(TM_MOE, n)
    dst, srt, ea, eb, n_used = _plan(info, counts, n, tmoe)
    return _moe(payload, dst, srt, ea, eb, n_used, w_all, tmoe, n)


def _pool_constants(tm, d):
    ng = len(POOL_WINDOWS)
    cg = d // ng
    pos = np.arange(tm)
    seg, off = pos // GRID_W, pos % GRID_W
    mats = np.zeros((ng, tm, tm), np.float32)
    inv = np.zeros((tm, d), np.float32)
    for gi, w in enumerate(POOL_WINDOWS):
        lo = np.clip(off - w // 2, 0, GRID_W)
        hi = np.clip(off - w // 2 + w, 0, GRID_W)
        same = seg[:, None] == seg[None, :]
        inside = (off[None, :] >= lo[:, None]) & (off[None, :] < hi[:, None])
        mats[gi] = (same & inside).astype(np.float32)
        inv[:, gi * cg:(gi + 1) * cg] = (1.0 / (hi - lo).astype(np.float32))[:, None]
    return jnp.asarray(mats, BF16), jnp.asarray(inv, F32)


def _layer1_kernel(x_ref, f_ref, g2p_ref, sh1_ref, sc1_ref, g1_ref, sh2_ref, sc2_ref,
                   n1_ref, n2_ref, pm_ref, inv_ref, wp_ref, bp_ref, ps_ref, wr_ref, br_ref, earlier_ref,
                   x3_ref, pay_ref, info_ref, counts_ref, count_ref):
    x2 = x_ref[...] + g2p_ref[...] * _load_tiled_rows(f_ref, x_ref.shape[0])
    h = _rmsnorm(x2, n1_ref[...]) * (1.0 + sc1_ref[...]) + sh1_ref[...]
    hb = h.astype(BF16)
    ng = pm_ref.shape[0]
    cg = h.shape[1] // ng
    ys = []
    for gi in range(ng):
        cs = slice(gi * cg, (gi + 1) * cg)
        wsum = _dot(pm_ref[gi], hb[:, cs])
        pooled = wsum * inv_ref[:, cs] - h[:, cs]
        ys.append(_dot(pooled.astype(BF16), wp_ref[gi]))
    y = (jnp.concatenate(ys, axis=1) + bp_ref[...]) * ps_ref[...]
    x3 = x2 + g1_ref[...] * y
    x3_ref[...] = x3
    h2 = _rmsnorm(x3, n2_ref[...]) * (1.0 + sc2_ref[...]) + sh2_ref[...]
    _route_outputs(h2, wr_ref, br_ref, earlier_ref, count_ref, pay_ref, info_ref, counts_ref)


def _layer1(x1, f, mods4, n1, n2, wp, bp, ps, wr_t, br, tm):
    bsz, t, d = x1.shape
    tm = min(tm, t)
    pm, inv = _pool_constants(tm, d)
    earlier = _earlier_matrix(tm)
    rb = lambda b: b
    tok = lambda w: pl.BlockSpec((None, tm, w), lambda b, i: (b, i, 0))
    full = lambda a: pl.BlockSpec(a.shape, lambda b, i: (0,) * a.ndim)
    nt = t // tm
    return pl.pallas_call(
        _layer1_kernel,
        out_shape=(jax.ShapeDtypeStruct((bsz, t, d), F32),) + _route_out_shapes(bsz, t, d),
        grid=(bsz, nt),
        in_specs=[
            tok(d), pl.BlockSpec((tm * (d // LANES), LANES), lambda b, i: (b * nt + i, 0)),
            _mod_spec(0, 5, d, rb),
            _mod_spec(1, 0, d, rb), _mod_spec(1, 1, d, rb), _mod_spec(1, 2, d, rb),
            _mod_spec(1, 3, d, rb), _mod_spec(1, 4, d, rb),
            full(n1), full(n2), full(pm), full(inv), full(wp), full(bp), full(ps),
            full(wr_t), full(br), full(earlier),
        ],
        out_specs=(tok(d),) + _route_out_specs(tm, nt, d),
        scratch_shapes=[pltpu.VMEM((CLASS_ROWS, 128), F32)],
        compiler_params=_cparams(("arbitrary", "arbitrary")),
        name="layer1",
    )(x1, f, mods4, mods4, mods4, mods4, mods4, mods4, n1, n2, pm, inv, wp, bp, ps, wr_t, br, earlier)


def _final_kernel(x_ref, f_ref, g2_ref, fg_ref, o_ref):
    f = _load_tiled_rows(f_ref, x_ref.shape[0])
    o_ref[...] = _rmsnorm(x_ref[...] + g2_ref[...] * f, fg_ref[...])


def _final(x3, f, mods4, fg, tm):
    bsz, t, d = x3.shape
    tm = min(tm, t)
    nt = t // tm
    tok = pl.BlockSpec((None, tm, d), lambda b, i: (b, i, 0))
    return pl.pallas_call(
        _final_kernel,
        out_shape=jax.ShapeDtypeStruct((bsz, t, d), F32),
        grid=(bsz, nt),
        in_specs=[tok, pl.BlockSpec((tm * (d // LANES), LANES), lambda b, i: (b * nt + i, 0)),
                  _mod_spec(1, 5, d, lambda b: b), pl.BlockSpec((1, d), lambda b, i: (0, 0))],
        out_specs=tok,
        compiler_params=_cparams(("arbitrary", "arbitrary")),
        name="final",
    )(x3, f, mods4, fg)


def kernel(x, c, ctx, c_ctx, norm1_g, norm2_g, w_mod, b_mod, gla_w_in, gla_w_gate_a, gla_w_gate_b,
           gla_b_gate, gla_norm_g, gla_w_out, pool_w, pool_b, pool_scale, w_router, b_router,
           w_gate_e, w_up_e, w_down_e, final_g):
    bsz, t, d = x.shape
    depth = w_mod.shape[0]
    assert depth == 2 and t % GRID_W == 0
    kdim = d // 2
    row = lambda a: a.reshape(1, -1)

    c_rows = -(-(bsz + 1) // 8) * 8
    c_all = jnp.zeros((c_rows, d), F32).at[:bsz].set(c).at[bsz].set(c_ctx)
    mods = _modulation(c_all, w_mod, b_mod)
    mods4 = mods.reshape(depth, c_rows, 1, 6 * d)

    win = gla_w_in[0].astype(BF16)
    wga = jnp.concatenate([gla_w_gate_a[0, 0], gla_w_gate_a[0, 1]], axis=1).astype(BF16)
    dk = kdim // GLA_HEADS
    by_head = lambda a: a.reshape(a.shape[0], GLA_HEADS, 1, dk)
    zero = jnp.zeros((GLA_GATE_RANK, GLA_HEADS, 1, dk), F32)
    wgb = jnp.concatenate([
        jnp.concatenate([by_head(gla_w_gate_b[0, 0]), zero], axis=2),
        jnp.concatenate([zero, by_head(gla_w_gate_b[0, 1])], axis=2)], axis=0)
    wgb = wgb.reshape(2 * GLA_GATE_RANK, 2 * kdim).astype(BF16)
    bg = jnp.stack([gla_b_gate[0, 0].reshape(GLA_HEADS, dk), gla_b_gate[0, 1].reshape(GLA_HEADS, dk)],
                   axis=1).reshape(1, 2 * kdim)
    n1_0 = row(norm1_g[0])
    q, k, v, r, g = _gla_in(x, mods4, 0, lambda b: b, n1_0, win, wga, wgb, bg, TM_PROJ)
    _, kc, vc, _, gc = _gla_in(ctx, mods4, 0, lambda b: bsz, n1_0, win, wga, wgb, bg, TM_CTX)
    s0 = _gla_state(kc, vc, gc, GLA_CHUNK, GLA_BLOCK)
    o = _gla_scan(q, k, v, g, s0, row(gla_norm_g[0]), GLA_CHUNK, GLA_BLOCK)

    wr_t = jnp.transpose(w_router)
    br = b_router.reshape(N_EXPERTS, 1)
    x1, payload, info, counts = _post0(o, r, x, mods4, 0, row(norm2_g[0]),
                                       gla_w_out[0].astype(BF16), wr_t, br, TM_POST)
    f0 = _sparse_moe(payload, info, counts, _expert_weights(w_gate_e[0], w_up_e[0], w_down_e[0]))

    x3, payload, info, counts = _layer1(x1, f0, mods4, row(norm1_g[1]), row(norm2_g[1]),
                                        pool_w[0].astype(BF16), row(pool_b[0]), row(pool_scale[0]),
                                        wr_t, br, TM_POOL)
    f1 = _sparse_moe(payload, info, counts, _expert_weights(w_gate_e[1], w_up_e[1], w_down_e[1]))
    return _final(x3, f1, mods4, row(final_g), TM_FINAL)
```

```python
import functools

import numpy as np
import jax
import jax.numpy as jnp
from jax import lax
from jax.experimental import pallas as pl
from jax.experimental.pallas import tpu as pltpu

EPS = 1e-6
GRID_W = 64
GLA_HEADS = 4
GLA_GATE_RANK = 16
GLA_GATE_NORM = 16.0
GLA_CHUNK = 128
GLA_BLOCK = 256
GLA_BLOCKS_PER_TRIP = 4
GLA_FINISH_BLOCKS_PER_TRIP = 4
POOL_WINDOWS = (2, 4, 8, 16)
N_EXPERTS = 16
N_EXPERT_GROUPS = 4
EXPERTS_PER_GROUP = N_EXPERTS // N_EXPERT_GROUPS
TOP_K = 2

_PAIRS = tuple((a, b) for a in range(EXPERTS_PER_GROUP) for b in range(a + 1, EXPERTS_PER_GROUP))
_PAIR_BASE = (0, 3, 5)
N_CLASSES = N_EXPERT_GROUPS * len(_PAIRS)
CLASS_ROWS = 32
INFO_ROWS = 8
LANES = 128

TM_PROJ = 512
TM_CTX = 256
TM_POST = 512
TM_POOL = 512
TM_MOE = 256
TM_FINAL = 512
DMA_UNROLL = 8

F32 = jnp.float32
BF16 = jnp.bfloat16

_NT = (((1,), (1,)), ((), ()))
_TN = (((0,), (0,)), ((), ()))
_VMEM_LIMIT = 56 * 1024 * 1024


def _cparams(sem):
    return pltpu.CompilerParams(dimension_semantics=sem, vmem_limit_bytes=_VMEM_LIMIT)


def _dot(a, b, dims=None):
    if dims is None:
        return jnp.dot(a, b, preferred_element_type=F32)
    return lax.dot_general(a, b, dims, preferred_element_type=F32)


def _split(a):
    hi = a.astype(BF16)
    lo = (a - hi.astype(F32)).astype(BF16)
    return hi, lo


def _dot3(a, b, dims=None):
    ah, al = _split(a)
    bh, bl = _split(b)
    return _dot(ah, bh, dims) + _dot(ah, bl, dims) + _dot(al, bh, dims)


def _sigmoid(x):
    return 1.0 / (1.0 + jnp.exp(-x))


def _silu(x):
    return x * _sigmoid(x)


def _rmsnorm(xf, g):
    return xf * lax.rsqrt(jnp.mean(xf * xf, axis=-1, keepdims=True) + EPS) * g


def _mod_kernel(c_ref, w_ref, b_ref, o_ref):
    o_ref[...] = _dot3(_silu(c_ref[...]), w_ref[...]) + b_ref[...]


def _modulation(c_all, w_mod, b_mod):
    depth, d, d6 = w_mod.shape
    rows = c_all.shape[0]
    tn = 1536
    return pl.pallas_call(
        _mod_kernel,
        out_shape=jax.ShapeDtypeStruct((depth, rows, d6), F32),
        grid=(depth, d6 // tn),
        in_specs=[
            pl.BlockSpec((rows, d), lambda l, j: (0, 0)),
            pl.BlockSpec((None, d, tn), lambda l, j: (l, 0, j)),
            pl.BlockSpec((None, 1, tn), lambda l, j: (l, 0, j)),
        ],
        out_specs=pl.BlockSpec((None, rows, tn), lambda l, j: (l, 0, j)),
        compiler_params=_cparams(("arbitrary", "arbitrary")),
        name="mod",
    )(c_all, w_mod, b_mod.reshape(depth, 1, d6))


def _mod_spec(layer, chunk, d, row_of_batch):
    return pl.BlockSpec((None, None, 1, d), lambda b, t, *_: (layer, row_of_batch(b), 0, chunk))


def _gla_in_kernel(x_ref, sh_ref, sc_ref, ng_ref, win_ref, wga_ref, wgb_ref, bg_ref,
                   q_ref, k_ref, v_ref, r_ref, g_ref, *, kdim, vdim, qscale):
    h = _rmsnorm(x_ref[...], ng_ref[...]) * (1.0 + sc_ref[...]) + sh_ref[...]
    hb = h.astype(BF16)
    proj = _dot(hb, win_ref[...])
    q_ref[...] = (proj[:, :kdim] * qscale).astype(BF16)
    k_ref[...] = proj[:, kdim:2 * kdim].astype(BF16)
    v_ref[...] = proj[:, 2 * kdim:2 * kdim + vdim].astype(BF16)
    r_ref[...] = proj[:, 2 * kdim + vdim:].astype(BF16)
    low = _dot(hb, wga_ref[...])
    z = _dot(low.astype(BF16), wgb_ref[...]) + bg_ref[...]
    logsig = jnp.minimum(z, 0.0) - jnp.log(1.0 + jnp.exp(-jnp.abs(z)))
    g_ref[...] = logsig * (1.0 / GLA_GATE_NORM)


def _gla_in(x, mods4, layer, row_of_batch, ng, win, wga, wgb, bg, tm):
    bsz, t, d = x.shape
    kdim, vdim = d // 2, d
    tm = min(tm, t)
    tok = lambda w: pl.BlockSpec((None, tm, w), lambda b, i: (b, i, 0))
    full = lambda a: pl.BlockSpec(a.shape, lambda b, i: (0,) * a.ndim)
    kern = functools.partial(_gla_in_kernel, kdim=kdim, vdim=vdim,
                             qscale=float((kdim // GLA_HEADS) ** -0.5))
    return pl.pallas_call(
        kern,
        out_shape=(
            jax.ShapeDtypeStruct((bsz, t, kdim), BF16),
            jax.ShapeDtypeStruct((bsz, t, kdim), BF16),
            jax.ShapeDtypeStruct((bsz, t, vdim), BF16),
            jax.ShapeDtypeStruct((bsz, t, vdim), BF16),
            jax.ShapeDtypeStruct((bsz, t, 2 * kdim), F32),
        ),
        grid=(bsz, t // tm),
        in_specs=[
            tok(d),
            _mod_spec(layer, 0, d, row_of_batch),
            _mod_spec(layer, 1, d, row_of_batch),
            full(ng), full(win), full(wga), full(wgb), full(bg),
        ],
        out_specs=(tok(kdim), tok(kdim), tok(vdim), tok(vdim), tok(2 * kdim)),
        compiler_params=_cparams(("arbitrary", "arbitrary")),
        name="gla_in",
    )(x, mods4, mods4, ng, win, wga, wgb, bg)


def _block_masks(rows, chunk):
    i = np.arange(rows)
    same = (i[:, None] // chunk) == (i[None, :] // chunk)
    lower = same & (i[:, None] >= i[None, :])
    upper = same & (i[:, None] <= i[None, :])
    return jnp.asarray(lower, BF16), jnp.asarray(upper, BF16)


def _per_chunk_row(a, chunk, r):
    rows, w = a.shape
    parts = [jnp.broadcast_to(a[c * chunk + r:c * chunk + r + 1, :], (chunk, w))
             for c in range(rows // chunk)]
    return jnp.concatenate(parts, axis=0)


def _block_terms(q, k, v, g2, lo, up, chunk, want_out):
    dk = g2.shape[1] // 2
    gh, gl = _split(g2)
    pre = _dot(lo, gh) + _dot(lo, gl)
    tot = _per_chunk_row(pre, chunk, chunk - 1)
    fwd = lax.broadcasted_iota(jnp.int32, g2.shape, 1) < dk
    cum = jnp.where(fwd, pre, tot - pre + g2)
    kf = k.astype(F32)
    kf2 = jnp.concatenate([kf, kf], axis=1)
    kl = (kf2 * jnp.exp(tot - cum)).astype(BF16)
    etot = jnp.exp(tot)
    if not want_out:
        return kl, etot, None, None
    mid = _per_chunk_row(cum, chunk, chunk // 2)
    qf = q.astype(F32)
    qf2 = jnp.concatenate([qf, qf], axis=1)
    qe = (qf2 * jnp.exp(cum)).astype(BF16)
    qi = (qf2 * jnp.exp(cum - mid)).astype(BF16)
    ki = (kf2 * jnp.exp(mid - cum)).astype(BF16)
    s = (jnp.where(lo > 0, _dot(qi[:, :dk], ki[:, :dk], _NT), 0.0)
         + jnp.where(up > 0, _dot(qi[:, dk:], ki[:, dk:], _NT), 0.0))
    return kl, etot, qe, _dot(s.astype(BF16), v)


def _store_chunk_states(v, kl, etot, ds_ref, e_ref, first_chunk, chunk):
    for c in range(v.shape[0] // chunk):
        rows = slice(c * chunk, (c + 1) * chunk)
        ds_ref[first_chunk + c] = _dot(v[rows, :], kl[rows, :], _TN)
        e_ref[first_chunk + c] = etot[c * chunk:c * chunk + 8, :]


def _scan_states(s0, ds_ref, e_ref, sc_ref, n, dk):
    def body(c, carry):
        sf, sb = carry
        r = n - 1 - c
        if sc_ref is not None:
            sc_ref[c, :, :dk] = sf.astype(BF16)
            sc_ref[r, :, dk:] = sb.astype(BF16)
        return (sf * e_ref[c, 0:1, :dk] + ds_ref[c, :, :dk],
                sb * e_ref[r, 0:1, dk:] + ds_ref[r, :, dk:])

    return lax.fori_loop(0, n, body, (s0[:, :dk], s0[:, dk:]))


def _gla_state_kernel(k_ref, v_ref, g_ref, lo_ref, up_ref, s_ref, ds_ref, e_ref, *, chunk, block):
    t = k_ref.shape[0]
    dk = k_ref.shape[1]
    per = block // chunk

    def terms(i, carry):
        rows = pl.ds(pl.multiple_of(i * block, block), block)
        v = v_ref[rows, :]
        kl, etot, _, _ = _block_terms(None, k_ref[rows, :], v, g_ref[rows, :], lo_ref[...],
                                      up_ref[...], chunk, False)
        _store_chunk_states(v, kl, etot, ds_ref, e_ref, i * per, chunk)
        return carry

    lax.fori_loop(0, t // block, terms, 0)
    sf, sb = _scan_states(jnp.zeros(s_ref.shape, F32), ds_ref, e_ref, None, t // chunk, dk)
    s_ref[:, :dk] = sf
    s_ref[:, dk:] = sb


def _gla_scan_kernel(q_ref, k_ref, v_ref, g_ref, s0_ref, ng_ref, lo_ref, up_ref,
                     o_ref, oi_ref, qe_ref, ds_ref, e_ref, sc_ref, *, chunk, block):
    t = q_ref.shape[0]
    dk = q_ref.shape[1]
    per = block // chunk

    group = GLA_BLOCKS_PER_TRIP if (t // block) % GLA_BLOCKS_PER_TRIP == 0 else 1

    def terms(i, carry):
        for u in range(group):
            blk = i * group + u
            rows = pl.ds(pl.multiple_of(blk * block, block), block)
            v = v_ref[rows, :]
            kl, etot, qe, oi = _block_terms(q_ref[rows, :], k_ref[rows, :], v, g_ref[rows, :],
                                            lo_ref[...], up_ref[...], chunk, True)
            _store_chunk_states(v, kl, etot, ds_ref, e_ref, blk * per, chunk)
            qe_ref[rows, :] = qe
            oi_ref[rows, :] = oi
        return carry

    lax.fori_loop(0, t // (block * group), terms, 0)
    _scan_states(s0_ref[...], ds_ref, e_ref, sc_ref, t // chunk, dk)

    fgroup = GLA_FINISH_BLOCKS_PER_TRIP if (t // block) % GLA_FINISH_BLOCKS_PER_TRIP == 0 else 1

    def finish(i, carry):
        for u in range(fgroup):
            blk = i * fgroup + u
            rows = pl.ds(pl.multiple_of(blk * block, block), block)
            inter = [_dot(qe_ref[pl.ds(pl.multiple_of(blk * block + c * chunk, chunk), chunk), :],
                          sc_ref[blk * per + c], _NT) for c in range(per)]
            o = oi_ref[rows, :] + jnp.concatenate(inter, axis=0)
            o_ref[rows, :] = _rmsnorm(o, ng_ref[...]).astype(o_ref.dtype)
        return carry

    lax.fori_loop(0, t // (block * fgroup), finish, 0)


def _gla_state(k, v, g, chunk, block):
    bsz, t, kdim = k.shape
    vdim = v.shape[-1]
    nh = GLA_HEADS
    dk, dv = kdim // nh, vdim // nh
    block = min(block, t)
    n = t // chunk
    lo, up = _block_masks(block, chunk)
    mask_spec = pl.BlockSpec((block, block), lambda b, h: (0, 0))
    return pl.pallas_call(
        functools.partial(_gla_state_kernel, chunk=chunk, block=block),
        out_shape=jax.ShapeDtypeStruct((bsz, nh, dv, 2 * dk), F32),
        grid=(bsz, nh),
        in_specs=[
            pl.BlockSpec((None, t, dk), lambda b, h: (b, 0, h)),
            pl.BlockSpec((None, t, dv), lambda b, h: (b, 0, h)),
            pl.BlockSpec((None, t, 2 * dk), lambda b, h: (b, 0, h)),
            mask_spec, mask_spec,
        ],
        out_specs=pl.BlockSpec((None, None, dv, 2 * dk), lambda b, h: (b, h, 0, 0)),
        scratch_shapes=[pltpu.VMEM((n, dv, 2 * dk), F32), pltpu.VMEM((n, 8, 2 * dk), F32)],
        compiler_params=_cparams(("arbitrary", "arbitrary")),
        name="gla_state",
    )(k, v, g, lo, up)


def _gla_scan(q, k, v, g, s0, norm_g, chunk, block):
    bsz, t, kdim = k.shape
    vdim = v.shape[-1]
    nh = GLA_HEADS
    dk, dv = kdim // nh, vdim // nh
    block = min(block, t)
    n = t // chunk
    lo, up = _block_masks(block, chunk)
    kspec = pl.BlockSpec((None, t, dk), lambda b, h: (b, 0, h))
    vspec = pl.BlockSpec((None, t, dv), lambda b, h: (b, 0, h))
    mask_spec = pl.BlockSpec((block, block), lambda b, h: (0, 0))
    return pl.pallas_call(
        functools.partial(_gla_scan_kernel, chunk=chunk, block=block),
        out_shape=jax.ShapeDtypeStruct((bsz, t, vdim), BF16),
        grid=(bsz, nh),
        in_specs=[
            kspec, kspec, vspec,
            pl.BlockSpec((None, t, 2 * dk), lambda b, h: (b, 0, h)),
            pl.BlockSpec((None, None, dv, 2 * dk), lambda b, h: (b, h, 0, 0)),
            pl.BlockSpec((1, dv), lambda b, h: (0, 0)),
            mask_spec, mask_spec,
        ],
        out_specs=vspec,
        scratch_shapes=[
            pltpu.VMEM((t, dv), F32),
            pltpu.VMEM((t, 2 * dk), BF16),
            pltpu.VMEM((n, dv, 2 * dk), F32),
            pltpu.VMEM((n, 8, 2 * dk), F32),
            pltpu.VMEM((n, dv, 2 * dk), BF16),
        ],
        compiler_params=_cparams(("arbitrary", "arbitrary")),
        name="gla_scan",
    )(q, k, v, g, s0, norm_g, lo, up)


def _route(h2, wr_t, br, earlier_bf, count_ref):
    tm = h2.shape[0]
    logits = _dot3(wr_t, h2, _NT)
    scores = _sigmoid(logits)
    sel = scores + br
    row = [sel[e:e + 1, :] for e in range(N_EXPERTS)]
    picked = []
    for e in range(N_EXPERTS):
        g0 = (e // EXPERTS_PER_GROUP) * EXPERTS_PER_GROUP
        ahead_count = jnp.zeros(row[e].shape, jnp.int32)
        for j in range(g0, g0 + EXPERTS_PER_GROUP):
            if j == e:
                continue
            ahead = (row[j] >= row[e]) if j < e else (row[j] > row[e])
            ahead_count = ahead_count + ahead.astype(jnp.int32)
        picked.append(ahead_count < TOP_K)
    zero = jnp.zeros_like(row[0])
    gscore, pair, first_w, second_w = [], [], [], []
    for g in range(N_EXPERT_GROUPS):
        acc, pidx, fw, sw = zero, zero, zero, zero
        seen = None
        for a in range(EXPERTS_PER_GROUP):
            e = g * EXPERTS_PER_GROUP + a
            sc = scores[e:e + 1, :]
            acc = acc + jnp.where(picked[e], row[e], 0.0)
            if seen is None:
                is_first = picked[e]
            else:
                is_first = picked[e] & jnp.logical_not(seen)
                is_second = picked[e] & seen
                sw = sw + jnp.where(is_second, sc, 0.0)
                pidx = pidx + jnp.where(is_second, float(a), 0.0)
            fw = fw + jnp.where(is_first, sc, 0.0)
            if a < len(_PAIR_BASE):
                pidx = pidx + jnp.where(is_first, float(_PAIR_BASE[a] - a - 1), 0.0)
            seen = picked[e] if seen is None else (seen | picked[e])
        gscore.append(acc)
        pair.append(pidx)
        first_w.append(fw)
        second_w.append(sw)
    cls, wa, wb = zero, zero, zero
    for g in range(N_EXPERT_GROUPS):
        ok = None
        for j in range(N_EXPERT_GROUPS):
            if j == g:
                continue
            c = (gscore[g] > gscore[j]) if j < g else (gscore[g] >= gscore[j])
            ok = c if ok is None else (ok & c)
        cls = cls + jnp.where(ok, pair[g] + float(len(_PAIRS) * g), 0.0)
        wa = wa + jnp.where(ok, first_w[g], 0.0)
        wb = wb + jnp.where(ok, second_w[g], 0.0)
    denom = wa + wb
    wa = wa / denom
    wb = wb / denom

    cid = lax.broadcasted_iota(jnp.int32, (CLASS_ROWS, tm), 0).astype(F32)
    onehot = (cid == cls).astype(BF16)
    before = _dot(onehot, earlier_bf)
    oh = onehot.astype(F32)
    base = count_ref[...][:, 0:1]
    rank = jnp.sum(oh * (before + base), axis=0, keepdims=True)
    count_ref[...] = count_ref[...] + jnp.sum(oh, axis=1, keepdims=True)
    pad = jnp.zeros((INFO_ROWS - 4, tm), F32)
    return jnp.concatenate([cls, rank, wa, wb, pad], axis=0)


def _earlier_matrix(tm):
    i = np.arange(tm)
    return jnp.asarray(i[:, None] < i[None, :], BF16)


def _store_tiled_rows(ref, x):
    tm, d = x.shape
    per = d // LANES
    for c in range(per):
        ref[pl.ds(c, tm, stride=per), :] = x[:, c * LANES:(c + 1) * LANES]


def _load_tiled_rows(ref, tm):
    per = ref.shape[0] // tm
    return jnp.concatenate([ref[pl.ds(c, tm, stride=per), :] for c in range(per)], axis=1)


def _route_outputs(h2, wr_ref, br_ref, earlier_ref, count_ref, pay_ref, info_ref, counts_ref):
    first = (pl.program_id(0) == 0) & (pl.program_id(1) == 0)

    @pl.when(first)
    def _():
        count_ref[...] = jnp.zeros_like(count_ref)

    info = _route(h2, wr_ref[...], br_ref[...], earlier_ref[...], count_ref)
    _store_tiled_rows(pay_ref, h2)
    info_ref[...] = info
    counts_ref[...] = count_ref[...]


def _route_out_shapes(bsz, t, d):
    return (
        jax.ShapeDtypeStruct((bsz * t * (d // LANES), LANES), F32),
        jax.ShapeDtypeStruct((INFO_ROWS, bsz * t), F32),
        jax.ShapeDtypeStruct((CLASS_ROWS, 128), F32),
    )


def _route_out_specs(tm, nt, d):
    return (
        pl.BlockSpec((tm * (d // LANES), LANES), lambda b, i, *_: (b * nt + i, 0)),
        pl.BlockSpec((INFO_ROWS, tm), lambda b, i, *_: (0, b * nt + i)),
        pl.BlockSpec((CLASS_ROWS, 128), lambda b, i, *_: (0, 0)),
    )


def _plan(info, counts, n, tmoe):
    cls = info[0].astype(jnp.int32)
    rank = info[1].astype(jnp.int32)
    cnt = counts[:N_CLASSES, 0].astype(jnp.int32)
    padded = (cnt + tmoe - 1) // tmoe * tmoe
    ends = jnp.cumsum(padded)
    dest = (ends - padded)[cls] + rank
    n_tiles = n // tmoe + N_CLASSES
    rows = jnp.stack([jnp.arange(n, dtype=F32), info[2], info[3], jnp.zeros((n,), F32)], axis=1)
    init = jnp.zeros((n_tiles * tmoe, 4), F32).at[:, 0].set(-1.0)
    srt = init.at[dest].set(rows, unique_indices=True)
    token = srt[:, 0].astype(jnp.int32)
    is_pad = token < 0
    dst = jnp.where(is_pad, n - 1 + jnp.cumsum(is_pad.astype(jnp.int32)), token)
    n_used = ends[-1] // tmoe
    tile = jnp.minimum(jnp.arange(n_tiles + 1, dtype=jnp.int32), n_used - 1)
    tcls = jnp.sum((ends[None, :] <= (tile * tmoe)[:, None]).astype(jnp.int32), axis=1)
    group, pair = tcls // len(_PAIRS), tcls % len(_PAIRS)
    pa = jnp.asarray([p[0] for p in _PAIRS], jnp.int32)[pair]
    pb = jnp.asarray([p[1] for p in _PAIRS], jnp.int32)[pair]
    ea = group * EXPERTS_PER_GROUP + pa
    eb = group * EXPERTS_PER_GROUP + pb
    return dst, srt, ea, eb, n_used.reshape(1).astype(jnp.int32)


def _post0_kernel(o_ref, r_ref, x_ref, g1_ref, sh2_ref, sc2_ref, n2_ref, wout_ref, wr_ref, br_ref,
                  earlier_ref, x1_ref, pay_ref, info_ref, counts_ref, count_ref):
    a = (o_ref[...].astype(F32) * _silu(r_ref[...].astype(F32))).astype(BF16)
    x1 = x_ref[...] + g1_ref[...] * _dot(a, wout_ref[...])
    x1_ref[...] = x1
    h2 = _rmsnorm(x1, n2_ref[...]) * (1.0 + sc2_ref[...]) + sh2_ref[...]
    _route_outputs(h2, wr_ref, br_ref, earlier_ref, count_ref, pay_ref, info_ref, counts_ref)


def _post0(o, r, x, mods4, layer, n2, wout, wr_t, br, tm):
    bsz, t, d = x.shape
    tm = min(tm, t)
    rb = lambda b: b
    tok = lambda w: pl.BlockSpec((None, tm, w), lambda b, i: (b, i, 0))
    full = lambda a: pl.BlockSpec(a.shape, lambda b, i: (0,) * a.ndim)
    nt = t // tm
    earlier = _earlier_matrix(tm)
    return pl.pallas_call(
        _post0_kernel,
        out_shape=(jax.ShapeDtypeStruct((bsz, t, d), F32),) + _route_out_shapes(bsz, t, d),
        grid=(bsz, nt),
        in_specs=[
            tok(d), tok(d), tok(d),
            _mod_spec(layer, 2, d, rb), _mod_spec(layer, 3, d, rb), _mod_spec(layer, 4, d, rb),
            full(n2), full(wout), full(wr_t), full(br), full(earlier),
        ],
        out_specs=(tok(d),) + _route_out_specs(tm, nt, d),
        scratch_shapes=[pltpu.VMEM((CLASS_ROWS, 128), F32)],
        compiler_params=_cparams(("arbitrary", "arbitrary")),
        name="post0",
    )(o, r, x, mods4, mods4, mods4, n2, wout, wr_t, br, earlier)


def _moe_kernel(dst_ref, ea_ref, eb_ref, used_ref, p_hbm, gw_ref, wa_ref, wb_ref, f_hbm,
                pbuf0, pbuf1, obuf0, obuf1, gsem, ssem, *, d, tm, n_tokens):
    del ea_ref, eb_ref
    j = pl.program_id(0)
    n_tiles = pl.num_programs(0) - 1
    used = used_ref[0]
    per = d // LANES
    pbufs, obufs = (pbuf0, pbuf1), (obuf0, obuf1)

    def group(i):
        return pl.ds(pl.multiple_of(i * per, per), per)

    def gather_row(tile, r, p, pred, lane):
        i = dst_ref[tile * tm + r]
        i = jnp.where(i < n_tokens, i, 0)

        @pl.when(pred)
        def _():
            pltpu.make_async_copy(p_hbm.at[group(i), :], pbufs[p].at[group(r), :],
                                  gsem.at[p]).start(priority=lane)

    def scatter_row(tile, r, p, pred, lane):
        i = dst_ref[tile * tm + r]

        @pl.when(pred)
        def _():
            pltpu.make_async_copy(obufs[p].at[group(r), :], f_hbm.at[group(i), :],
                                  ssem.at[p]).start(priority=lane)

    def rolled(row_fn, tile, p):
        def body(blk, carry):
            for u in range(DMA_UNROLL):
                row_fn(tile, blk * DMA_UNROLL + u, p, True, u % 2)
            return carry
        lax.fori_loop(0, tm // DMA_UNROLL, body, 0)

    def gather_wait(p):
        pltpu.make_async_copy(p_hbm.at[pl.ds(0, tm * per), :], pbufs[p], gsem.at[p]).wait()

    def scatter_wait(p):
        pltpu.make_async_copy(obufs[p], f_hbm.at[pl.ds(0, tm * per), :], ssem.at[p]).wait()

    @pl.when(j == 0)
    def _():
        rolled(gather_row, 0, 0)

    def step(p, beside_matmuls):
        prev = jnp.maximum(j - 1, 0)

        def obuf_free():
            @pl.when(j >= 2)
            def _():
                scatter_wait(p)

        if beside_matmuls:
            obuf_free()

        @pl.when(j < used)
        def _():
            has_next = j + 1 < used
            has_prev = j >= 1
            nxt = jnp.minimum(j + 1, n_tiles - 1)
            if beside_matmuls:
                gather_wait(p)
                for r in range(tm):
                    gather_row(nxt, r, 1 - p, has_next, r % 2)
            else:
                @pl.when(j >= 0)
                def _():
                    for r in range(tm):
                        gather_row(nxt, r, 1 - p, has_next, r % 2)
                        scatter_row(prev, r, 1 - p, has_prev, r % 2)
                gather_wait(p)
            h = _load_tiled_rows(pbufs[p], tm).astype(BF16)
            gw = gw_ref[...]

            def expert(w_ref):
                de = w_ref.shape[2]
                he = (_silu(_dot(h, w_ref[0])) * _dot(h, w_ref[1])).astype(BF16)
                return jnp.concatenate([_dot(he, w_ref[2, :de, :]), _dot(he, w_ref[2, de:, :])], axis=1)

            y = gw[:, 1:2] * expert(wa_ref) + gw[:, 2:3] * expert(wb_ref)
            if not beside_matmuls:
                obuf_free()
            _store_tiled_rows(obufs[p], y)
            if beside_matmuls:
                for r in range(tm):
                    scatter_row(prev, r, 1 - p, has_prev, r % 2)

        @pl.when((j >= used) & (j < n_tiles))
        def _():
            if not beside_matmuls:
                obuf_free()
            obufs[p][...] = jnp.zeros(obufs[p].shape, F32)
            rolled(scatter_row, prev, 1 - p)

        @pl.when(j == n_tiles)
        def _():
            if not beside_matmuls:
                obuf_free()
            rolled(scatter_row, prev, 1 - p)
            scatter_wait(1 - p)

    @pl.when(j % 2 == 0)
    def _():
        step(0, True)

    @pl.when(j % 2 == 1)
    def _():
        step(1, False)


def _moe(payload, dst, srt, ea, eb, n_used, w_all, layer, tm, n):
    n_sorted = dst.shape[0]
    n_tiles = n_sorted // tm
    _, _, _, d, de = w_all.shape
    per = d // LANES
    wa = lambda j, dst, ea, eb, used: (layer, ea[j], 0, 0, 0)
    wb = lambda j, dst, ea, eb, used: (layer, eb[j], 0, 0, 0)
    return pl.pallas_call(
        functools.partial(_moe_kernel, d=d, tm=tm, n_tokens=n),
        out_shape=jax.ShapeDtypeStruct((n_sorted * per, LANES), F32),
        grid_spec=pltpu.PrefetchScalarGridSpec(
            num_scalar_prefetch=4,
            grid=(n_tiles + 1,),
            in_specs=[
                pl.BlockSpec(memory_space=pl.ANY),
                pl.BlockSpec((tm, srt.shape[1]), lambda j, *_: (jnp.minimum(j, n_tiles - 1), 0)),
                pl.BlockSpec((None, None, 3, d, de), wa), pl.BlockSpec((None, None, 3, d, de), wb),
            ],
            out_specs=pl.BlockSpec(memory_space=pl.ANY),
            scratch_shapes=[pltpu.VMEM((tm * per, LANES), F32), pltpu.VMEM((tm * per, LANES), F32),
                            pltpu.VMEM((tm * per, LANES), F32), pltpu.VMEM((tm * per, LANES), F32),
                            pltpu.SemaphoreType.DMA((2,)), pltpu.SemaphoreType.DMA((2,))],
        ),
        compiler_params=_cparams(("arbitrary",)),
        name="moe",
    )(dst, ea, eb, n_used, payload, srt, w_all, w_all)


def _expert_weights_kernel(w1_ref, w3_ref, w2_ref, o_ref):
    de = w2_ref.shape[0]
    o_ref[0] = w1_ref[...].astype(BF16)
    o_ref[1] = w3_ref[...].astype(BF16)
    o_ref[2, :de, :] = w2_ref[:, :de].astype(BF16)
    o_ref[2, de:, :] = w2_ref[:, de:].astype(BF16)


def _expert_weights(w1, w3, w2):
    nl, ne, d, de = w1.shape
    assert d == 2 * de
    up = pl.BlockSpec((None, None, d, de), lambda l, e: (l, e, 0, 0))
    return pl.pallas_call(
        _expert_weights_kernel,
        out_shape=jax.ShapeDtypeStruct((nl, ne, 3, d, de), BF16),
        grid=(nl, ne),
        in_specs=[up, up, pl.BlockSpec((None, None, de, d), lambda l, e: (l, e, 0, 0))],
        out_specs=pl.BlockSpec((None, None, 3, d, de), lambda l, e: (l, e, 0, 0, 0)),
        compiler_params=_cparams(("arbitrary", "arbitrary")),
        name="expert_weights",
    )(w1, w3, w2)


def _sparse_moe(payload, info, counts, w_all, layer):
    n = info.shape[1]
    tmoe = min(TM_MOE, n)
    dst, srt, ea, eb, n_used = _plan(info, counts, n, tmoe)
    return _moe(payload, dst, srt, ea, eb, n_used, w_all, layer, tmoe, n)


def _pool_constants(tm, d):
    ng = len(POOL_WINDOWS)
    cg = d // ng
    pos = np.arange(tm)
    seg, off = pos // GRID_W, pos % GRID_W
    mats = np.zeros((ng, tm, tm), np.float32)
    inv = np.zeros((tm, d), np.float32)
    for gi, w in enumerate(POOL_WINDOWS):
        lo = np.clip(off - w // 2, 0, GRID_W)
        hi = np.clip(off - w // 2 + w, 0, GRID_W)
        same = seg[:, None] == seg[None, :]
        inside = (off[None, :] >= lo[:, None]) & (off[None, :] < hi[:, None])
        mats[gi] = (same & inside).astype(np.float32)
        inv[:, gi * cg:(gi + 1) * cg] = (1.0 / (hi - lo).astype(np.float32))[:, None]
    return jnp.asarray(mats, BF16), jnp.asarray(inv, F32)


def _layer1_kernel(x_ref, f_ref, g2p_ref, sh1_ref, sc1_ref, g1_ref, sh2_ref, sc2_ref,
                   n1_ref, n2_ref, pm_ref, inv_ref, wp_ref, bp_ref, ps_ref, wr_ref, br_ref, earlier_ref,
                   x3_ref, pay_ref, info_ref, counts_ref, count_ref):
    x2 = x_ref[...] + g2p_ref[...] * _load_tiled_rows(f_ref, x_ref.shape[0])
    h = _rmsnorm(x2, n1_ref[...]) * (1.0 + sc1_ref[...]) + sh1_ref[...]
    hb = h.astype(BF16)
    ng = pm_ref.shape[0]
    cg = h.shape[1] // ng
    ys = []
    for gi in range(ng):
        cs = slice(gi * cg, (gi + 1) * cg)
        wsum = _dot(pm_ref[gi], hb[:, cs])
        pooled = wsum * inv_ref[:, cs] - h[:, cs]
        ys.append(_dot(pooled.astype(BF16), wp_ref[gi]))
    y = (jnp.concatenate(ys, axis=1) + bp_ref[...]) * ps_ref[...]
    x3 = x2 + g1_ref[...] * y
    x3_ref[...] = x3
    h2 = _rmsnorm(x3, n2_ref[...]) * (1.0 + sc2_ref[...]) + sh2_ref[...]
    _route_outputs(h2, wr_ref, br_ref, earlier_ref, count_ref, pay_ref, info_ref, counts_ref)


def _layer1(x1, f, mods4, n1, n2, wp, bp, ps, wr_t, br, tm):
    bsz, t, d = x1.shape
    tm = min(tm, t)
    pm, inv = _pool_constants(tm, d)
    earlier = _earlier_matrix(tm)
    rb = lambda b: b
    tok = lambda w: pl.BlockSpec((None, tm, w), lambda b, i: (b, i, 0))
    full = lambda a: pl.BlockSpec(a.shape, lambda b, i: (0,) * a.ndim)
    nt = t // tm
    return pl.pallas_call(
        _layer1_kernel,
        out_shape=(jax.ShapeDtypeStruct((bsz, t, d), F32),) + _route_out_shapes(bsz, t, d),
        grid=(bsz, nt),
        in_specs=[
            tok(d), pl.BlockSpec((tm * (d // LANES), LANES), lambda b, i: (b * nt + i, 0)),
            _mod_spec(0, 5, d, rb),
            _mod_spec(1, 0, d, rb), _mod_spec(1, 1, d, rb), _mod_spec(1, 2, d, rb),
            _mod_spec(1, 3, d, rb), _mod_spec(1, 4, d, rb),
            full(n1), full(n2), full(pm), full(inv), full(wp), full(bp), full(ps),
            full(wr_t), full(br), full(earlier),
        ],
        out_specs=(tok(d),) + _route_out_specs(tm, nt, d),
        scratch_shapes=[pltpu.VMEM((CLASS_ROWS, 128), F32)],
        compiler_params=_cparams(("arbitrary", "arbitrary")),
        name="layer1",
    )(x1, f, mods4, mods4, mods4, mods4, mods4, mods4, n1, n2, pm, inv, wp, bp, ps, wr_t, br, earlier)


def _final_kernel(x_ref, f_ref, g2_ref, fg_ref, o_ref):
    f = _load_tiled_rows(f_ref, x_ref.shape[0])
    o_ref[...] = _rmsnorm(x_ref[...] + g2_ref[...] * f, fg_ref[...])


def _final(x3, f, mods4, fg, tm):
    bsz, t, d = x3.shape
    tm = min(tm, t)
    nt = t // tm
    tok = pl.BlockSpec((None, tm, d), lambda b, i: (b, i, 0))
    return pl.pallas_call(
        _final_kernel,
        out_shape=jax.ShapeDtypeStruct((bsz, t, d), F32),
        grid=(bsz, nt),
        in_specs=[tok, pl.BlockSpec((tm * (d // LANES), LANES), lambda b, i: (b * nt + i, 0)),
                  _mod_spec(1, 5, d, lambda b: b), pl.BlockSpec((1, d), lambda b, i: (0, 0))],
        out_specs=tok,
        compiler_params=_cparams(("arbitrary", "arbitrary")),
        name="final",
    )(x3, f, mods4, fg)


def kernel(x, c, ctx, c_ctx, norm1_g, norm2_g, w_mod, b_mod, gla_w_in, gla_w_gate_a, gla_w_gate_b,
           gla_b_gate, gla_norm_g, gla_w_out, pool_w, pool_b, pool_scale, w_router, b_router,
           w_gate_e, w_up_e, w_down_e, final_g):
    bsz, t, d = x.shape
    depth = w_mod.shape[0]
    assert depth == 2 and t % GRID_W == 0
    kdim = d // 2
    row = lambda a: a.reshape(1, -1)

    c_rows = -(-(bsz + 1) // 8) * 8
    c_all = jnp.zeros((c_rows, d), F32).at[:bsz].set(c).at[bsz].set(c_ctx)
    mods = _modulation(c_all, w_mod, b_mod)
    mods4 = mods.reshape(depth, c_rows, 1, 6 * d)

    win = gla_w_in[0].astype(BF16)
    wga = jnp.concatenate([gla_w_gate_a[0, 0], gla_w_gate_a[0, 1]], axis=1).astype(BF16)
    dk = kdim // GLA_HEADS
    by_head = lambda a: a.reshape(a.shape[0], GLA_HEADS, 1, dk)
    zero = jnp.zeros((GLA_GATE_RANK, GLA_HEADS, 1, dk), F32)
    wgb = jnp.concatenate([
        jnp.concatenate([by_head(gla_w_gate_b[0, 0]), zero], axis=2),
        jnp.concatenate([zero, by_head(gla_w_gate_b[0, 1])], axis=2)], axis=0)
    wgb = wgb.reshape(2 * GLA_GATE_RANK, 2 * kdim).astype(BF16)
    bg = jnp.stack([gla_b_gate[0, 0].reshape(GLA_HEADS, dk), gla_b_gate[0, 1].reshape(GLA_HEADS, dk)],
                   axis=1).reshape(1, 2 * kdim)
    n1_0 = row(norm1_g[0])
    q, k, v, r, g = _gla_in(x, mods4, 0, lambda b: b, n1_0, win, wga, wgb, bg, TM_PROJ)
    _, kc, vc, _, gc = _gla_in(ctx, mods4, 0, lambda b: bsz, n1_0, win, wga, wgb, bg, TM_CTX)
    s0 = _gla_state(kc, vc, gc, GLA_CHUNK, GLA_BLOCK)
    o = _gla_scan(q, k, v, g, s0, row(gla_norm_g[0]), GLA_CHUNK, GLA_BLOCK)

    wr_t = jnp.transpose(w_router)
    br = b_router.reshape(N_EXPERTS, 1)
    x1, payload, info, counts = _post0(o, r, x, mods4, 0, row(norm2_g[0]),
                                       gla_w_out[0].astype(BF16), wr_t, br, TM_POST)
    w_experts = _expert_weights(w_gate_e, w_up_e, w_down_e)
    f0 = _sparse_moe(payload, info, counts, w_experts, 0)

    x3, payload, info, counts = _layer1(x1, f0, mods4, row(norm1_g[1]), row(norm2_g[1]),
                                        pool_w[0].astype(BF16), row(pool_b[0]), row(pool_scale[0]),
                                        wr_t, br, TM_POOL)
    f1 = _sparse_moe(payload, info, counts, w_experts, 1)
    return _final(x3, f1, mods4, row(final_g), TM_FINAL)
```

```python
import functools

import numpy as np
import jax
import jax.numpy as jnp
from jax import lax
from jax.experimental import pallas as pl
from jax.experimental.pallas import tpu as pltpu

EPS = 1e-6
GRID_W = 64
GLA_HEADS = 4
GLA_GATE_RANK = 16
GLA_GATE_NORM = 16.0
GLA_CHUNK = 128
GLA_BLOCK = 256
GLA_BLOCKS_PER_TRIP = 8
GLA_FINISH_BLOCKS_PER_TRIP = 8
POOL_WINDOWS = (2, 4, 8, 16)
N_EXPERTS = 16
N_EXPERT_GROUPS = 4
EXPERTS_PER_GROUP = N_EXPERTS // N_EXPERT_GROUPS
TOP_K = 2

_PAIRS = tuple((a, b) for a in range(EXPERTS_PER_GROUP) for b in range(a + 1, EXPERTS_PER_GROUP))
_PAIR_BASE = (0, 3, 5)
N_CLASSES = N_EXPERT_GROUPS * len(_PAIRS)
CLASS_ROWS = 32
INFO_ROWS = 8
LANES = 128

TM_PROJ = 1024
TM_CTX = 256
TM_POST = 512
TM_POOL = 512
TM_MOE = 256
TM_FINAL = 512
DMA_UNROLL = 8

F32 = jnp.float32
BF16 = jnp.bfloat16

_NT = (((1,), (1,)), ((), ()))
_TN = (((0,), (0,)), ((), ()))
_VMEM_LIMIT = 56 * 1024 * 1024


def _cparams(sem):
    return pltpu.CompilerParams(dimension_semantics=sem, vmem_limit_bytes=_VMEM_LIMIT)


def _dot(a, b, dims=None):
    if dims is None:
        return jnp.dot(a, b, preferred_element_type=F32)
    return lax.dot_general(a, b, dims, preferred_element_type=F32)


def _split(a):
    hi = a.astype(BF16)
    lo = (a - hi.astype(F32)).astype(BF16)
    return hi, lo


def _dot3(a, b, dims=None):
    ah, al = _split(a)
    bh, bl = _split(b)
    return _dot(ah, bh, dims) + _dot(ah, bl, dims) + _dot(al, bh, dims)


def _sigmoid(x):
    return 1.0 / (1.0 + jnp.exp(-x))


def _silu(x):
    return x * _sigmoid(x)


def _rmsnorm(xf, g):
    return xf * lax.rsqrt(jnp.mean(xf * xf, axis=-1, keepdims=True) + EPS) * g


def _norm_modulate(xf, g, scale, shift):
    return xf * lax.rsqrt(jnp.mean(xf * xf, axis=-1, keepdims=True) + EPS) * (g * (1.0 + scale)) + shift


def _mod_kernel(c_ref, w_ref, b_ref, o_ref):
    o_ref[...] = _dot3(_silu(c_ref[...]), w_ref[...]) + b_ref[...]


def _modulation(c_all, w_mod, b_mod):
    depth, d, d6 = w_mod.shape
    rows = c_all.shape[0]
    tn = 1536
    return pl.pallas_call(
        _mod_kernel,
        out_shape=jax.ShapeDtypeStruct((depth, rows, d6), F32),
        grid=(depth, d6 // tn),
        in_specs=[
            pl.BlockSpec((rows, d), lambda l, j: (0, 0)),
            pl.BlockSpec((None, d, tn), lambda l, j: (l, 0, j)),
            pl.BlockSpec((None, 1, tn), lambda l, j: (l, 0, j)),
        ],
        out_specs=pl.BlockSpec((None, rows, tn), lambda l, j: (l, 0, j)),
        compiler_params=_cparams(("arbitrary", "arbitrary")),
        name="mod",
    )(c_all, w_mod, b_mod.reshape(depth, 1, d6))


def _mod_spec(layer, chunk, d, row_of_batch):
    return pl.BlockSpec((None, None, 1, d), lambda b, t, *_: (layer, row_of_batch(b), 0, chunk))


def _gla_in_kernel(x_ref, sh_ref, sc_ref, ng_ref, win_ref, wga_ref, wgb_ref, bg_ref, *out_refs,
                   kdim, vdim, qscale, state_only):
    hb = _norm_modulate(x_ref[...], ng_ref[...], sc_ref[...], sh_ref[...]).astype(BF16)
    if state_only:
        k_ref, v_ref, g_ref = out_refs
        proj = _dot(hb, win_ref[:, kdim:2 * kdim + vdim])
        k_ref[...] = proj[:, :kdim].astype(BF16)
        v_ref[...] = proj[:, kdim:].astype(BF16)
    else:
        q_ref, k_ref, v_ref, r_ref, g_ref = out_refs
        proj = _dot(hb, win_ref[...])
        q_ref[...] = (proj[:, :kdim] * qscale).astype(BF16)
        k_ref[...] = proj[:, kdim:2 * kdim].astype(BF16)
        v_ref[...] = proj[:, 2 * kdim:2 * kdim + vdim].astype(BF16)
        r_ref[...] = proj[:, 2 * kdim + vdim:].astype(BF16)
    low = _dot(hb, wga_ref[...])
    z = _dot(low.astype(BF16), wgb_ref[...]) + bg_ref[...]
    logsig = jnp.minimum(z, 0.0) - jnp.log(1.0 + jnp.exp(-jnp.abs(z)))
    g_ref[...] = logsig * (1.0 / GLA_GATE_NORM)


def _gla_in(x, mods4, layer, row_of_batch, ng, win, wga, wgb, bg, tm, state_only=False):
    bsz, t, d = x.shape
    kdim, vdim = d // 2, d
    tm = min(tm, t)
    tok = lambda w: pl.BlockSpec((None, tm, w), lambda b, i: (b, i, 0))
    full = lambda a: pl.BlockSpec(a.shape, lambda b, i: (0,) * a.ndim)
    kern = functools.partial(_gla_in_kernel, kdim=kdim, vdim=vdim,
                             qscale=float((kdim // GLA_HEADS) ** -0.5), state_only=state_only)
    widths = (kdim, vdim, 2 * kdim) if state_only else (kdim, kdim, vdim, vdim, 2 * kdim)
    dtypes = (BF16,) * (len(widths) - 1) + (F32,)
    return pl.pallas_call(
        kern,
        out_shape=tuple(jax.ShapeDtypeStruct((bsz, t, w), dt) for w, dt in zip(widths, dtypes)),
        grid=(bsz, t // tm),
        in_specs=[
            tok(d),
            _mod_spec(layer, 0, d, row_of_batch),
            _mod_spec(layer, 1, d, row_of_batch),
            full(ng), full(win), full(wga), full(wgb), full(bg),
        ],
        out_specs=tuple(tok(w) for w in widths),
        compiler_params=_cparams(("arbitrary", "arbitrary")),
        name="gla_in",
    )(x, mods4, mods4, ng, win, wga, wgb, bg)


def _block_masks(rows, chunk):
    i = np.arange(rows)
    same = (i[:, None] // chunk) == (i[None, :] // chunk)
    lower = same & (i[:, None] >= i[None, :])
    upper = same & (i[:, None] <= i[None, :])
    return jnp.asarray(lower, BF16), jnp.asarray(upper, BF16)


def _per_chunk_row(a, chunk, r):
    rows, w = a.shape
    parts = [jnp.broadcast_to(a[c * chunk + r:c * chunk + r + 1, :], (chunk, w))
             for c in range(rows // chunk)]
    return jnp.concatenate(parts, axis=0)


def _block_terms(q, k, v, g2, lo, up, chunk, want_out):
    dk = g2.shape[1] // 2
    gh, gl = _split(g2)
    pre = _dot(lo, gh) + _dot(lo, gl)
    tot = _per_chunk_row(pre, chunk, chunk - 1)
    fwd = lax.broadcasted_iota(jnp.int32, g2.shape, 1) < dk
    cum = jnp.where(fwd, pre, tot - pre + g2)
    kf = k.astype(F32)
    kf2 = jnp.concatenate([kf, kf], axis=1)
    kl = (kf2 * jnp.exp(tot - cum)).astype(BF16)
    etot = jnp.exp(tot)
    if not want_out:
        return kl, etot, None, None
    mid = _per_chunk_row(cum, chunk, chunk // 2)
    qf = q.astype(F32)
    qf2 = jnp.concatenate([qf, qf], axis=1)
    qe = (qf2 * jnp.exp(cum)).astype(BF16)
    qi = (qf2 * jnp.exp(cum - mid)).astype(BF16)
    ki = (kf2 * jnp.exp(mid - cum)).astype(BF16)
    s = (jnp.where(lo > 0, _dot(qi[:, :dk], ki[:, :dk], _NT), 0.0)
         + jnp.where(up > 0, _dot(qi[:, dk:], ki[:, dk:], _NT), 0.0))
    return kl, etot, qe, _dot(s.astype(BF16), v)


def _store_chunk_states(v, kl, etot, ds_ref, e_ref, first_chunk, chunk):
    for c in range(v.shape[0] // chunk):
        rows = slice(c * chunk, (c + 1) * chunk)
        ds_ref[first_chunk + c] = _dot(v[rows, :], kl[rows, :], _TN)
        e_ref[first_chunk + c] = etot[c * chunk:c * chunk + 8, :]


def _scan_states(s0, ds_ref, e_ref, sc_ref, n, dk):
    def body(c, carry):
        sf, sb = carry
        r = n - 1 - c
        if sc_ref is not None:
            sc_ref[c, :, :dk] = sf.astype(BF16)
            sc_ref[r, :, dk:] = sb.astype(BF16)
        return (sf * e_ref[c, 0:1, :dk] + ds_ref[c, :, :dk],
                sb * e_ref[r, 0:1, dk:] + ds_ref[r, :, dk:])

    return lax.fori_loop(0, n, body, (s0[:, :dk], s0[:, dk:]))


def _gla_state_kernel(k_ref, v_ref, g_ref, lo_ref, up_ref, s_ref, ds_ref, e_ref, *, chunk, block):
    t = k_ref.shape[0]
    dk = k_ref.shape[1]
    per = block // chunk

    def terms(i, carry):
        rows = pl.ds(pl.multiple_of(i * block, block), block)
        v = v_ref[rows, :]
        kl, etot, _, _ = _block_terms(None, k_ref[rows, :], v, g_ref[rows, :], lo_ref[...],
                                      up_ref[...], chunk, False)
        _store_chunk_states(v, kl, etot, ds_ref, e_ref, i * per, chunk)
        return carry

    lax.fori_loop(0, t // block, terms, 0)
    sf, sb = _scan_states(jnp.zeros(s_ref.shape, F32), ds_ref, e_ref, None, t // chunk, dk)
    s_ref[:, :dk] = sf
    s_ref[:, dk:] = sb


def _gla_scan_kernel(q_ref, k_ref, v_ref, g_ref, s0_ref, ng_ref, lo_ref, up_ref,
                     o_ref, oi_ref, qe_ref, ds_ref, e_ref, sc_ref, *, chunk, block):
    t = q_ref.shape[0]
    dk = q_ref.shape[1]
    per = block // chunk

    group = GLA_BLOCKS_PER_TRIP if (t // block) % GLA_BLOCKS_PER_TRIP == 0 else 1

    def terms(i, carry):
        for u in range(group):
            blk = i * group + u
            rows = pl.ds(pl.multiple_of(blk * block, block), block)
            v = v_ref[rows, :]
            kl, etot, qe, oi = _block_terms(q_ref[rows, :], k_ref[rows, :], v, g_ref[rows, :],
                                            lo_ref[...], up_ref[...], chunk, True)
            _store_chunk_states(v, kl, etot, ds_ref, e_ref, blk * per, chunk)
            qe_ref[rows, :] = qe
            oi_ref[rows, :] = oi
        return carry

    lax.fori_loop(0, t // (block * group), terms, 0)
    _scan_states(s0_ref[...], ds_ref, e_ref, sc_ref, t // chunk, dk)

    fgroup = GLA_FINISH_BLOCKS_PER_TRIP if (t // block) % GLA_FINISH_BLOCKS_PER_TRIP == 0 else 1

    def finish(i, carry):
        for u in range(fgroup):
            blk = i * fgroup + u
            rows = pl.ds(pl.multiple_of(blk * block, block), block)
            inter = [_dot(qe_ref[pl.ds(pl.multiple_of(blk * block + c * chunk, chunk), chunk), :],
                          sc_ref[blk * per + c], _NT) for c in range(per)]
            o = oi_ref[rows, :] + jnp.concatenate(inter, axis=0)
            o_ref[rows, :] = _rmsnorm(o, ng_ref[...]).astype(o_ref.dtype)
        return carry

    lax.fori_loop(0, t // (block * fgroup), finish, 0)


def _gla_state(k, v, g, chunk, block):
    bsz, t, kdim = k.shape
    vdim = v.shape[-1]
    nh = GLA_HEADS
    dk, dv = kdim // nh, vdim // nh
    block = min(block, t)
    n = t // chunk
    lo, up = _block_masks(block, chunk)
    mask_spec = pl.BlockSpec((block, block), lambda b, h: (0, 0))
    return pl.pallas_call(
        functools.partial(_gla_state_kernel, chunk=chunk, block=block),
        out_shape=jax.ShapeDtypeStruct((bsz, nh, dv, 2 * dk), F32),
        grid=(bsz, nh),
        in_specs=[
            pl.BlockSpec((None, t, dk), lambda b, h: (b, 0, h)),
            pl.BlockSpec((None, t, dv), lambda b, h: (b, 0, h)),
            pl.BlockSpec((None, t, 2 * dk), lambda b, h: (b, 0, h)),
            mask_spec, mask_spec,
        ],
        out_specs=pl.BlockSpec((None, None, dv, 2 * dk), lambda b, h: (b, h, 0, 0)),
        scratch_shapes=[pltpu.VMEM((n, dv, 2 * dk), F32), pltpu.VMEM((n, 8, 2 * dk), F32)],
        compiler_params=_cparams(("arbitrary", "arbitrary")),
        name="gla_state",
    )(k, v, g, lo, up)


def _gla_scan(q, k, v, g, s0, norm_g, chunk, block):
    bsz, t, kdim = k.shape
    vdim = v.shape[-1]
    nh = GLA_HEADS
    dk, dv = kdim // nh, vdim // nh
    block = min(block, t)
    n = t // chunk
    lo, up = _block_masks(block, chunk)
    kspec = pl.BlockSpec((None, t, dk), lambda b, h: (b, 0, h))
    vspec = pl.BlockSpec((None, t, dv), lambda b, h: (b, 0, h))
    mask_spec = pl.BlockSpec((block, block), lambda b, h: (0, 0))
    return pl.pallas_call(
        functools.partial(_gla_scan_kernel, chunk=chunk, block=block),
        out_shape=jax.ShapeDtypeStruct((bsz, t, vdim), BF16),
        grid=(bsz, nh),
        in_specs=[
            kspec, kspec, vspec,
            pl.BlockSpec((None, t, 2 * dk), lambda b, h: (b, 0, h)),
            pl.BlockSpec((None, None, dv, 2 * dk), lambda b, h: (b, h, 0, 0)),
            pl.BlockSpec((1, dv), lambda b, h: (0, 0)),
            mask_spec, mask_spec,
        ],
        out_specs=vspec,
        scratch_shapes=[
            pltpu.VMEM((t, dv), F32),
            pltpu.VMEM((t, 2 * dk), BF16),
            pltpu.VMEM((n, dv, 2 * dk), F32),
            pltpu.VMEM((n, 8, 2 * dk), F32),
            pltpu.VMEM((n, dv, 2 * dk), BF16),
        ],
        compiler_params=_cparams(("arbitrary", "arbitrary")),
        name="gla_scan",
    )(q, k, v, g, s0, norm_g, lo, up)


def _route(h2, wr_t, br, earlier_bf, count_ref):
    tm = h2.shape[0]
    logits = _dot3(wr_t, h2, _NT)
    scores = _sigmoid(logits)
    sel = scores + br
    row = [sel[e:e + 1, :] for e in range(N_EXPERTS)]
    picked = []
    for e in range(N_EXPERTS):
        g0 = (e // EXPERTS_PER_GROUP) * EXPERTS_PER_GROUP
        ahead_count = jnp.zeros(row[e].shape, jnp.int32)
        for j in range(g0, g0 + EXPERTS_PER_GROUP):
            if j == e:
                continue
            ahead = (row[j] >= row[e]) if j < e else (row[j] > row[e])
            ahead_count = ahead_count + ahead.astype(jnp.int32)
        picked.append(ahead_count < TOP_K)
    zero = jnp.zeros_like(row[0])
    gscore, pair, first_w, second_w = [], [], [], []
    for g in range(N_EXPERT_GROUPS):
        acc, pidx, fw, sw = zero, zero, zero, zero
        seen = None
        for a in range(EXPERTS_PER_GROUP):
            e = g * EXPERTS_PER_GROUP + a
            sc = scores[e:e + 1, :]
            acc = acc + jnp.where(picked[e], row[e], 0.0)
            if seen is None:
                is_first = picked[e]
            else:
                is_first = picked[e] & jnp.logical_not(seen)
                is_second = picked[e] & seen
                sw = sw + jnp.where(is_second, sc, 0.0)
                pidx = pidx + jnp.where(is_second, float(a), 0.0)
            fw = fw + jnp.where(is_first, sc, 0.0)
            if a < len(_PAIR_BASE):
                pidx = pidx + jnp.where(is_first, float(_PAIR_BASE[a] - a - 1), 0.0)
            seen = picked[e] if seen is None else (seen | picked[e])
        gscore.append(acc)
        pair.append(pidx)
        first_w.append(fw)
        second_w.append(sw)
    cls, wa, wb = zero, zero, zero
    for g in range(N_EXPERT_GROUPS):
        ok = None
        for j in range(N_EXPERT_GROUPS):
            if j == g:
                continue
            c = (gscore[g] > gscore[j]) if j < g else (gscore[g] >= gscore[j])
            ok = c if ok is None else (ok & c)
        cls = cls + jnp.where(ok, pair[g] + float(len(_PAIRS) * g), 0.0)
        wa = wa + jnp.where(ok, first_w[g], 0.0)
        wb = wb + jnp.where(ok, second_w[g], 0.0)
    denom = wa + wb
    wa = wa / denom
    wb = wb / denom

    cid = lax.broadcasted_iota(jnp.int32, (CLASS_ROWS, tm), 0).astype(F32)
    onehot = (cid == cls).astype(BF16)
    before = _dot(onehot, earlier_bf)
    oh = onehot.astype(F32)
    base = count_ref[...][:, 0:1]
    rank = jnp.sum(oh * (before + base), axis=0, keepdims=True)
    count_ref[...] = count_ref[...] + jnp.sum(oh, axis=1, keepdims=True)
    pad = jnp.zeros((INFO_ROWS - 4, tm), F32)
    return jnp.concatenate([cls, rank, wa, wb, pad], axis=0)


def _earlier_matrix(tm):
    i = np.arange(tm)
    return jnp.asarray(i[:, None] < i[None, :], BF16)


def _store_tiled_rows(ref, x):
    tm, d = x.shape
    per = d // LANES
    for c in range(per):
        ref[pl.ds(c, tm, stride=per), :] = x[:, c * LANES:(c + 1) * LANES]


def _load_tiled_rows(ref, tm):
    per = ref.shape[0] // tm
    return jnp.concatenate([ref[pl.ds(c, tm, stride=per), :] for c in range(per)], axis=1)


def _route_outputs(h2, wr_ref, br_ref, earlier_ref, count_ref, pay_ref, info_ref, counts_ref):
    first = (pl.program_id(0) == 0) & (pl.program_id(1) == 0)

    @pl.when(first)
    def _():
        count_ref[...] = jnp.zeros_like(count_ref)

    info = _route(h2, wr_ref[...], br_ref[...], earlier_ref[...], count_ref)
    _store_tiled_rows(pay_ref, h2)
    info_ref[...] = info
    counts_ref[...] = count_ref[...]


def _route_out_shapes(bsz, t, d):
    return (
        jax.ShapeDtypeStruct((bsz * t * (d // LANES), LANES), F32),
        jax.ShapeDtypeStruct((INFO_ROWS, bsz * t), F32),
        jax.ShapeDtypeStruct((CLASS_ROWS, 128), F32),
    )


def _route_out_specs(tm, nt, d):
    return (
        pl.BlockSpec((tm * (d // LANES), LANES), lambda b, i, *_: (b * nt + i, 0)),
        pl.BlockSpec((INFO_ROWS, tm), lambda b, i, *_: (0, b * nt + i)),
        pl.BlockSpec((CLASS_ROWS, 128), lambda b, i, *_: (0, 0)),
    )


def _plan(info, counts, n, tmoe):
    cls = info[0].astype(jnp.int32)
    rank = info[1].astype(jnp.int32)
    cnt = counts[:N_CLASSES, 0].astype(jnp.int32)
    padded = (cnt + tmoe - 1) // tmoe * tmoe
    ends = jnp.cumsum(padded)
    dest = (ends - padded)[cls] + rank
    n_tiles = n // tmoe + N_CLASSES
    rows = jnp.stack([jnp.arange(n, dtype=F32), info[2], info[3], jnp.zeros((n,), F32)], axis=1)
    init = jnp.zeros((n_tiles * tmoe, 4), F32).at[:, 0].set(-1.0)
    srt = init.at[dest].set(rows, unique_indices=True)
    token = srt[:, 0].astype(jnp.int32)
    is_pad = token < 0
    dst = jnp.where(is_pad, n - 1 + jnp.cumsum(is_pad.astype(jnp.int32)), token)
    n_used = ends[-1] // tmoe
    tile = jnp.minimum(jnp.arange(n_tiles + 1, dtype=jnp.int32), n_used - 1)
    tcls = jnp.sum((ends[None, :] <= (tile * tmoe)[:, None]).astype(jnp.int32), axis=1)
    group, pair = tcls // len(_PAIRS), tcls % len(_PAIRS)
    pa = jnp.asarray([p[0] for p in _PAIRS], jnp.int32)[pair]
    pb = jnp.asarray([p[1] for p in _PAIRS], jnp.int32)[pair]
    ea = group * EXPERTS_PER_GROUP + pa
    eb = group * EXPERTS_PER_GROUP + pb
    return dst, srt, ea, eb, n_used.reshape(1).astype(jnp.int32)


def _post0_kernel(o_ref, r_ref, x_ref, g1_ref, sh2_ref, sc2_ref, n2_ref, wout_ref, wr_ref, br_ref,
                  earlier_ref, x1_ref, pay_ref, info_ref, counts_ref, count_ref):
    a = (o_ref[...].astype(F32) * _silu(r_ref[...].astype(F32))).astype(BF16)
    x1 = x_ref[...] + g1_ref[...] * _dot(a, wout_ref[...])
    x1_ref[...] = x1
    h2 = _norm_modulate(x1, n2_ref[...], sc2_ref[...], sh2_ref[...])
    _route_outputs(h2, wr_ref, br_ref, earlier_ref, count_ref, pay_ref, info_ref, counts_ref)


def _post0(o, r, x, mods4, layer, n2, wout, wr_t, br, tm):
    bsz, t, d = x.shape
    tm = min(tm, t)
    rb = lambda b: b
    tok = lambda w: pl.BlockSpec((None, tm, w), lambda b, i: (b, i, 0))
    full = lambda a: pl.BlockSpec(a.shape, lambda b, i: (0,) * a.ndim)
    nt = t // tm
    earlier = _earlier_matrix(tm)
    return pl.pallas_call(
        _post0_kernel,
        out_shape=(jax.ShapeDtypeStruct((bsz, t, d), F32),) + _route_out_shapes(bsz, t, d),
        grid=(bsz, nt),
        in_specs=[
            tok(d), tok(d), tok(d),
            _mod_spec(layer, 2, d, rb), _mod_spec(layer, 3, d, rb), _mod_spec(layer, 4, d, rb),
            full(n2), full(wout), full(wr_t), full(br), full(earlier),
        ],
        out_specs=(tok(d),) + _route_out_specs(tm, nt, d),
        scratch_shapes=[pltpu.VMEM((CLASS_ROWS, 128), F32)],
        compiler_params=_cparams(("arbitrary", "arbitrary")),
        name="post0",
    )(o, r, x, mods4, mods4, mods4, n2, wout, wr_t, br, earlier)


def _moe_kernel(dst_ref, ea_ref, eb_ref, used_ref, p_hbm, gw_ref, wa_ref, wb_ref, f_hbm,
                pbuf0, pbuf1, obuf0, obuf1, gsem, ssem, *, d, tm, n_tokens):
    del ea_ref, eb_ref
    j = pl.program_id(0)
    n_tiles = pl.num_programs(0) - 1
    used = used_ref[0]
    per = d // LANES
    pbufs, obufs = (pbuf0, pbuf1), (obuf0, obuf1)

    def group(i):
        return pl.ds(pl.multiple_of(i * per, per), per)

    def gather_row(tile, r, p, pred, lane):
        i = dst_ref[tile * tm + r]
        i = jnp.where(i < n_tokens, i, 0)

        @pl.when(pred)
        def _():
            pltpu.make_async_copy(p_hbm.at[group(i), :], pbufs[p].at[group(r), :],
                                  gsem.at[p]).start(priority=lane)

    def scatter_row(tile, r, p, pred, lane):
        i = dst_ref[tile * tm + r]

        @pl.when(pred)
        def _():
            pltpu.make_async_copy(obufs[p].at[group(r), :], f_hbm.at[group(i), :],
                                  ssem.at[p]).start(priority=lane)

    def rolled(row_fn, tile, p):
        def body(blk, carry):
            for u in range(DMA_UNROLL):
                row_fn(tile, blk * DMA_UNROLL + u, p, True, u % 2)
            return carry
        lax.fori_loop(0, tm // DMA_UNROLL, body, 0)

    def gather_wait(p):
        pltpu.make_async_copy(p_hbm.at[pl.ds(0, tm * per), :], pbufs[p], gsem.at[p]).wait()

    def scatter_wait(p):
        pltpu.make_async_copy(obufs[p], f_hbm.at[pl.ds(0, tm * per), :], ssem.at[p]).wait()

    @pl.when(j == 0)
    def _():
        rolled(gather_row, 0, 0)

    def step(p, beside_matmuls):
        prev = jnp.maximum(j - 1, 0)

        def obuf_free():
            @pl.when(j >= 2)
            def _():
                scatter_wait(p)

        if beside_matmuls:
            obuf_free()

        @pl.when(j < used)
        def _():
            has_next = j + 1 < used
            has_prev = j >= 1
            nxt = jnp.minimum(j + 1, n_tiles - 1)
            if beside_matmuls:
                gather_wait(p)
                for r in range(tm):
                    gather_row(nxt, r, 1 - p, has_next, r % 2)
            else:
                @pl.when(j >= 0)
                def _():
                    for r in range(tm):
                        gather_row(nxt, r, 1 - p, has_next, r % 2)
                        scatter_row(prev, r, 1 - p, has_prev, r % 2)
                gather_wait(p)
            h = _load_tiled_rows(pbufs[p], tm).astype(BF16)
            gw = gw_ref[...]

            def expert(w_ref):
                de = w_ref.shape[2]
                he = (_silu(_dot(h, w_ref[0])) * _dot(h, w_ref[1])).astype(BF16)
                return jnp.concatenate([_dot(he, w_ref[2, :de, :]), _dot(he, w_ref[2, de:, :])], axis=1)

            y = gw[:, 1:2] * expert(wa_ref) + gw[:, 2:3] * expert(wb_ref)
            if not beside_matmuls:
                obuf_free()
            _store_tiled_rows(obufs[p], y)
            if beside_matmuls:
                for r in range(tm):
                    scatter_row(prev, r, 1 - p, has_prev, r % 2)

        @pl.when((j >= used) & (j < n_tiles))
        def _():
            if not beside_matmuls:
                obuf_free()
            obufs[p][...] = jnp.zeros(obufs[p].shape, F32)
            rolled(scatter_row, prev, 1 - p)

        @pl.when(j == n_tiles)
        def _():
            if not beside_matmuls:
                obuf_free()
            rolled(scatter_row, prev, 1 - p)
            scatter_wait(1 - p)

    @pl.when(j % 2 == 0)
    def _():
        step(0, True)

    @pl.when(j % 2 == 1)
    def _():
        step(1, False)


def _moe(payload, dst, srt, ea, eb, n_used, w_all, layer, tm, n):
    n_sorted = dst.shape[0]
    n_tiles = n_sorted // tm
    _, _, _, d, de = w_all.shape
    per = d // LANES
    wa = lambda j, dst, ea, eb, used: (layer, ea[j], 0, 0, 0)
    wb = lambda j, dst, ea, eb, used: (layer, eb[j], 0, 0, 0)
    return pl.pallas_call(
        functools.partial(_moe_kernel, d=d, tm=tm, n_tokens=n),
        out_shape=jax.ShapeDtypeStruct((n_sorted * per, LANES), F32),
        grid_spec=pltpu.PrefetchScalarGridSpec(
            num_scalar_prefetch=4,
            grid=(n_tiles + 1,),
            in_specs=[
                pl.BlockSpec(memory_space=pl.ANY),
                pl.BlockSpec((tm, srt.shape[1]), lambda j, *_: (jnp.minimum(j, n_tiles - 1), 0)),
                pl.BlockSpec((None, None, 3, d, de), wa), pl.BlockSpec((None, None, 3, d, de), wb),
            ],
            out_specs=pl.BlockSpec(memory_space=pl.ANY),
            scratch_shapes=[pltpu.VMEM((tm * per, LANES), F32), pltpu.VMEM((tm * per, LANES), F32),
                            pltpu.VMEM((tm * per, LANES), F32), pltpu.VMEM((tm * per, LANES), F32),
                            pltpu.SemaphoreType.DMA((2,)), pltpu.SemaphoreType.DMA((2,))],
        ),
        compiler_params=_cparams(("arbitrary",)),
        name="moe",
    )(dst, ea, eb, n_used, payload, srt, w_all, w_all)


def _expert_weights_kernel(w1_ref, w3_ref, w2_ref, o_ref):
    de = w2_ref.shape[0]
    o_ref[0] = w1_ref[...].astype(BF16)
    o_ref[1] = w3_ref[...].astype(BF16)
    o_ref[2, :de, :] = w2_ref[:, :de].astype(BF16)
    o_ref[2, de:, :] = w2_ref[:, de:].astype(BF16)


def _expert_weights(w1, w3, w2):
    nl, ne, d, de = w1.shape
    assert d == 2 * de
    up = pl.BlockSpec((None, None, d, de), lambda l, e: (l, e, 0, 0))
    return pl.pallas_call(
        _expert_weights_kernel,
        out_shape=jax.ShapeDtypeStruct((nl, ne, 3, d, de), BF16),
        grid=(nl, ne),
        in_specs=[up, up, pl.BlockSpec((None, None, de, d), lambda l, e: (l, e, 0, 0))],
        out_specs=pl.BlockSpec((None, None, 3, d, de), lambda l, e: (l, e, 0, 0, 0)),
        compiler_params=_cparams(("arbitrary", "arbitrary")),
        name="expert_weights",
    )(w1, w3, w2)


def _sparse_moe(payload, info, counts, w_all, layer):
    n = info.shape[1]
    tmoe = min(TM_MOE, n)
    dst, srt, ea, eb, n_used = _plan(info, counts, n, tmoe)
    return _moe(payload, dst, srt, ea, eb, n_used, w_all, layer, tmoe, n)


def _pool_constants(tm, d):
    ng = len(POOL_WINDOWS)
    cg = d // ng
    pos = np.arange(tm)
    seg, off = pos // GRID_W, pos % GRID_W
    mats = np.zeros((ng, tm, tm), np.float32)
    inv = np.zeros((tm, d), np.float32)
    for gi, w in enumerate(POOL_WINDOWS):
        lo = np.clip(off - w // 2, 0, GRID_W)
        hi = np.clip(off - w // 2 + w, 0, GRID_W)
        same = seg[:, None] == seg[None, :]
        inside = (off[None, :] >= lo[:, None]) & (off[None, :] < hi[:, None])
        mats[gi] = (same & inside).astype(np.float32)
        inv[:, gi * cg:(gi + 1) * cg] = (1.0 / (hi - lo).astype(np.float32))[:, None]
    return jnp.asarray(mats, BF16), jnp.asarray(inv, F32)


def _layer1_kernel(x_ref, f_ref, g2p_ref, sh1_ref, sc1_ref, g1_ref, sh2_ref, sc2_ref,
                   n1_ref, n2_ref, pm_ref, inv_ref, wp_ref, bp_ref, ps_ref, wr_ref, br_ref, earlier_ref,
                   x3_ref, pay_ref, info_ref, counts_ref, count_ref):
    x2 = x_ref[...] + g2p_ref[...] * _load_tiled_rows(f_ref, x_ref.shape[0])
    h = _norm_modulate(x2, n1_ref[...], sc1_ref[...], sh1_ref[...])
    hb = h.astype(BF16)
    ng = pm_ref.shape[0]
    cg = h.shape[1] // ng
    ys = []
    for gi in range(ng):
        cs = slice(gi * cg, (gi + 1) * cg)
        wsum = _dot(pm_ref[gi], hb[:, cs])
        pooled = wsum * inv_ref[:, cs] - h[:, cs]
        ys.append(_dot(pooled.astype(BF16), wp_ref[gi]))
    y = (jnp.concatenate(ys, axis=1) + bp_ref[...]) * ps_ref[...]
    x3 = x2 + g1_ref[...] * y
    x3_ref[...] = x3
    h2 = _norm_modulate(x3, n2_ref[...], sc2_ref[...], sh2_ref[...])
    _route_outputs(h2, wr_ref, br_ref, earlier_ref, count_ref, pay_ref, info_ref, counts_ref)


def _layer1(x1, f, mods4, n1, n2, wp, bp, ps, wr_t, br, tm):
    bsz, t, d = x1.shape
    tm = min(tm, t)
    pm, inv = _pool_constants(tm, d)
    earlier = _earlier_matrix(tm)
    rb = lambda b: b
    tok = lambda w: pl.BlockSpec((None, tm, w), lambda b, i: (b, i, 0))
    full = lambda a: pl.BlockSpec(a.shape, lambda b, i: (0,) * a.ndim)
    nt = t // tm
    return pl.pallas_call(
        _layer1_kernel,
        out_shape=(jax.ShapeDtypeStruct((bsz, t, d), F32),) + _route_out_shapes(bsz, t, d),
        grid=(bsz, nt),
        in_specs=[
            tok(d), pl.BlockSpec((tm * (d // LANES), LANES), lambda b, i: (b * nt + i, 0)),
            _mod_spec(0, 5, d, rb),
            _mod_spec(1, 0, d, rb), _mod_spec(1, 1, d, rb), _mod_spec(1, 2, d, rb),
            _mod_spec(1, 3, d, rb), _mod_spec(1, 4, d, rb),
            full(n1), full(n2), full(pm), full(inv), full(wp), full(bp), full(ps),
            full(wr_t), full(br), full(earlier),
        ],
        out_specs=(tok(d),) + _route_out_specs(tm, nt, d),
        scratch_shapes=[pltpu.VMEM((CLASS_ROWS, 128), F32)],
        compiler_params=_cparams(("arbitrary", "arbitrary")),
        name="layer1",
    )(x1, f, mods4, mods4, mods4, mods4, mods4, mods4, n1, n2, pm, inv, wp, bp, ps, wr_t, br, earlier)


def _final_kernel(x_ref, f_ref, g2_ref, fg_ref, o_ref):
    f = _load_tiled_rows(f_ref, x_ref.shape[0])
    o_ref[...] = _rmsnorm(x_ref[...] + g2_ref[...] * f, fg_ref[...])


def _final(x3, f, mods4, fg, tm):
    bsz, t, d = x3.shape
    tm = min(tm, t)
    nt = t // tm
    tok = pl.BlockSpec((None, tm, d), lambda b, i: (b, i, 0))
    return pl.pallas_call(
        _final_kernel,
        out_shape=jax.ShapeDtypeStruct((bsz, t, d), F32),
        grid=(bsz, nt),
        in_specs=[tok, pl.BlockSpec((tm * (d // LANES), LANES), lambda b, i: (b * nt + i, 0)),
                  _mod_spec(1, 5, d, lambda b: b), pl.BlockSpec((1, d), lambda b, i: (0, 0))],
        out_specs=tok,
        compiler_params=_cparams(("arbitrary", "arbitrary")),
        name="final",
    )(x3, f, mods4, fg)


def kernel(x, c, ctx, c_ctx, norm1_g, norm2_g, w_mod, b_mod, gla_w_in, gla_w_gate_a, gla_w_gate_b,
           gla_b_gate, gla_norm_g, gla_w_out, pool_w, pool_b, pool_scale, w_router, b_router,
           w_gate_e, w_up_e, w_down_e, final_g):
    bsz, t, d = x.shape
    depth = w_mod.shape[0]
    assert depth == 2 and t % GRID_W == 0
    kdim = d // 2
    row = lambda a: a.reshape(1, -1)

    c_rows = -(-(bsz + 1) // 8) * 8
    c_all = jnp.zeros((c_rows, d), F32).at[:bsz].set(c).at[bsz].set(c_ctx)
    mods = _modulation(c_all, w_mod, b_mod)
    mods4 = mods.reshape(depth, c_rows, 1, 6 * d)

    win = gla_w_in[0].astype(BF16)
    wga = jnp.concatenate([gla_w_gate_a[0, 0], gla_w_gate_a[0, 1]], axis=1).astype(BF16)
    dk = kdim // GLA_HEADS
    by_head = lambda a: a.reshape(a.shape[0], GLA_HEADS, 1, dk)
    zero = jnp.zeros((GLA_GATE_RANK, GLA_HEADS, 1, dk), F32)
    wgb = jnp.concatenate([
        jnp.concatenate([by_head(gla_w_gate_b[0, 0]), zero], axis=2),
        jnp.concatenate([zero, by_head(gla_w_gate_b[0, 1])], axis=2)], axis=0)
    wgb = wgb.reshape(2 * GLA_GATE_RANK, 2 * kdim).astype(BF16)
    bg = jnp.stack([gla_b_gate[0, 0].reshape(GLA_HEADS, dk), gla_b_gate[0, 1].reshape(GLA_HEADS, dk)],
                   axis=1).reshape(1, 2 * kdim)
    n1_0 = row(norm1_g[0])
    q, k, v, r, g = _gla_in(x, mods4, 0, lambda b: b, n1_0, win, wga, wgb, bg, TM_PROJ)
    kc, vc, gc = _gla_in(ctx, mods4, 0, lambda b: bsz, n1_0, win, wga, wgb, bg, TM_CTX, state_only=True)
    s0 = _gla_state(kc, vc, gc, GLA_CHUNK, GLA_BLOCK)
    o = _gla_scan(q, k, v, g, s0, row(gla_norm_g[0]), GLA_CHUNK, GLA_BLOCK)

    wr_t = jnp.transpose(w_router)
    br = b_router.reshape(N_EXPERTS, 1)
    x1, payload, info, counts = _post0(o, r, x, mods4, 0, row(norm2_g[0]),
                                       gla_w_out[0].astype(BF16), wr_t, br, TM_POST)
    w_experts = _expert_weights(w_gate_e, w_up_e, w_down_e)
    f0 = _sparse_moe(payload, info, counts, w_experts, 0)

    x3, payload, info, counts = _layer1(x1, f0, mods4, row(norm1_g[1]), row(norm2_g[1]),
                                        pool_w[0].astype(BF16), row(pool_b[0]), row(pool_scale[0]),
                                        wr_t, br, TM_POOL)
    f1 = _sparse_moe(payload, info, counts, w_experts, 1)
    return _final(x3, f1, mods4, row(final_g), TM_FINAL)
```

```python
import functools

import numpy as np
import jax
import jax.numpy as jnp
from jax import lax
from jax.experimental import pallas as pl
from jax.experimental.pallas import tpu as pltpu

EPS = 1e-6
GRID_W = 64
GLA_HEADS = 4
GLA_GATE_RANK = 16
GLA_GATE_NORM = 16.0
GLA_CHUNK = 128
GLA_BLOCK = 256
GLA_BLOCKS_PER_TRIP = 8
GLA_FINISH_BLOCKS_PER_TRIP = 8
POOL_WINDOWS = (2, 4, 8, 16)
N_EXPERTS = 16
N_EXPERT_GROUPS = 4
EXPERTS_PER_GROUP = N_EXPERTS // N_EXPERT_GROUPS
TOP_K = 2

_PAIRS = tuple((a, b) for a in range(EXPERTS_PER_GROUP) for b in range(a + 1, EXPERTS_PER_GROUP))
_PAIR_BASE = (0, 3, 5)
N_CLASSES = N_EXPERT_GROUPS * len(_PAIRS)
CLASS_ROWS = 32
INFO_ROWS = 8
LANES = 128

TM_PROJ = 1024
TM_CTX = 256
TM_POST = 512
TM_POOL = 512
TM_MOE = 256
TM_FINAL = 512
DMA_UNROLL = 8

F32 = jnp.float32
BF16 = jnp.bfloat16

_NT = (((1,), (1,)), ((), ()))
_TN = (((0,), (0,)), ((), ()))
_VMEM_LIMIT = 56 * 1024 * 1024


def _cparams(sem):
    return pltpu.CompilerParams(dimension_semantics=sem, vmem_limit_bytes=_VMEM_LIMIT)


def _dot(a, b, dims=None):
    if dims is None:
        return jnp.dot(a, b, preferred_element_type=F32)
    return lax.dot_general(a, b, dims, preferred_element_type=F32)


def _split(a):
    hi = a.astype(BF16)
    lo = (a - hi.astype(F32)).astype(BF16)
    return hi, lo


def _dot3(a, b, dims=None):
    ah, al = _split(a)
    bh, bl = _split(b)
    return _dot(ah, bh, dims) + _dot(ah, bl, dims) + _dot(al, bh, dims)


def _sigmoid(x):
    return 1.0 / (1.0 + jnp.exp(-x))


def _silu(x):
    return x * _sigmoid(x)


def _rmsnorm(xf, g):
    return xf * lax.rsqrt(jnp.mean(xf * xf, axis=-1, keepdims=True) + EPS) * g


def _norm_modulate(xf, g, scale, shift):
    return xf * lax.rsqrt(jnp.mean(xf * xf, axis=-1, keepdims=True) + EPS) * (g * (1.0 + scale)) + shift


def _mod_kernel(c_ref, w_ref, b_ref, o_ref):
    o_ref[...] = _dot3(_silu(c_ref[...]), w_ref[...]) + b_ref[...]


def _modulation(c_all, w_mod, b_mod):
    depth, d, d6 = w_mod.shape
    rows = c_all.shape[0]
    tn = 1536
    return pl.pallas_call(
        _mod_kernel,
        out_shape=jax.ShapeDtypeStruct((depth, rows, d6), F32),
        grid=(depth, d6 // tn),
        in_specs=[
            pl.BlockSpec((rows, d), lambda l, j: (0, 0)),
            pl.BlockSpec((None, d, tn), lambda l, j: (l, 0, j)),
            pl.BlockSpec((None, 1, tn), lambda l, j: (l, 0, j)),
        ],
        out_specs=pl.BlockSpec((None, rows, tn), lambda l, j: (l, 0, j)),
        compiler_params=_cparams(("arbitrary", "arbitrary")),
        name="mod",
    )(c_all, w_mod, b_mod.reshape(depth, 1, d6))


def _mod_spec(layer, chunk, d, row_of_batch):
    return pl.BlockSpec((None, None, 1, d), lambda b, t, *_: (layer, row_of_batch(b), 0, chunk))


def _gla_in_kernel(x_ref, sh_ref, sc_ref, ng_ref, win_ref, wga_ref, wgb_ref, bg_ref, *out_refs,
                   kdim, vdim, qscale, state_only):
    hb = _norm_modulate(x_ref[...], ng_ref[...], sc_ref[...], sh_ref[...]).astype(BF16)
    if state_only:
        k_ref, v_ref, g_ref = out_refs
        proj = _dot(hb, win_ref[:, kdim:2 * kdim + vdim])
        k_ref[...] = proj[:, :kdim].astype(BF16)
        v_ref[...] = proj[:, kdim:].astype(BF16)
    else:
        q_ref, k_ref, v_ref, r_ref, g_ref = out_refs
        proj = _dot(hb, win_ref[...])
        q_ref[...] = (proj[:, :kdim] * qscale).astype(BF16)
        k_ref[...] = proj[:, kdim:2 * kdim].astype(BF16)
        v_ref[...] = proj[:, 2 * kdim:2 * kdim + vdim].astype(BF16)
        r_ref[...] = proj[:, 2 * kdim + vdim:].astype(BF16)
    low = _dot(hb, wga_ref[...])
    z = _dot(low.astype(BF16), wgb_ref[...]) + bg_ref[...]
    logsig = jnp.minimum(z, 0.0) - jnp.log(1.0 + jnp.exp(-jnp.abs(z)))
    g_ref[...] = logsig * (1.0 / GLA_GATE_NORM)


def _gla_in(x, mods4, layer, row_of_batch, ng, win, wga, wgb, bg, tm, state_only=False):
    bsz, t, d = x.shape
    kdim, vdim = d // 2, d
    tm = min(tm, t)
    tok = lambda w: pl.BlockSpec((None, tm, w), lambda b, i: (b, i, 0))
    full = lambda a: pl.BlockSpec(a.shape, lambda b, i: (0,) * a.ndim)
    kern = functools.partial(_gla_in_kernel, kdim=kdim, vdim=vdim,
                             qscale=float((kdim // GLA_HEADS) ** -0.5), state_only=state_only)
    widths = (kdim, vdim, 2 * kdim) if state_only else (kdim, kdim, vdim, vdim, 2 * kdim)
    dtypes = (BF16,) * (len(widths) - 1) + (F32,)
    return pl.pallas_call(
        kern,
        out_shape=tuple(jax.ShapeDtypeStruct((bsz, t, w), dt) for w, dt in zip(widths, dtypes)),
        grid=(bsz, t // tm),
        in_specs=[
            tok(d),
            _mod_spec(layer, 0, d, row_of_batch),
            _mod_spec(layer, 1, d, row_of_batch),
            full(ng), full(win), full(wga), full(wgb), full(bg),
        ],
        out_specs=tuple(tok(w) for w in widths),
        compiler_params=_cparams(("arbitrary", "arbitrary")),
        name="gla_in",
    )(x, mods4, mods4, ng, win, wga, wgb, bg)


def _block_masks(rows, chunk):
    i = np.arange(rows)
    same = (i[:, None] // chunk) == (i[None, :] // chunk)
    lower = same & (i[:, None] >= i[None, :])
    upper = same & (i[:, None] <= i[None, :])
    return jnp.asarray(lower, BF16), jnp.asarray(upper, BF16)


def _per_chunk_row(a, chunk, r):
    rows, w = a.shape
    parts = [jnp.broadcast_to(a[c * chunk + r:c * chunk + r + 1, :], (chunk, w))
             for c in range(rows // chunk)]
    return jnp.concatenate(parts, axis=0)


def _block_terms(q, k, v, g2, lo, up, chunk, want_out):
    dk = g2.shape[1] // 2
    gh, gl = _split(g2)
    pre = _dot(lo, gh) + _dot(lo, gl)
    tot = _per_chunk_row(pre, chunk, chunk - 1)
    fwd = lax.broadcasted_iota(jnp.int32, g2.shape, 1) < dk
    cum = jnp.where(fwd, pre, tot - pre + g2)
    kf = k.astype(F32)
    kf2 = jnp.concatenate([kf, kf], axis=1)
    kl = (kf2 * jnp.exp(tot - cum)).astype(BF16)
    etot = jnp.exp(tot)
    if not want_out:
        return kl, etot, None, None
    mid = _per_chunk_row(cum, chunk, chunk // 2)
    qf = q.astype(F32)
    qf2 = jnp.concatenate([qf, qf], axis=1)
    qe = (qf2 * jnp.exp(cum)).astype(BF16)
    qi = (qf2 * jnp.exp(cum - mid)).astype(BF16)
    ki = (kf2 * jnp.exp(mid - cum)).astype(BF16)
    s = (jnp.where(lo > 0, _dot(qi[:, :dk], ki[:, :dk], _NT), 0.0)
         + jnp.where(up > 0, _dot(qi[:, dk:], ki[:, dk:], _NT), 0.0))
    return kl, etot, qe, _dot(s.astype(BF16), v)


def _store_chunk_states(v, kl, etot, ds_ref, e_ref, first_chunk, chunk):
    for c in range(v.shape[0] // chunk):
        rows = slice(c * chunk, (c + 1) * chunk)
        ds_ref[first_chunk + c] = _dot(v[rows, :], kl[rows, :], _TN)
        e_ref[first_chunk + c] = etot[c * chunk:c * chunk + 8, :]


def _scan_states(s0, ds_ref, e_ref, sc_ref, n, dk):
    def body(c, carry):
        sf, sb = carry
        r = n - 1 - c
        if sc_ref is not None:
            sc_ref[c, :, :dk] = sf.astype(BF16)
            sc_ref[r, :, dk:] = sb.astype(BF16)
        return (sf * e_ref[c, 0:1, :dk] + ds_ref[c, :, :dk],
                sb * e_ref[r, 0:1, dk:] + ds_ref[r, :, dk:])

    return lax.fori_loop(0, n, body, (s0[:, :dk], s0[:, dk:]))


def _gla_state_kernel(k_ref, v_ref, g_ref, lo_ref, up_ref, s_ref, ds_ref, e_ref, *, chunk, block):
    t = k_ref.shape[0]
    dk = k_ref.shape[1]
    per = block // chunk

    def terms(i, carry):
        rows = pl.ds(pl.multiple_of(i * block, block), block)
        v = v_ref[rows, :]
        kl, etot, _, _ = _block_terms(None, k_ref[rows, :], v, g_ref[rows, :], lo_ref[...],
                                      up_ref[...], chunk, False)
        _store_chunk_states(v, kl, etot, ds_ref, e_ref, i * per, chunk)
        return carry

    lax.fori_loop(0, t // block, terms, 0)
    sf, sb = _scan_states(jnp.zeros(s_ref.shape, F32), ds_ref, e_ref, None, t // chunk, dk)
    s_ref[:, :dk] = sf
    s_ref[:, dk:] = sb


def _gla_scan_kernel(q_ref, k_ref, v_ref, g_ref, s0_ref, ng_ref, lo_ref, up_ref,
                     o_ref, oi_ref, qe_ref, ds_ref, e_ref, sc_ref, *, chunk, block):
    t = q_ref.shape[0]
    dk = q_ref.shape[1]
    per = block // chunk

    group = GLA_BLOCKS_PER_TRIP if (t // block) % GLA_BLOCKS_PER_TRIP == 0 else 1

    def terms(i, carry):
        for u in range(group):
            blk = i * group + u
            rows = pl.ds(pl.multiple_of(blk * block, block), block)
            v = v_ref[rows, :]
            kl, etot, qe, oi = _block_terms(q_ref[rows, :], k_ref[rows, :], v, g_ref[rows, :],
                                            lo_ref[...], up_ref[...], chunk, True)
            _store_chunk_states(v, kl, etot, ds_ref, e_ref, blk * per, chunk)
            qe_ref[rows, :] = qe
            oi_ref[rows, :] = oi
        return carry

    lax.fori_loop(0, t // (block * group), terms, 0)
    _scan_states(s0_ref[...], ds_ref, e_ref, sc_ref, t // chunk, dk)

    fgroup = GLA_FINISH_BLOCKS_PER_TRIP if (t // block) % GLA_FINISH_BLOCKS_PER_TRIP == 0 else 1

    def finish(i, carry):
        for u in range(fgroup):
            blk = i * fgroup + u
            rows = pl.ds(pl.multiple_of(blk * block, block), block)
            inter = [_dot(qe_ref[pl.ds(pl.multiple_of(blk * block + c * chunk, chunk), chunk), :],
                          sc_ref[blk * per + c], _NT) for c in range(per)]
            o = oi_ref[rows, :] + jnp.concatenate(inter, axis=0)
            o_ref[rows, :] = _rmsnorm(o, ng_ref[...]).astype(o_ref.dtype)
        return carry

    lax.fori_loop(0, t // (block * fgroup), finish, 0)


def _gla_state(k, v, g, chunk, block):
    bsz, t, kdim = k.shape
    vdim = v.shape[-1]
    nh = GLA_HEADS
    dk, dv = kdim // nh, vdim // nh
    block = min(block, t)
    n = t // chunk
    lo, up = _block_masks(block, chunk)
    mask_spec = pl.BlockSpec((block, block), lambda b, h: (0, 0))
    return pl.pallas_call(
        functools.partial(_gla_state_kernel, chunk=chunk, block=block),
        out_shape=jax.ShapeDtypeStruct((bsz, nh, dv, 2 * dk), F32),
        grid=(bsz, nh),
        in_specs=[
            pl.BlockSpec((None, t, dk), lambda b, h: (b, 0, h)),
            pl.BlockSpec((None, t, dv), lambda b, h: (b, 0, h)),
            pl.BlockSpec((None, t, 2 * dk), lambda b, h: (b, 0, h)),
            mask_spec, mask_spec,
        ],
        out_specs=pl.BlockSpec((None, None, dv, 2 * dk), lambda b, h: (b, h, 0, 0)),
        scratch_shapes=[pltpu.VMEM((n, dv, 2 * dk), F32), pltpu.VMEM((n, 8, 2 * dk), F32)],
        compiler_params=_cparams(("arbitrary", "arbitrary")),
        name="gla_state",
    )(k, v, g, lo, up)


def _gla_scan(q, k, v, g, s0, norm_g, chunk, block):
    bsz, t, kdim = k.shape
    vdim = v.shape[-1]
    nh = GLA_HEADS
    dk, dv = kdim // nh, vdim // nh
    block = min(block, t)
    n = t // chunk
    lo, up = _block_masks(block, chunk)
    kspec = pl.BlockSpec((None, t, dk), lambda b, h: (b, 0, h))
    vspec = pl.BlockSpec((None, t, dv), lambda b, h: (b, 0, h))
    mask_spec = pl.BlockSpec((block, block), lambda b, h: (0, 0))
    return pl.pallas_call(
        functools.partial(_gla_scan_kernel, chunk=chunk, block=block),
        out_shape=jax.ShapeDtypeStruct((bsz, t, vdim), BF16),
        grid=(bsz, nh),
        in_specs=[
            kspec, kspec, vspec,
            pl.BlockSpec((None, t, 2 * dk), lambda b, h: (b, 0, h)),
            pl.BlockSpec((None, None, dv, 2 * dk), lambda b, h: (b, h, 0, 0)),
            pl.BlockSpec((1, dv), lambda b, h: (0, 0)),
            mask_spec, mask_spec,
        ],
        out_specs=vspec,
        scratch_shapes=[
            pltpu.VMEM((t, dv), F32),
            pltpu.VMEM((t, 2 * dk), BF16),
            pltpu.VMEM((n, dv, 2 * dk), F32),
            pltpu.VMEM((n, 8, 2 * dk), F32),
            pltpu.VMEM((n, dv, 2 * dk), BF16),
        ],
        compiler_params=_cparams(("arbitrary", "arbitrary")),
        name="gla_scan",
    )(q, k, v, g, s0, norm_g, lo, up)


def _route(h2, wr_t, br, earlier_bf, count_ref):
    tm = h2.shape[0]
    logits = _dot3(wr_t, h2, _NT)
    scores = _sigmoid(logits)
    sel = scores + br
    row = [sel[e:e + 1, :] for e in range(N_EXPERTS)]
    picked = []
    for e in range(N_EXPERTS):
        g0 = (e // EXPERTS_PER_GROUP) * EXPERTS_PER_GROUP
        ahead_count = jnp.zeros(row[e].shape, jnp.int32)
        for j in range(g0, g0 + EXPERTS_PER_GROUP):
            if j == e:
                continue
            ahead = (row[j] >= row[e]) if j < e else (row[j] > row[e])
            ahead_count = ahead_count + ahead.astype(jnp.int32)
        picked.append(ahead_count < TOP_K)
    zero = jnp.zeros_like(row[0])
    gscore, pair, first_w, second_w = [], [], [], []
    for g in range(N_EXPERT_GROUPS):
        acc, pidx, fw, sw = zero, zero, zero, zero
        seen = None
        for a in range(EXPERTS_PER_GROUP):
            e = g * EXPERTS_PER_GROUP + a
            sc = scores[e:e + 1, :]
            acc = acc + jnp.where(picked[e], row[e], 0.0)
            if seen is None:
                is_first = picked[e]
            else:
                is_first = picked[e] & jnp.logical_not(seen)
                is_second = picked[e] & seen
                sw = sw + jnp.where(is_second, sc, 0.0)
                pidx = pidx + jnp.where(is_second, float(a), 0.0)
            fw = fw + jnp.where(is_first, sc, 0.0)
            if a < len(_PAIR_BASE):
                pidx = pidx + jnp.where(is_first, float(_PAIR_BASE[a] - a - 1), 0.0)
            seen = picked[e] if seen is None else (seen | picked[e])
        gscore.append(acc)
        pair.append(pidx)
        first_w.append(fw)
        second_w.append(sw)
    cls, wa, wb = zero, zero, zero
    for g in range(N_EXPERT_GROUPS):
        ok = None
        for j in range(N_EXPERT_GROUPS):
            if j == g:
                continue
            c = (gscore[g] > gscore[j]) if j < g else (gscore[g] >= gscore[j])
            ok = c if ok is None else (ok & c)
        cls = cls + jnp.where(ok, pair[g] + float(len(_PAIRS) * g), 0.0)
        wa = wa + jnp.where(ok, first_w[g], 0.0)
        wb = wb + jnp.where(ok, second_w[g], 0.0)
    denom = wa + wb
    wa = wa / denom
    wb = wb / denom

    cid = lax.broadcasted_iota(jnp.int32, (CLASS_ROWS, tm), 0).astype(F32)
    onehot = (cid == cls).astype(BF16)
    before = _dot(onehot, earlier_bf)
    oh = onehot.astype(F32)
    base = count_ref[...][:, 0:1]
    rank = jnp.sum(oh * (before + base), axis=0, keepdims=True)
    count_ref[...] = count_ref[...] + jnp.sum(oh, axis=1, keepdims=True)
    pad = jnp.zeros((INFO_ROWS - 4, tm), F32)
    return jnp.concatenate([cls, rank, wa, wb, pad], axis=0)


def _earlier_matrix(tm):
    i = np.arange(tm)
    return jnp.asarray(i[:, None] < i[None, :], BF16)


def _store_tiled_rows(ref, x):
    tm, d = x.shape
    per = d // LANES
    for c in range(per):
        ref[pl.ds(c, tm, stride=per), :] = x[:, c * LANES:(c + 1) * LANES]


def _load_tiled_rows(ref, tm):
    per = ref.shape[0] // tm
    return jnp.concatenate([ref[pl.ds(c, tm, stride=per), :] for c in range(per)], axis=1)


def _route_outputs(h2, wr_ref, br_ref, earlier_ref, count_ref, pay_ref, info_ref, counts_ref):
    first = (pl.program_id(0) == 0) & (pl.program_id(1) == 0)

    @pl.when(first)
    def _():
        count_ref[...] = jnp.zeros_like(count_ref)

    info = _route(h2, wr_ref[...], br_ref[...], earlier_ref[...], count_ref)
    _store_tiled_rows(pay_ref, h2)
    info_ref[...] = info
    counts_ref[...] = count_ref[...]


def _route_out_shapes(bsz, t, d):
    return (
        jax.ShapeDtypeStruct((bsz * t * (d // LANES), LANES), F32),
        jax.ShapeDtypeStruct((INFO_ROWS, bsz * t), F32),
        jax.ShapeDtypeStruct((CLASS_ROWS, 128), F32),
    )


def _route_out_specs(tm, nt, d):
    return (
        pl.BlockSpec((tm * (d // LANES), LANES), lambda b, i, *_: (b * nt + i, 0)),
        pl.BlockSpec((INFO_ROWS, tm), lambda b, i, *_: (0, b * nt + i)),
        pl.BlockSpec((CLASS_ROWS, 128), lambda b, i, *_: (0, 0)),
    )


def _plan(info, counts, n, tmoe):
    cls = info[0].astype(jnp.int32)
    rank = info[1].astype(jnp.int32)
    cnt = counts[:N_CLASSES, 0].astype(jnp.int32)
    padded = (cnt + tmoe - 1) // tmoe * tmoe
    ends = jnp.cumsum(padded)
    dest = (ends - padded)[cls] + rank
    n_tiles = n // tmoe + N_CLASSES
    rows = jnp.stack([jnp.arange(n, dtype=F32), info[2], info[3], jnp.zeros((n,), F32)], axis=1)
    srt = jnp.zeros((n_tiles * tmoe, 4), F32).at[dest].set(rows, unique_indices=True)
    tok = srt[:, 0].astype(jnp.int32)
    n_used = ends[-1] // tmoe
    tile = jnp.minimum(jnp.arange(n_tiles, dtype=jnp.int32), n_used - 1)
    tcls = jnp.sum((ends[None, :] <= (tile * tmoe)[:, None]).astype(jnp.int32), axis=1)
    group, pair = tcls // len(_PAIRS), tcls % len(_PAIRS)
    pa = jnp.asarray([p[0] for p in _PAIRS], jnp.int32)[pair]
    pb = jnp.asarray([p[1] for p in _PAIRS], jnp.int32)[pair]
    ea = group * EXPERTS_PER_GROUP + pa
    eb = group * EXPERTS_PER_GROUP + pb
    return dest, tok, srt, ea, eb, n_used.reshape(1).astype(jnp.int32)


def _post0_kernel(o_ref, r_ref, x_ref, g1_ref, sh2_ref, sc2_ref, n2_ref, wout_ref, wr_ref, br_ref,
                  earlier_ref, x1_ref, pay_ref, info_ref, counts_ref, count_ref):
    a = (o_ref[...].astype(F32) * _silu(r_ref[...].astype(F32))).astype(BF16)
    x1 = x_ref[...] + g1_ref[...] * _dot(a, wout_ref[...])
    x1_ref[...] = x1
    h2 = _norm_modulate(x1, n2_ref[...], sc2_ref[...], sh2_ref[...])
    _route_outputs(h2, wr_ref, br_ref, earlier_ref, count_ref, pay_ref, info_ref, counts_ref)


def _post0(o, r, x, mods4, layer, n2, wout, wr_t, br, tm):
    bsz, t, d = x.shape
    tm = min(tm, t)
    rb = lambda b: b
    tok = lambda w: pl.BlockSpec((None, tm, w), lambda b, i: (b, i, 0))
    full = lambda a: pl.BlockSpec(a.shape, lambda b, i: (0,) * a.ndim)
    nt = t // tm
    earlier = _earlier_matrix(tm)
    return pl.pallas_call(
        _post0_kernel,
        out_shape=(jax.ShapeDtypeStruct((bsz, t, d), F32),) + _route_out_shapes(bsz, t, d),
        grid=(bsz, nt),
        in_specs=[
            tok(d), tok(d), tok(d),
            _mod_spec(layer, 2, d, rb), _mod_spec(layer, 3, d, rb), _mod_spec(layer, 4, d, rb),
            full(n2), full(wout), full(wr_t), full(br), full(earlier),
        ],
        out_specs=(tok(d),) + _route_out_specs(tm, nt, d),
        scratch_shapes=[pltpu.VMEM((CLASS_ROWS, 128), F32)],
        compiler_params=_cparams(("arbitrary", "arbitrary")),
        name="post0",
    )(o, r, x, mods4, mods4, mods4, n2, wout, wr_t, br, earlier)


def _row_group(i, per):
    return pl.ds(pl.multiple_of(i * per, per), per)


def _moe_kernel(tok_ref, ea_ref, eb_ref, used_ref, p_hbm, gw_ref, wa_ref, wb_ref, f_ref,
                pbuf0, pbuf1, gsem, *, d, tm):
    del ea_ref, eb_ref
    j = pl.program_id(0)
    n_tiles = pl.num_programs(0)
    used = used_ref[0]
    per = d // LANES
    pbufs = (pbuf0, pbuf1)

    def gather_row(tile, r, p, pred, lane):
        i = tok_ref[tile * tm + r]

        @pl.when(pred)
        def _():
            pltpu.make_async_copy(p_hbm.at[_row_group(i, per), :], pbufs[p].at[_row_group(r, per), :],
                                  gsem.at[p]).start(priority=lane)

    def gather_wait(p):
        pltpu.make_async_copy(p_hbm.at[pl.ds(0, tm * per), :], pbufs[p], gsem.at[p]).wait()

    @pl.when(j == 0)
    def _():
        def body(blk, carry):
            for u in range(DMA_UNROLL):
                gather_row(0, blk * DMA_UNROLL + u, 0, True, u % 2)
            return carry
        lax.fori_loop(0, tm // DMA_UNROLL, body, 0)

    def step(p, beside_matmuls):
        @pl.when(j < used)
        def _():
            has_next = j + 1 < used
            nxt = jnp.minimum(j + 1, n_tiles - 1)
            if beside_matmuls:
                gather_wait(p)
                for r in range(tm):
                    gather_row(nxt, r, 1 - p, has_next, r % 2)
            else:
                @pl.when(j >= 0)
                def _():
                    for r in range(tm):
                        gather_row(nxt, r, 1 - p, has_next, r % 2)
                gather_wait(p)
            h = _load_tiled_rows(pbufs[p], tm).astype(BF16)
            gw = gw_ref[...]

            def expert(w_ref):
                de = w_ref.shape[2]
                he = (_silu(_dot(h, w_ref[0])) * _dot(h, w_ref[1])).astype(BF16)
                return jnp.concatenate([_dot(he, w_ref[2, :de, :]), _dot(he, w_ref[2, de:, :])], axis=1)

            _store_tiled_rows(f_ref, gw[:, 1:2] * expert(wa_ref) + gw[:, 2:3] * expert(wb_ref))

    @pl.when(j % 2 == 0)
    def _():
        step(0, True)

    @pl.when(j % 2 == 1)
    def _():
        step(1, False)

    @pl.when(j >= used)
    def _():
        f_ref[...] = jnp.zeros(f_ref.shape, F32)


def _moe(payload, tok, srt, ea, eb, n_used, w_all, layer, tm):
    n_sorted = tok.shape[0]
    n_tiles = n_sorted // tm
    _, _, _, d, de = w_all.shape
    per = d // LANES
    wa = lambda j, tok, ea, eb, used: (layer, ea[j], 0, 0, 0)
    wb = lambda j, tok, ea, eb, used: (layer, eb[j], 0, 0, 0)
    return pl.pallas_call(
        functools.partial(_moe_kernel, d=d, tm=tm),
        out_shape=jax.ShapeDtypeStruct((n_sorted * per, LANES), F32),
        grid_spec=pltpu.PrefetchScalarGridSpec(
            num_scalar_prefetch=4,
            grid=(n_tiles,),
            in_specs=[
                pl.BlockSpec(memory_space=pl.ANY),
                pl.BlockSpec((tm, srt.shape[1]), lambda j, *_: (j, 0)),
                pl.BlockSpec((None, None, 3, d, de), wa), pl.BlockSpec((None, None, 3, d, de), wb),
            ],
            out_specs=pl.BlockSpec((tm * per, LANES), lambda j, *_: (j, 0)),
            scratch_shapes=[pltpu.VMEM((tm * per, LANES), F32), pltpu.VMEM((tm * per, LANES), F32),
                            pltpu.SemaphoreType.DMA((2,))],
        ),
        compiler_params=_cparams(("arbitrary",)),
        name="moe",
    )(tok, ea, eb, n_used, payload, srt, w_all, w_all)


def _expert_weights_kernel(w1_ref, w3_ref, w2_ref, o_ref):
    de = w2_ref.shape[0]
    o_ref[0] = w1_ref[...].astype(BF16)
    o_ref[1] = w3_ref[...].astype(BF16)
    o_ref[2, :de, :] = w2_ref[:, :de].astype(BF16)
    o_ref[2, de:, :] = w2_ref[:, de:].astype(BF16)


def _expert_weights(w1, w3, w2):
    nl, ne, d, de = w1.shape
    assert d == 2 * de
    up = pl.BlockSpec((None, None, d, de), lambda l, e: (l, e, 0, 0))
    return pl.pallas_call(
        _expert_weights_kernel,
        out_shape=jax.ShapeDtypeStruct((nl, ne, 3, d, de), BF16),
        grid=(nl, ne),
        in_specs=[up, up, pl.BlockSpec((None, None, de, d), lambda l, e: (l, e, 0, 0))],
        out_specs=pl.BlockSpec((None, None, 3, d, de), lambda l, e: (l, e, 0, 0, 0)),
        compiler_params=_cparams(("arbitrary", "arbitrary")),
        name="expert_weights",
    )(w1, w3, w2)


def _sparse_moe(payload, info, counts, w_all, layer):
    n = info.shape[1]
    tmoe = min(TM_MOE, n)
    dest, tok, srt, ea, eb, n_used = _plan(info, counts, n, tmoe)
    return _moe(payload, tok, srt, ea, eb, n_used, w_all, layer, tmoe), dest


def _gather_rows_start(dest_ref, f_hbm, fbuf, fsem, tile, tm, per, pred, unrolled):
    def one(r, lane):
        i = dest_ref[tile * tm + r]

        @pl.when(pred)
        def _():
            pltpu.make_async_copy(f_hbm.at[_row_group(i, per), :], fbuf.at[_row_group(r, per), :],
                                  fsem).start(priority=lane)

    if unrolled:
        for r in range(tm):
            one(r, r % 2)
    else:
        def body(blk, carry):
            for u in range(DMA_UNROLL):
                one(blk * DMA_UNROLL + u, u % 2)
            return carry
        lax.fori_loop(0, tm // DMA_UNROLL, body, 0)


def _gather_rows_wait(f_hbm, fbuf, fsem):
    pltpu.make_async_copy(f_hbm.at[pl.ds(0, fbuf.shape[0]), :], fbuf, fsem).wait()


def _pool_constants(tm, d):
    ng = len(POOL_WINDOWS)
    cg = d // ng
    pos = np.arange(tm)
    seg, off = pos // GRID_W, pos % GRID_W
    mats = np.zeros((ng, tm, tm), np.float32)
    inv = np.zeros((tm, d), np.float32)
    for gi, w in enumerate(POOL_WINDOWS):
        lo = np.clip(off - w // 2, 0, GRID_W)
        hi = np.clip(off - w // 2 + w, 0, GRID_W)
        same = seg[:, None] == seg[None, :]
        inside = (off[None, :] >= lo[:, None]) & (off[None, :] < hi[:, None])
        mats[gi] = (same & inside).astype(np.float32)
        inv[:, gi * cg:(gi + 1) * cg] = (1.0 / (hi - lo).astype(np.float32))[:, None]
    return jnp.asarray(mats, BF16), jnp.asarray(inv, F32)


def _layer1_kernel(dest_ref, x_ref, f_hbm, g2p_ref, sh1_ref, sc1_ref, g1_ref, sh2_ref, sc2_ref,
                   n1_ref, n2_ref, pm_ref, inv_ref, wp_ref, bp_ref, ps_ref, wr_ref, br_ref, earlier_ref,
                   x3_ref, pay_ref, info_ref, counts_ref, fbuf, fsem, x2_ref, count_ref):
    tm, d = x_ref.shape
    per = d // LANES
    nt = pl.num_programs(1)
    lin = pl.program_id(0) * nt + pl.program_id(1)

    @pl.when(lin == 0)
    def _():
        _gather_rows_start(dest_ref, f_hbm, fbuf, fsem, 0, tm, per, True, False)

    @pl.when(lin >= 0)
    def _():
        _gather_rows_wait(f_hbm, fbuf, fsem)
        x2_ref[...] = x_ref[...] + g2p_ref[...] * _load_tiled_rows(fbuf, tm)

    total = pl.num_programs(0) * nt
    _gather_rows_start(dest_ref, f_hbm, fbuf, fsem, jnp.minimum(lin + 1, total - 1), tm, per,
                       lin + 1 < total, True)
    x2 = x2_ref[...]
    h = _norm_modulate(x2, n1_ref[...], sc1_ref[...], sh1_ref[...])
    hb = h.astype(BF16)
    ng = pm_ref.shape[0]
    cg = h.shape[1] // ng
    ys = []
    for gi in range(ng):
        cs = slice(gi * cg, (gi + 1) * cg)
        wsum = _dot(pm_ref[gi], hb[:, cs])
        pooled = wsum * inv_ref[:, cs] - h[:, cs]
        ys.append(_dot(pooled.astype(BF16), wp_ref[gi]))
    y = (jnp.concatenate(ys, axis=1) + bp_ref[...]) * ps_ref[...]
    x3 = x2 + g1_ref[...] * y
    x3_ref[...] = x3
    h2 = _norm_modulate(x3, n2_ref[...], sc2_ref[...], sh2_ref[...])
    _route_outputs(h2, wr_ref, br_ref, earlier_ref, count_ref, pay_ref, info_ref, counts_ref)


def _layer1(x1, f_sorted, dest, mods4, n1, n2, wp, bp, ps, wr_t, br, tm):
    bsz, t, d = x1.shape
    tm = min(tm, t)
    pm, inv = _pool_constants(tm, d)
    earlier = _earlier_matrix(tm)
    rb = lambda b: b
    tok = lambda w: pl.BlockSpec((None, tm, w), lambda b, i, dest: (b, i, 0))
    full = lambda a: pl.BlockSpec(a.shape, lambda b, i, dest: (0,) * a.ndim)
    nt = t // tm
    return pl.pallas_call(
        _layer1_kernel,
        out_shape=(jax.ShapeDtypeStruct((bsz, t, d), F32),) + _route_out_shapes(bsz, t, d),
        grid_spec=pltpu.PrefetchScalarGridSpec(
            num_scalar_prefetch=1,
            grid=(bsz, nt),
            in_specs=[
                tok(d), pl.BlockSpec(memory_space=pl.ANY),
                _mod_spec(0, 5, d, rb),
                _mod_spec(1, 0, d, rb), _mod_spec(1, 1, d, rb), _mod_spec(1, 2, d, rb),
                _mod_spec(1, 3, d, rb), _mod_spec(1, 4, d, rb),
                full(n1), full(n2), full(pm), full(inv), full(wp), full(bp), full(ps),
                full(wr_t), full(br), full(earlier),
            ],
            out_specs=(tok(d),) + _route_out_specs(tm, nt, d),
            scratch_shapes=[pltpu.VMEM((tm * (d // LANES), LANES), F32), pltpu.SemaphoreType.DMA(()),
                            pltpu.VMEM((tm, d), F32), pltpu.VMEM((CLASS_ROWS, 128), F32)],
        ),
        compiler_params=_cparams(("arbitrary", "arbitrary")),
        name="layer1",
    )(dest, x1, f_sorted, mods4, mods4, mods4, mods4, mods4, mods4, n1, n2, pm, inv, wp, bp, ps,
      wr_t, br, earlier)


def _final_kernel(dest_ref, x_ref, f_hbm, g2_ref, fg_ref, o_ref, fbuf, fsem):
    tm, d = x_ref.shape
    per = d // LANES
    nt = pl.num_programs(1)
    lin = pl.program_id(0) * nt + pl.program_id(1)

    @pl.when(lin == 0)
    def _():
        _gather_rows_start(dest_ref, f_hbm, fbuf, fsem, 0, tm, per, True, False)

    @pl.when(lin >= 0)
    def _():
        _gather_rows_wait(f_hbm, fbuf, fsem)
        f = _load_tiled_rows(fbuf, tm)
        o_ref[...] = _rmsnorm(x_ref[...] + g2_ref[...] * f, fg_ref[...])

    total = pl.num_programs(0) * nt
    _gather_rows_start(dest_ref, f_hbm, fbuf, fsem, jnp.minimum(lin + 1, total - 1), tm, per,
                       lin + 1 < total, False)


def _final(x3, f_sorted, dest, mods4, fg, tm):
    bsz, t, d = x3.shape
    tm = min(tm, t)
    tok = pl.BlockSpec((None, tm, d), lambda b, i, dest: (b, i, 0))
    return pl.pallas_call(
        _final_kernel,
        out_shape=jax.ShapeDtypeStruct((bsz, t, d), F32),
        grid_spec=pltpu.PrefetchScalarGridSpec(
            num_scalar_prefetch=1,
            grid=(bsz, t // tm),
            in_specs=[tok, pl.BlockSpec(memory_space=pl.ANY), _mod_spec(1, 5, d, lambda b: b),
                      pl.BlockSpec((1, d), lambda b, i, dest: (0, 0))],
            out_specs=tok,
            scratch_shapes=[pltpu.VMEM((tm * (d // LANES), LANES), F32), pltpu.SemaphoreType.DMA(())],
        ),
        compiler_params=_cparams(("arbitrary", "arbitrary")),
        name="final",
    )(dest, x3, f_sorted, mods4, fg)


def kernel(x, c, ctx, c_ctx, norm1_g, norm2_g, w_mod, b_mod, gla_w_in, gla_w_gate_a, gla_w_gate_b,
           gla_b_gate, gla_norm_g, gla_w_out, pool_w, pool_b, pool_scale, w_router, b_router,
           w_gate_e, w_up_e, w_down_e, final_g):
    bsz, t, d = x.shape
    depth = w_mod.shape[0]
    assert depth == 2 and t % GRID_W == 0
    kdim = d // 2
    row = lambda a: a.reshape(1, -1)

    c_rows = -(-(bsz + 1) // 8) * 8
    c_all = jnp.zeros((c_rows, d), F32).at[:bsz].set(c).at[bsz].set(c_ctx)
    mods = _modulation(c_all, w_mod, b_mod)
    mods4 = mods.reshape(depth, c_rows, 1, 6 * d)

    win = gla_w_in[0].astype(BF16)
    wga = jnp.concatenate([gla_w_gate_a[0, 0], gla_w_gate_a[0, 1]], axis=1).astype(BF16)
    dk = kdim // GLA_HEADS
    by_head = lambda a: a.reshape(a.shape[0], GLA_HEADS, 1, dk)
    zero = jnp.zeros((GLA_GATE_RANK, GLA_HEADS, 1, dk), F32)
    wgb = jnp.concatenate([
        jnp.concatenate([by_head(gla_w_gate_b[0, 0]), zero], axis=2),
        jnp.concatenate([zero, by_head(gla_w_gate_b[0, 1])], axis=2)], axis=0)
    wgb = wgb.reshape(2 * GLA_GATE_RANK, 2 * kdim).astype(BF16)
    bg = jnp.stack([gla_b_gate[0, 0].reshape(GLA_HEADS, dk), gla_b_gate[0, 1].reshape(GLA_HEADS, dk)],
                   axis=1).reshape(1, 2 * kdim)
    n1_0 = row(norm1_g[0])
    q, k, v, r, g = _gla_in(x, mods4, 0, lambda b: b, n1_0, win, wga, wgb, bg, TM_PROJ)
    kc, vc, gc = _gla_in(ctx, mods4, 0, lambda b: bsz, n1_0, win, wga, wgb, bg, TM_CTX, state_only=True)
    s0 = _gla_state(kc, vc, gc, GLA_CHUNK, GLA_BLOCK)
    o = _gla_scan(q, k, v, g, s0, row(gla_norm_g[0]), GLA_CHUNK, GLA_BLOCK)

    wr_t = jnp.transpose(w_router)
    br = b_router.reshape(N_EXPERTS, 1)
    x1, payload, info, counts = _post0(o, r, x, mods4, 0, row(norm2_g[0]),
                                       gla_w_out[0].astype(BF16), wr_t, br, TM_POST)
    w_experts = _expert_weights(w_gate_e, w_up_e, w_down_e)
    f0, dest0 = _sparse_moe(payload, info, counts, w_experts, 0)

    x3, payload, info, counts = _layer1(x1, f0, dest0, mods4, row(norm1_g[1]), row(norm2_g[1]),
                                        pool_w[0].astype(BF16), row(pool_b[0]), row(pool_scale[0]),
                                        wr_t, br, TM_POOL)
    f1, dest1 = _sparse_moe(payload, info, counts, w_experts, 1)
    return _final(x3, f1, dest1, mods4, row(final_g), TM_FINAL)
```

```python
import functools

import numpy as np
import jax
import jax.numpy as jnp
from jax import lax
from jax.experimental import pallas as pl
from jax.experimental.pallas import tpu as pltpu

EPS = 1e-6
GRID_W = 64
GLA_HEADS = 4
GLA_GATE_RANK = 16
GLA_GATE_NORM = 16.0
GLA_CHUNK = 128
GLA_BLOCK = 256
GLA_BLOCKS_PER_TRIP = 8
GLA_FINISH_BLOCKS_PER_TRIP = 8
POOL_WINDOWS = (2, 4, 8, 16)
N_EXPERTS = 16
N_EXPERT_GROUPS = 4
EXPERTS_PER_GROUP = N_EXPERTS // N_EXPERT_GROUPS
TOP_K = 2

_PAIRS = tuple((a, b) for a in range(EXPERTS_PER_GROUP) for b in range(a + 1, EXPERTS_PER_GROUP))
_PAIR_BASE = (0, 3, 5)
N_CLASSES = N_EXPERT_GROUPS * len(_PAIRS)
CLASS_ROWS = 32
INFO_ROWS = 8
LANES = 128

TM_PROJ = 1024
TM_CTX = 256
TM_POST = 1024
TM_POOL = 512
TM_MOE = 256
TM_FINAL = 512
DMA_UNROLL = 8

F32 = jnp.float32
BF16 = jnp.bfloat16

_NT = (((1,), (1,)), ((), ()))
_TN = (((0,), (0,)), ((), ()))
_VMEM_LIMIT = 56 * 1024 * 1024


def _cparams(sem):
    return pltpu.CompilerParams(dimension_semantics=sem, vmem_limit_bytes=_VMEM_LIMIT)


def _dot(a, b, dims=None):
    if dims is None:
        return jnp.dot(a, b, preferred_element_type=F32)
    return lax.dot_general(a, b, dims, preferred_element_type=F32)


def _split(a):
    hi = a.astype(BF16)
    lo = (a - hi.astype(F32)).astype(BF16)
    return hi, lo


def _dot3(a, b, dims=None):
    ah, al = _split(a)
    bh, bl = _split(b)
    return _dot(ah, bh, dims) + _dot(ah, bl, dims) + _dot(al, bh, dims)


def _sigmoid(x):
    return 1.0 / (1.0 + jnp.exp(-x))


def _silu(x):
    return x * _sigmoid(x)


def _rmsnorm(xf, g):
    return xf * lax.rsqrt(jnp.mean(xf * xf, axis=-1, keepdims=True) + EPS) * g


def _norm_modulate(xf, g, scale, shift):
    return xf * lax.rsqrt(jnp.mean(xf * xf, axis=-1, keepdims=True) + EPS) * (g * (1.0 + scale)) + shift


def _mod_kernel(c_ref, w_ref, b_ref, o_ref):
    o_ref[...] = _dot3(_silu(c_ref[...]), w_ref[...]) + b_ref[...]


def _modulation(c_all, w_mod, b_mod):
    depth, d, d6 = w_mod.shape
    rows = c_all.shape[0]
    tn = 1536
    return pl.pallas_call(
        _mod_kernel,
        out_shape=jax.ShapeDtypeStruct((depth, rows, d6), F32),
        grid=(depth, d6 // tn),
        in_specs=[
            pl.BlockSpec((rows, d), lambda l, j: (0, 0)),
            pl.BlockSpec((None, d, tn), lambda l, j: (l, 0, j)),
            pl.BlockSpec((None, 1, tn), lambda l, j: (l, 0, j)),
        ],
        out_specs=pl.BlockSpec((None, rows, tn), lambda l, j: (l, 0, j)),
        compiler_params=_cparams(("arbitrary", "arbitrary")),
        name="mod",
    )(c_all, w_mod, b_mod.reshape(depth, 1, d6))


def _mod_spec(layer, chunk, d, row_of_batch):
    return pl.BlockSpec((None, None, 1, d), lambda b, t, *_: (layer, row_of_batch(b), 0, chunk))


def _gla_in_kernel(x_ref, sh_ref, sc_ref, ng_ref, win_ref, wga_ref, wgb_ref, bg_ref, *out_refs,
                   kdim, vdim, qscale, state_only):
    hb = _norm_modulate(x_ref[...], ng_ref[...], sc_ref[...], sh_ref[...]).astype(BF16)
    if state_only:
        k_ref, v_ref, g_ref = out_refs
        proj = _dot(hb, win_ref[:, kdim:2 * kdim + vdim])
        k_ref[...] = proj[:, :kdim].astype(BF16)
        v_ref[...] = proj[:, kdim:].astype(BF16)
    else:
        q_ref, k_ref, v_ref, r_ref, g_ref = out_refs
        proj = _dot(hb, win_ref[...])
        q_ref[...] = (proj[:, :kdim] * qscale).astype(BF16)
        k_ref[...] = proj[:, kdim:2 * kdim].astype(BF16)
        v_ref[...] = proj[:, 2 * kdim:2 * kdim + vdim].astype(BF16)
        r_ref[...] = proj[:, 2 * kdim + vdim:].astype(BF16)
    low = _dot(hb, wga_ref[...])
    z = _dot(low.astype(BF16), wgb_ref[...]) + bg_ref[...]
    logsig = jnp.minimum(z, 0.0) - jnp.log(1.0 + jnp.exp(-jnp.abs(z)))
    g_ref[...] = logsig * (1.0 / GLA_GATE_NORM)


def _gla_in(x, mods4, layer, row_of_batch, ng, win, wga, wgb, bg, tm, state_only=False):
    bsz, t, d = x.shape
    kdim, vdim = d // 2, d
    tm = min(tm, t)
    tok = lambda w: pl.BlockSpec((None, tm, w), lambda b, i: (b, i, 0))
    full = lambda a: pl.BlockSpec(a.shape, lambda b, i: (0,) * a.ndim)
    kern = functools.partial(_gla_in_kernel, kdim=kdim, vdim=vdim,
                             qscale=float((kdim // GLA_HEADS) ** -0.5), state_only=state_only)
    widths = (kdim, vdim, 2 * kdim) if state_only else (kdim, kdim, vdim, vdim, 2 * kdim)
    dtypes = (BF16,) * (len(widths) - 1) + (F32,)
    return pl.pallas_call(
        kern,
        out_shape=tuple(jax.ShapeDtypeStruct((bsz, t, w), dt) for w, dt in zip(widths, dtypes)),
        grid=(bsz, t // tm),
        in_specs=[
            tok(d),
            _mod_spec(layer, 0, d, row_of_batch),
            _mod_spec(layer, 1, d, row_of_batch),
            full(ng), full(win), full(wga), full(wgb), full(bg),
        ],
        out_specs=tuple(tok(w) for w in widths),
        compiler_params=_cparams(("arbitrary", "arbitrary")),
        name="gla_in",
    )(x, mods4, mods4, ng, win, wga, wgb, bg)


def _block_masks(rows, chunk):
    i = np.arange(rows)
    same = (i[:, None] // chunk) == (i[None, :] // chunk)
    lower = same & (i[:, None] >= i[None, :])
    upper = same & (i[:, None] <= i[None, :])
    return jnp.asarray(lower, BF16), jnp.asarray(upper, BF16)


def _per_chunk_row(a, chunk, r):
    rows, w = a.shape
    parts = [jnp.broadcast_to(a[c * chunk + r:c * chunk + r + 1, :], (chunk, w))
             for c in range(rows // chunk)]
    return jnp.concatenate(parts, axis=0)


def _block_terms(q, k, v, g2, lo, up, chunk, want_out):
    dk = g2.shape[1] // 2
    gh, gl = _split(g2)
    pre = _dot(lo, gh) + _dot(lo, gl)
    tot = _per_chunk_row(pre, chunk, chunk - 1)
    fwd = lax.broadcasted_iota(jnp.int32, g2.shape, 1) < dk
    cum = jnp.where(fwd, pre, tot - pre + g2)
    kf = k.astype(F32)
    kf2 = jnp.concatenate([kf, kf], axis=1)
    kl = (kf2 * jnp.exp(tot - cum)).astype(BF16)
    etot = jnp.exp(tot)
    if not want_out:
        return kl, etot, None, None
    mid = _per_chunk_row(cum, chunk, chunk // 2)
    qf = q.astype(F32)
    qf2 = jnp.concatenate([qf, qf], axis=1)
    qe = (qf2 * jnp.exp(cum)).astype(BF16)
    qi = (qf2 * jnp.exp(cum - mid)).astype(BF16)
    ki = (kf2 * jnp.exp(mid - cum)).astype(BF16)
    s = (jnp.where(lo > 0, _dot(qi[:, :dk], ki[:, :dk], _NT), 0.0)
         + jnp.where(up > 0, _dot(qi[:, dk:], ki[:, dk:], _NT), 0.0))
    return kl, etot, qe, _dot(s.astype(BF16), v)


def _store_chunk_states(v, kl, etot, ds_ref, e_ref, first_chunk, chunk):
    for c in range(v.shape[0] // chunk):
        rows = slice(c * chunk, (c + 1) * chunk)
        ds_ref[first_chunk + c] = _dot(v[rows, :], kl[rows, :], _TN)
        e_ref[first_chunk + c] = etot[c * chunk:c * chunk + 8, :]


def _scan_states(s0, ds_ref, e_ref, sc_ref, n, dk, first=0):
    def body(c, carry):
        sf, sb = carry
        r = n - 1 - c
        if sc_ref is not None:
            sc_ref[first + c, :, :dk] = sf.astype(BF16)
            sc_ref[first + r, :, dk:] = sb.astype(BF16)
        return (sf * e_ref[first + c, 0:1, :dk] + ds_ref[first + c, :, :dk],
                sb * e_ref[first + r, 0:1, dk:] + ds_ref[first + r, :, dk:])

    return lax.fori_loop(0, n, body, (s0[:, :dk], s0[:, dk:]))


def _gla_state_kernel(k_ref, v_ref, g_ref, lo_ref, up_ref, s_ref, ds_ref, e_ref, *, chunk, block):
    nh, dv = s_ref.shape[0], s_ref.shape[1]
    t = k_ref.shape[0]
    dk = k_ref.shape[1] // nh
    n = t // chunk
    per = block // chunk

    def terms(i, carry):
        rows = pl.ds(pl.multiple_of(i * block, block), block)
        for h in range(nh):
            v = v_ref[rows, h * dv:(h + 1) * dv]
            kl, etot, _, _ = _block_terms(None, k_ref[rows, h * dk:(h + 1) * dk], v,
                                          g_ref[rows, 2 * h * dk:2 * (h + 1) * dk],
                                          lo_ref[...], up_ref[...], chunk, False)
            _store_chunk_states(v, kl, etot, ds_ref, e_ref, h * n + i * per, chunk)
        return carry

    lax.fori_loop(0, t // block, terms, 0)
    for h in range(nh):
        sf, sb = _scan_states(jnp.zeros((dv, 2 * dk), F32), ds_ref, e_ref, None, n, dk, first=h * n)
        s_ref[h, :, :dk] = sf
        s_ref[h, :, dk:] = sb


def _gla_scan_kernel(q_ref, k_ref, v_ref, g_ref, s0_ref, ng_ref, lo_ref, up_ref,
                     o_ref, oi_ref, qe_ref, ds_ref, e_ref, sc_ref, *, chunk, block):
    t = q_ref.shape[0]
    dk = q_ref.shape[1]
    per = block // chunk

    group = GLA_BLOCKS_PER_TRIP if (t // block) % GLA_BLOCKS_PER_TRIP == 0 else 1

    def terms(i, carry):
        for u in range(group):
            blk = i * group + u
            rows = pl.ds(pl.multiple_of(blk * block, block), block)
            v = v_ref[rows, :]
            kl, etot, qe, oi = _block_terms(q_ref[rows, :], k_ref[rows, :], v, g_ref[rows, :],
                                            lo_ref[...], up_ref[...], chunk, True)
            _store_chunk_states(v, kl, etot, ds_ref, e_ref, blk * per, chunk)
            qe_ref[rows, :] = qe
            oi_ref[rows, :] = oi
        return carry

    lax.fori_loop(0, t // (block * group), terms, 0)
    _scan_states(s0_ref[...], ds_ref, e_ref, sc_ref, t // chunk, dk)

    fgroup = GLA_FINISH_BLOCKS_PER_TRIP if (t // block) % GLA_FINISH_BLOCKS_PER_TRIP == 0 else 1

    def finish(i, carry):
        for u in range(fgroup):
            blk = i * fgroup + u
            rows = pl.ds(pl.multiple_of(blk * block, block), block)
            inter = [_dot(qe_ref[pl.ds(pl.multiple_of(blk * block + c * chunk, chunk), chunk), :],
                          sc_ref[blk * per + c], _NT) for c in range(per)]
            o = oi_ref[rows, :] + jnp.concatenate(inter, axis=0)
            o_ref[rows, :] = _rmsnorm(o, ng_ref[...]).astype(o_ref.dtype)
        return carry

    lax.fori_loop(0, t // (block * fgroup), finish, 0)


def _gla_state(k, v, g, chunk, block):
    bsz, t, kdim = k.shape
    vdim = v.shape[-1]
    nh = GLA_HEADS
    dk, dv = kdim // nh, vdim // nh
    block = min(block, t)
    n = t // chunk
    lo, up = _block_masks(block, chunk)
    mask_spec = pl.BlockSpec((block, block), lambda b: (0, 0))
    seq = lambda w: pl.BlockSpec((None, t, w), lambda b: (b, 0, 0))
    return pl.pallas_call(
        functools.partial(_gla_state_kernel, chunk=chunk, block=block),
        out_shape=jax.ShapeDtypeStruct((bsz, nh, dv, 2 * dk), F32),
        grid=(bsz,),
        in_specs=[seq(kdim), seq(vdim), seq(2 * kdim), mask_spec, mask_spec],
        out_specs=pl.BlockSpec((None, nh, dv, 2 * dk), lambda b: (b, 0, 0, 0)),
        scratch_shapes=[pltpu.VMEM((nh * n, dv, 2 * dk), F32), pltpu.VMEM((nh * n, 8, 2 * dk), F32)],
        compiler_params=_cparams(("arbitrary",)),
        name="gla_state",
    )(k, v, g, lo, up)


def _gla_scan(q, k, v, g, s0, norm_g, chunk, block):
    bsz, t, kdim = k.shape
    vdim = v.shape[-1]
    nh = GLA_HEADS
    dk, dv = kdim // nh, vdim // nh
    block = min(block, t)
    n = t // chunk
    lo, up = _block_masks(block, chunk)
    kspec = pl.BlockSpec((None, t, dk), lambda b, h: (b, 0, h))
    vspec = pl.BlockSpec((None, t, dv), lambda b, h: (b, 0, h))
    mask_spec = pl.BlockSpec((block, block), lambda b, h: (0, 0))
    return pl.pallas_call(
        functools.partial(_gla_scan_kernel, chunk=chunk, block=block),
        out_shape=jax.ShapeDtypeStruct((bsz, t, vdim), BF16),
        grid=(bsz, nh),
        in_specs=[
            kspec, kspec, vspec,
            pl.BlockSpec((None, t, 2 * dk), lambda b, h: (b, 0, h)),
            pl.BlockSpec((None, None, dv, 2 * dk), lambda b, h: (b, h, 0, 0)),
            pl.BlockSpec((1, dv), lambda b, h: (0, 0)),
            mask_spec, mask_spec,
        ],
        out_specs=vspec,
        scratch_shapes=[
            pltpu.VMEM((t, dv), F32),
            pltpu.VMEM((t, 2 * dk), BF16),
            pltpu.VMEM((n, dv, 2 * dk), F32),
            pltpu.VMEM((n, 8, 2 * dk), F32),
            pltpu.VMEM((n, dv, 2 * dk), BF16),
        ],
        compiler_params=_cparams(("arbitrary", "arbitrary")),
        name="gla_scan",
    )(q, k, v, g, s0, norm_g, lo, up)


def _route(h2, wr_t, br, earlier_bf, count_ref):
    tm = h2.shape[0]
    logits = _dot3(wr_t, h2, _NT)
    scores = _sigmoid(logits)
    sel = scores + br
    row = [sel[e:e + 1, :] for e in range(N_EXPERTS)]
    picked = []
    for e in range(N_EXPERTS):
        g0 = (e // EXPERTS_PER_GROUP) * EXPERTS_PER_GROUP
        ahead_count = jnp.zeros(row[e].shape, jnp.int32)
        for j in range(g0, g0 + EXPERTS_PER_GROUP):
            if j == e:
                continue
            ahead = (row[j] >= row[e]) if j < e else (row[j] > row[e])
            ahead_count = ahead_count + ahead.astype(jnp.int32)
        picked.append(ahead_count < TOP_K)
    zero = jnp.zeros_like(row[0])
    gscore, pair, first_w, second_w = [], [], [], []
    for g in range(N_EXPERT_GROUPS):
        acc, pidx, fw, sw = zero, zero, zero, zero
        seen = None
        for a in range(EXPERTS_PER_GROUP):
            e = g * EXPERTS_PER_GROUP + a
            sc = scores[e:e + 1, :]
            acc = acc + jnp.where(picked[e], row[e], 0.0)
            if seen is None:
                is_first = picked[e]
            else:
                is_first = picked[e] & jnp.logical_not(seen)
                is_second = picked[e] & seen
                sw = sw + jnp.where(is_second, sc, 0.0)
                pidx = pidx + jnp.where(is_second, float(a), 0.0)
            fw = fw + jnp.where(is_first, sc, 0.0)
            if a < len(_PAIR_BASE):
                pidx = pidx + jnp.where(is_first, float(_PAIR_BASE[a] - a - 1), 0.0)
            seen = picked[e] if seen is None else (seen | picked[e])
        gscore.append(acc)
        pair.append(pidx)
        first_w.append(fw)
        second_w.append(sw)
    cls, wa, wb = zero, zero, zero
    for g in range(N_EXPERT_GROUPS):
        ok = None
        for j in range(N_EXPERT_GROUPS):
            if j == g:
                continue
            c = (gscore[g] > gscore[j]) if j < g else (gscore[g] >= gscore[j])
            ok = c if ok is None else (ok & c)
        cls = cls + jnp.where(ok, pair[g] + float(len(_PAIRS) * g), 0.0)
        wa = wa + jnp.where(ok, first_w[g], 0.0)
        wb = wb + jnp.where(ok, second_w[g], 0.0)
    denom = wa + wb
    wa = wa / denom
    wb = wb / denom

    cid = lax.broadcasted_iota(jnp.int32, (CLASS_ROWS, tm), 0).astype(F32)
    onehot = (cid == cls).astype(BF16)
    before = _dot(onehot, earlier_bf)
    oh = onehot.astype(F32)
    base = count_ref[...][:, 0:1]
    rank = jnp.sum(oh * (before + base), axis=0, keepdims=True)
    count_ref[...] = count_ref[...] + jnp.sum(oh, axis=1, keepdims=True)
    pad = jnp.zeros((INFO_ROWS - 4, tm), F32)
    return jnp.concatenate([cls, rank, wa, wb, pad], axis=0)


def _earlier_matrix(tm):
    i = np.arange(tm)
    return jnp.asarray(i[:, None] < i[None, :], BF16)


def _store_tiled_rows(ref, x):
    tm, d = x.shape
    per = d // LANES
    for c in range(per):
        ref[pl.ds(c, tm, stride=per), :] = x[:, c * LANES:(c + 1) * LANES]


def _load_tiled_rows(ref, tm):
    per = ref.shape[0] // tm
    return jnp.concatenate([ref[pl.ds(c, tm, stride=per), :] for c in range(per)], axis=1)


def _route_outputs(h2, wr_ref, br_ref, earlier_ref, count_ref, pay_ref, info_ref, counts_ref):
    first = (pl.program_id(0) == 0) & (pl.program_id(1) == 0)

    @pl.when(first)
    def _():
        count_ref[...] = jnp.zeros_like(count_ref)

    info = _route(h2, wr_ref[...], br_ref[...], earlier_ref[...], count_ref)
    _store_tiled_rows(pay_ref, h2)
    info_ref[...] = info
    counts_ref[...] = count_ref[...]


def _route_out_shapes(bsz, t, d):
    return (
        jax.ShapeDtypeStruct((bsz * t * (d // LANES), LANES), F32),
        jax.ShapeDtypeStruct((INFO_ROWS, bsz * t), F32),
        jax.ShapeDtypeStruct((CLASS_ROWS, 128), F32),
    )


def _route_out_specs(tm, nt, d):
    return (
        pl.BlockSpec((tm * (d // LANES), LANES), lambda b, i, *_: (b * nt + i, 0)),
        pl.BlockSpec((INFO_ROWS, tm), lambda b, i, *_: (0, b * nt + i)),
        pl.BlockSpec((CLASS_ROWS, 128), lambda b, i, *_: (0, 0)),
    )


def _plan(info, counts, n, tmoe):
    cls = info[0].astype(jnp.int32)
    rank = info[1].astype(jnp.int32)
    cnt = counts[:N_CLASSES, 0].astype(jnp.int32)
    padded = (cnt + tmoe - 1) // tmoe * tmoe
    ends = jnp.cumsum(padded)
    dest = (ends - padded)[cls] + rank
    n_tiles = n // tmoe + N_CLASSES
    rows = jnp.stack([jnp.arange(n, dtype=F32), info[2], info[3], jnp.zeros((n,), F32)], axis=1)
    srt = jnp.zeros((n_tiles * tmoe, 4), F32).at[dest].set(rows, unique_indices=True)
    tok = srt[:, 0].astype(jnp.int32)
    n_used = ends[-1] // tmoe
    tile = jnp.minimum(jnp.arange(n_tiles, dtype=jnp.int32), n_used - 1)
    tcls = jnp.sum((ends[None, :] <= (tile * tmoe)[:, None]).astype(jnp.int32), axis=1)
    group, pair = tcls // len(_PAIRS), tcls % len(_PAIRS)
    pa = jnp.asarray([p[0] for p in _PAIRS], jnp.int32)[pair]
    pb = jnp.asarray([p[1] for p in _PAIRS], jnp.int32)[pair]
    ea = group * EXPERTS_PER_GROUP + pa
    eb = group * EXPERTS_PER_GROUP + pb
    return dest, tok, srt, ea, eb, n_used.reshape(1).astype(jnp.int32)


def _post0_kernel(o_ref, r_ref, x_ref, g1_ref, sh2_ref, sc2_ref, n2_ref, wout_ref, wr_ref, br_ref,
                  earlier_ref, x1_ref, pay_ref, info_ref, counts_ref, count_ref):
    a = (o_ref[...].astype(F32) * _silu(r_ref[...].astype(F32))).astype(BF16)
    x1 = x_ref[...] + g1_ref[...] * _dot(a, wout_ref[...])
    x1_ref[...] = x1
    h2 = _norm_modulate(x1, n2_ref[...], sc2_ref[...], sh2_ref[...])
    _route_outputs(h2, wr_ref, br_ref, earlier_ref, count_ref, pay_ref, info_ref, counts_ref)


def _post0(o, r, x, mods4, layer, n2, wout, wr_t, br, tm):
    bsz, t, d = x.shape
    tm = min(tm, t)
    rb = lambda b: b
    tok = lambda w: pl.BlockSpec((None, tm, w), lambda b, i: (b, i, 0))
    full = lambda a: pl.BlockSpec(a.shape, lambda b, i: (0,) * a.ndim)
    nt = t // tm
    earlier = _earlier_matrix(tm)
    return pl.pallas_call(
        _post0_kernel,
        out_shape=(jax.ShapeDtypeStruct((bsz, t, d), F32),) + _route_out_shapes(bsz, t, d),
        grid=(bsz, nt),
        in_specs=[
            tok(d), tok(d), tok(d),
            _mod_spec(layer, 2, d, rb), _mod_spec(layer, 3, d, rb), _mod_spec(layer, 4, d, rb),
            full(n2), full(wout), full(wr_t), full(br), full(earlier),
        ],
        out_specs=(tok(d),) + _route_out_specs(tm, nt, d),
        scratch_shapes=[pltpu.VMEM((CLASS_ROWS, 128), F32)],
        compiler_params=_cparams(("arbitrary", "arbitrary")),
        name="post0",
    )(o, r, x, mods4, mods4, mods4, n2, wout, wr_t, br, earlier)


def _row_group(i, per):
    return pl.ds(pl.multiple_of(i * per, per), per)


def _moe_kernel(tok_ref, ea_ref, eb_ref, used_ref, p_hbm, gw_ref, wa_ref, wb_ref, f_ref,
                pbuf0, pbuf1, gsem, *, d, tm):
    del ea_ref, eb_ref
    j = pl.program_id(0)
    n_tiles = pl.num_programs(0)
    used = used_ref[0]
    per = d // LANES
    pbufs = (pbuf0, pbuf1)

    def gather_row(tile, r, p, pred, lane):
        i = tok_ref[tile * tm + r]

        @pl.when(pred)
        def _():
            pltpu.make_async_copy(p_hbm.at[_row_group(i, per), :], pbufs[p].at[_row_group(r, per), :],
                                  gsem.at[p]).start(priority=lane)

    def gather_wait(p):
        pltpu.make_async_copy(p_hbm.at[pl.ds(0, tm * per), :], pbufs[p], gsem.at[p]).wait()

    @pl.when(j == 0)
    def _():
        def body(blk, carry):
            for u in range(DMA_UNROLL):
                gather_row(0, blk * DMA_UNROLL + u, 0, True, u % 2)
            return carry
        lax.fori_loop(0, tm // DMA_UNROLL, body, 0)

    def step(p, beside_matmuls):
        @pl.when(j < used)
        def _():
            has_next = j + 1 < used
            nxt = jnp.minimum(j + 1, n_tiles - 1)
            if beside_matmuls:
                gather_wait(p)
                for r in range(tm):
                    gather_row(nxt, r, 1 - p, has_next, r % 2)
            else:
                @pl.when(j >= 0)
                def _():
                    for r in range(tm):
                        gather_row(nxt, r, 1 - p, has_next, r % 2)
                gather_wait(p)
            h = _load_tiled_rows(pbufs[p], tm).astype(BF16)
            gw = gw_ref[...]

            def expert(w_ref):
                de = w_ref.shape[2]
                he = (_silu(_dot(h, w_ref[0])) * _dot(h, w_ref[1])).astype(BF16)
                return jnp.concatenate([_dot(he, w_ref[2, :de, :]), _dot(he, w_ref[2, de:, :])], axis=1)

            _store_tiled_rows(f_ref, gw[:, 1:2] * expert(wa_ref) + gw[:, 2:3] * expert(wb_ref))

    @pl.when(j % 2 == 0)
    def _():
        step(0, True)

    @pl.when(j % 2 == 1)
    def _():
        step(1, False)

    @pl.when(j >= used)
    def _():
        f_ref[...] = jnp.zeros(f_ref.shape, F32)


def _moe(payload, tok, srt, ea, eb, n_used, w_all, layer, tm):
    n_sorted = tok.shape[0]
    n_tiles = n_sorted // tm
    _, _, _, d, de = w_all.shape
    per = d // LANES
    wa = lambda j, tok, ea, eb, used: (layer, ea[j], 0, 0, 0)
    wb = lambda j, tok, ea, eb, used: (layer, eb[j], 0, 0, 0)
    return pl.pallas_call(
        functools.partial(_moe_kernel, d=d, tm=tm),
        out_shape=jax.ShapeDtypeStruct((n_sorted * per, LANES), F32),
        grid_spec=pltpu.PrefetchScalarGridSpec(
            num_scalar_prefetch=4,
            grid=(n_tiles,),
            in_specs=[
                pl.BlockSpec(memory_space=pl.ANY),
                pl.BlockSpec((tm, srt.shape[1]), lambda j, *_: (j, 0)),
                pl.BlockSpec((None, None, 3, d, de), wa), pl.BlockSpec((None, None, 3, d, de), wb),
            ],
            out_specs=pl.BlockSpec((tm * per, LANES), lambda j, *_: (j, 0)),
            scratch_shapes=[pltpu.VMEM((tm * per, LANES), F32), pltpu.VMEM((tm * per, LANES), F32),
                            pltpu.SemaphoreType.DMA((2,))],
        ),
        compiler_params=_cparams(("arbitrary",)),
        name="moe",
    )(tok, ea, eb, n_used, payload, srt, w_all, w_all)


def _expert_weights_kernel(w1_ref, w3_ref, w2_ref, o_ref):
    de = w2_ref.shape[0]
    o_ref[0] = w1_ref[...].astype(BF16)
    o_ref[1] = w3_ref[...].astype(BF16)
    o_ref[2, :de, :] = w2_ref[:, :de].astype(BF16)
    o_ref[2, de:, :] = w2_ref[:, de:].astype(BF16)


def _expert_weights(w1, w3, w2):
    nl, ne, d, de = w1.shape
    assert d == 2 * de
    up = pl.BlockSpec((None, None, d, de), lambda l, e: (l, e, 0, 0))
    return pl.pallas_call(
        _expert_weights_kernel,
        out_shape=jax.ShapeDtypeStruct((nl, ne, 3, d, de), BF16),
        grid=(nl, ne),
        in_specs=[up, up, pl.BlockSpec((None, None, de, d), lambda l, e: (l, e, 0, 0))],
        out_specs=pl.BlockSpec((None, None, 3, d, de), lambda l, e: (l, e, 0, 0, 0)),
        compiler_params=_cparams(("arbitrary", "arbitrary")),
        name="expert_weights",
    )(w1, w3, w2)


def _sparse_moe(payload, info, counts, w_all, layer):
    n = info.shape[1]
    tmoe = min(TM_MOE, n)
    dest, tok, srt, ea, eb, n_used = _plan(info, counts, n, tmoe)
    return _moe(payload, tok, srt, ea, eb, n_used, w_all, layer, tmoe), dest


def _gather_rows_start(dest_ref, f_hbm, fbuf, fsem, tile, tm, per, pred, unrolled):
    def one(r, lane):
        i = dest_ref[tile * tm + r]

        @pl.when(pred)
        def _():
            pltpu.make_async_copy(f_hbm.at[_row_group(i, per), :], fbuf.at[_row_group(r, per), :],
                                  fsem).start(priority=lane)

    if unrolled:
        for r in range(tm):
            one(r, r % 2)
    else:
        def body(blk, carry):
            for u in range(DMA_UNROLL):
                one(blk * DMA_UNROLL + u, u % 2)
            return carry
        lax.fori_loop(0, tm // DMA_UNROLL, body, 0)


def _gather_rows_wait(f_hbm, fbuf, fsem):
    pltpu.make_async_copy(f_hbm.at[pl.ds(0, fbuf.shape[0]), :], fbuf, fsem).wait()


def _pool_constants(tm, d):
    ng = len(POOL_WINDOWS)
    cg = d // ng
    pos = np.arange(tm)
    seg, off = pos // GRID_W, pos % GRID_W
    mats = np.zeros((ng, tm, tm), np.float32)
    inv = np.zeros((tm, d), np.float32)
    for gi, w in enumerate(POOL_WINDOWS):
        lo = np.clip(off - w // 2, 0, GRID_W)
        hi = np.clip(off - w // 2 + w, 0, GRID_W)
        same = seg[:, None] == seg[None, :]
        inside = (off[None, :] >= lo[:, None]) & (off[None, :] < hi[:, None])
        mats[gi] = (same & inside).astype(np.float32)
        inv[:, gi * cg:(gi + 1) * cg] = (1.0 / (hi - lo).astype(np.float32))[:, None]
    return jnp.asarray(mats, BF16), jnp.asarray(inv, F32)


def _layer1_kernel(dest_ref, x_ref, f_hbm, g2p_ref, sh1_ref, sc1_ref, g1_ref, sh2_ref, sc2_ref,
                   n1_ref, n2_ref, pm_ref, inv_ref, wp_ref, bp_ref, ps_ref, wr_ref, br_ref, earlier_ref,
                   x3_ref, pay_ref, info_ref, counts_ref, fbuf, fsem, x2_ref, count_ref):
    tm, d = x_ref.shape
    per = d // LANES
    nt = pl.num_programs(1)
    lin = pl.program_id(0) * nt + pl.program_id(1)

    @pl.when(lin == 0)
    def _():
        _gather_rows_start(dest_ref, f_hbm, fbuf, fsem, 0, tm, per, True, False)

    @pl.when(lin >= 0)
    def _():
        _gather_rows_wait(f_hbm, fbuf, fsem)
        x2_ref[...] = x_ref[...] + g2p_ref[...] * _load_tiled_rows(fbuf, tm)

    total = pl.num_programs(0) * nt
    _gather_rows_start(dest_ref, f_hbm, fbuf, fsem, jnp.minimum(lin + 1, total - 1), tm, per,
                       lin + 1 < total, True)
    x2 = x2_ref[...]
    h = _norm_modulate(x2, n1_ref[...], sc1_ref[...], sh1_ref[...])
    hb = h.astype(BF16)
    ng = pm_ref.shape[0]
    cg = h.shape[1] // ng
    ys = []
    for gi in range(ng):
        cs = slice(gi * cg, (gi + 1) * cg)
        wsum = _dot(pm_ref[gi], hb[:, cs])
        pooled = wsum * inv_ref[:, cs] - h[:, cs]
        ys.append(_dot(pooled.astype(BF16), wp_ref[gi]))
    y = (jnp.concatenate(ys, axis=1) + bp_ref[...]) * ps_ref[...]
    x3 = x2 + g1_ref[...] * y
    x3_ref[...] = x3
    h2 = _norm_modulate(x3, n2_ref[...], sc2_ref[...], sh2_ref[...])
    _route_outputs(h2, wr_ref, br_ref, earlier_ref, count_ref, pay_ref, info_ref, counts_ref)


def _layer1(x1, f_sorted, dest, mods4, n1, n2, wp, bp, ps, wr_t, br, tm):
    bsz, t, d = x1.shape
    tm = min(tm, t)
    pm, inv = _pool_constants(tm, d)
    earlier = _earlier_matrix(tm)
    rb = lambda b: b
    tok = lambda w: pl.BlockSpec((None, tm, w), lambda b, i, dest: (b, i, 0))
    full = lambda a: pl.BlockSpec(a.shape, lambda b, i, dest: (0,) * a.ndim)
    nt = t // tm
    return pl.pallas_call(
        _layer1_kernel,
        out_shape=(jax.ShapeDtypeStruct((bsz, t, d), F32),) + _route_out_shapes(bsz, t, d),
        grid_spec=pltpu.PrefetchScalarGridSpec(
            num_scalar_prefetch=1,
            grid=(bsz, nt),
            in_specs=[
                tok(d), pl.BlockSpec(memory_space=pl.ANY),
                _mod_spec(0, 5, d, rb),
                _mod_spec(1, 0, d, rb), _mod_spec(1, 1, d, rb), _mod_spec(1, 2, d, rb),
                _mod_spec(1, 3, d, rb), _mod_spec(1, 4, d, rb),
                full(n1), full(n2), full(pm), full(inv), full(wp), full(bp), full(ps),
                full(wr_t), full(br), full(earlier),
            ],
            out_specs=(tok(d),) + _route_out_specs(tm, nt, d),
            scratch_shapes=[pltpu.VMEM((tm * (d // LANES), LANES), F32), pltpu.SemaphoreType.DMA(()),
                            pltpu.VMEM((tm, d), F32), pltpu.VMEM((CLASS_ROWS, 128), F32)],
        ),
        compiler_params=_cparams(("arbitrary", "arbitrary")),
        name="layer1",
    )(dest, x1, f_sorted, mods4, mods4, mods4, mods4, mods4, mods4, n1, n2, pm, inv, wp, bp, ps,
      wr_t, br, earlier)


def _final_kernel(dest_ref, x_ref, f_hbm, g2_ref, fg_ref, o_ref, fbuf, fsem):
    tm, d = x_ref.shape
    per = d // LANES
    nt = pl.num_programs(1)
    lin = pl.program_id(0) * nt + pl.program_id(1)

    @pl.when(lin == 0)
    def _():
        _gather_rows_start(dest_ref, f_hbm, fbuf, fsem, 0, tm, per, True, False)

    @pl.when(lin >= 0)
    def _():
        _gather_rows_wait(f_hbm, fbuf, fsem)
        f = _load_tiled_rows(fbuf, tm)
        o_ref[...] = _rmsnorm(x_ref[...] + g2_ref[...] * f, fg_ref[...])

    total = pl.num_programs(0) * nt
    _gather_rows_start(dest_ref, f_hbm, fbuf, fsem, jnp.minimum(lin + 1, total - 1), tm, per,
                       lin + 1 < total, False)


def _final(x3, f_sorted, dest, mods4, fg, tm):
    bsz, t, d = x3.shape
    tm = min(tm, t)
    tok = pl.BlockSpec((None, tm, d), lambda b, i, dest: (b, i, 0))
    return pl.pallas_call(
        _final_kernel,
        out_shape=jax.ShapeDtypeStruct((bsz, t, d), F32),
        grid_spec=pltpu.PrefetchScalarGridSpec(
            num_scalar_prefetch=1,
            grid=(bsz, t // tm),
            in_specs=[tok, pl.BlockSpec(memory_space=pl.ANY), _mod_spec(1, 5, d, lambda b: b),
                      pl.BlockSpec((1, d), lambda b, i, dest: (0, 0))],
            out_specs=tok,
            scratch_shapes=[pltpu.VMEM((tm * (d // LANES), LANES), F32), pltpu.SemaphoreType.DMA(())],
        ),
        compiler_params=_cparams(("arbitrary", "arbitrary")),
        name="final",
    )(dest, x3, f_sorted, mods4, fg)


def kernel(x, c, ctx, c_ctx, norm1_g, norm2_g, w_mod, b_mod, gla_w_in, gla_w_gate_a, gla_w_gate_b,
           gla_b_gate, gla_norm_g, gla_w_out, pool_w, pool_b, pool_scale, w_router, b_router,
           w_gate_e, w_up_e, w_down_e, final_g):
    bsz, t, d = x.shape
    depth = w_mod.shape[0]
    assert depth == 2 and t % GRID_W == 0
    kdim = d // 2
    row = lambda a: a.reshape(1, -1)

    c_rows = -(-(bsz + 1) // 8) * 8
    c_all = jnp.zeros((c_rows, d), F32).at[:bsz].set(c).at[bsz].set(c_ctx)
    mods = _modulation(c_all, w_mod, b_mod)
    mods4 = mods.reshape(depth, c_rows, 1, 6 * d)

    win = gla_w_in[0].astype(BF16)
    wga = jnp.concatenate([gla_w_gate_a[0, 0], gla_w_gate_a[0, 1]], axis=1).astype(BF16)
    dk = kdim // GLA_HEADS
    by_head = lambda a: a.reshape(a.shape[0], GLA_HEADS, 1, dk)
    zero = jnp.zeros((GLA_GATE_RANK, GLA_HEADS, 1, dk), F32)
    wgb = jnp.concatenate([
        jnp.concatenate([by_head(gla_w_gate_b[0, 0]), zero], axis=2),
        jnp.concatenate([zero, by_head(gla_w_gate_b[0, 1])], axis=2)], axis=0)
    wgb = wgb.reshape(2 * GLA_GATE_RANK, 2 * kdim).astype(BF16)
    bg = jnp.stack([gla_b_gate[0, 0].reshape(GLA_HEADS, dk), gla_b_gate[0, 1].reshape(GLA_HEADS, dk)],
                   axis=1).reshape(1, 2 * kdim)
    n1_0 = row(norm1_g[0])
    q, k, v, r, g = _gla_in(x, mods4, 0, lambda b: b, n1_0, win, wga, wgb, bg, TM_PROJ)
    kc, vc, gc = _gla_in(ctx, mods4, 0, lambda b: bsz, n1_0, win, wga, wgb, bg, TM_CTX, state_only=True)
    s0 = _gla_state(kc, vc, gc, GLA_CHUNK, GLA_BLOCK)
    o = _gla_scan(q, k, v, g, s0, row(gla_norm_g[0]), GLA_CHUNK, GLA_BLOCK)

    wr_t = jnp.transpose(w_router)
    br = b_router.reshape(N_EXPERTS, 1)
    x1, payload, info, counts = _post0(o, r, x, mods4, 0, row(norm2_g[0]),
                                       gla_w_out[0].astype(BF16), wr_t, br, TM_POST)
    w_experts = _expert_weights(w_gate_e, w_up_e, w_down_e)
    f0, dest0 = _sparse_moe(payload, info, counts, w_experts, 0)

    x3, payload, info, counts = _layer1(x1, f0, dest0, mods4, row(norm1_g[1]), row(norm2_g[1]),
                                        pool_w[0].astype(BF16), row(pool_b[0]), row(pool_scale[0]),
                                        wr_t, br, TM_POOL)
    f1, dest1 = _sparse_moe(payload, info, counts, w_experts, 1)
    return _final(x3, f1, dest1, mods4, row(final_g), TM_FINAL)
```

```python
import functools

import numpy as np
import jax
import jax.numpy as jnp
from jax import lax
from jax.experimental import pallas as pl
from jax.experimental.pallas import tpu as pltpu

EPS = 1e-6
GRID_W = 64
GLA_HEADS = 4
GLA_GATE_RANK = 16
GLA_GATE_NORM = 16.0
GLA_CHUNK = 128
GLA_BLOCK = 256
GLA_BLOCKS_PER_TRIP = 8
GLA_FINISH_BLOCKS_PER_TRIP = 8
POOL_WINDOWS = (2, 4, 8, 16)
N_EXPERTS = 16
N_EXPERT_GROUPS = 4
EXPERTS_PER_GROUP = N_EXPERTS // N_EXPERT_GROUPS
TOP_K = 2

_PAIRS = tuple((a, b) for a in range(EXPERTS_PER_GROUP) for b in range(a + 1, EXPERTS_PER_GROUP))
_PAIR_BASE = (0, 3, 5)
N_CLASSES = N_EXPERT_GROUPS * len(_PAIRS)
CLASS_ROWS = 32
INFO_ROWS = 8
LANES = 128
SUBLANES = 8

TM_PROJ = 1024
TM_CTX = 256
TM_POST = 1024
TM_POOL = 512
TM_MOE = 512
TM_FINAL = 512
DMA_UNROLL = 8

F32 = jnp.float32
BF16 = jnp.bfloat16

_NT = (((1,), (1,)), ((), ()))
_TN = (((0,), (0,)), ((), ()))
_VMEM_LIMIT = 56 * 1024 * 1024


def _cparams(sem):
    return pltpu.CompilerParams(dimension_semantics=sem, vmem_limit_bytes=_VMEM_LIMIT)


def _dot(a, b, dims=None):
    if dims is None:
        return jnp.dot(a, b, preferred_element_type=F32)
    return lax.dot_general(a, b, dims, preferred_element_type=F32)


def _split(a):
    hi = a.astype(BF16)
    lo = (a - hi.astype(F32)).astype(BF16)
    return hi, lo


def _dot3(a, b, dims=None):
    ah, al = _split(a)
    bh, bl = _split(b)
    return _dot(ah, bh, dims) + _dot(ah, bl, dims) + _dot(al, bh, dims)


def _sigmoid(x):
    return 1.0 / (1.0 + jnp.exp(-x))


def _silu(x):
    return x * _sigmoid(x)


def _rmsnorm(xf, g):
    return xf * lax.rsqrt(jnp.mean(xf * xf, axis=-1, keepdims=True) + EPS) * g


def _norm_modulate(xf, g, scale, shift):
    return xf * lax.rsqrt(jnp.mean(xf * xf, axis=-1, keepdims=True) + EPS) * (g * (1.0 + scale)) + shift


def _mod_kernel(c_ref, w_ref, b_ref, o_ref):
    o_ref[...] = _dot3(_silu(c_ref[...]), w_ref[...]) + b_ref[...]


def _modulation(c_all, w_mod, b_mod):
    depth, d, d6 = w_mod.shape
    rows = c_all.shape[0]
    tn = 1536
    return pl.pallas_call(
        _mod_kernel,
        out_shape=jax.ShapeDtypeStruct((depth, rows, d6), F32),
        grid=(depth, d6 // tn),
        in_specs=[
            pl.BlockSpec((rows, d), lambda l, j: (0, 0)),
            pl.BlockSpec((None, d, tn), lambda l, j: (l, 0, j)),
            pl.BlockSpec((None, 1, tn), lambda l, j: (l, 0, j)),
        ],
        out_specs=pl.BlockSpec((None, rows, tn), lambda l, j: (l, 0, j)),
        compiler_params=_cparams(("arbitrary", "arbitrary")),
        name="mod",
    )(c_all, w_mod, b_mod.reshape(depth, 1, d6))


def _mod_spec(layer, chunk, d, row_of_batch):
    return pl.BlockSpec((None, None, 1, d), lambda b, t, *_: (layer, row_of_batch(b), 0, chunk))


def _gla_in_kernel(x_ref, sh_ref, sc_ref, ng_ref, win_ref, wga_ref, wgb_ref, bg_ref, *out_refs,
                   kdim, vdim, qscale, state_only):
    hb = _norm_modulate(x_ref[...], ng_ref[...], sc_ref[...], sh_ref[...]).astype(BF16)
    if state_only:
        k_ref, v_ref, g_ref = out_refs
        proj = _dot(hb, win_ref[:, kdim:2 * kdim + vdim])
        k_ref[...] = proj[:, :kdim].astype(BF16)
        v_ref[...] = proj[:, kdim:].astype(BF16)
    else:
        q_ref, k_ref, v_ref, r_ref, g_ref = out_refs
        proj = _dot(hb, win_ref[...])
        q_ref[...] = (proj[:, :kdim] * qscale).astype(BF16)
        k_ref[...] = proj[:, kdim:2 * kdim].astype(BF16)
        v_ref[...] = proj[:, 2 * kdim:2 * kdim + vdim].astype(BF16)
        r_ref[...] = proj[:, 2 * kdim + vdim:].astype(BF16)
    low = _dot(hb, wga_ref[...])
    z = _dot(low.astype(BF16), wgb_ref[...]) + bg_ref[...]
    logsig = jnp.minimum(z, 0.0) - jnp.log(1.0 + jnp.exp(-jnp.abs(z)))
    g_ref[...] = logsig * (1.0 / GLA_GATE_NORM)


def _gla_in(x, mods4, layer, row_of_batch, ng, win, wga, wgb, bg, tm, state_only=False):
    bsz, t, d = x.shape
    kdim, vdim = d // 2, d
    tm = min(tm, t)
    tok = lambda w: pl.BlockSpec((None, tm, w), lambda b, i: (b, i, 0))
    full = lambda a: pl.BlockSpec(a.shape, lambda b, i: (0,) * a.ndim)
    kern = functools.partial(_gla_in_kernel, kdim=kdim, vdim=vdim,
                             qscale=float((kdim // GLA_HEADS) ** -0.5), state_only=state_only)
    widths = (kdim, vdim, 2 * kdim) if state_only else (kdim, kdim, vdim, vdim, 2 * kdim)
    dtypes = (BF16,) * (len(widths) - 1) + (F32,)
    return pl.pallas_call(
        kern,
        out_shape=tuple(jax.ShapeDtypeStruct((bsz, t, w), dt) for w, dt in zip(widths, dtypes)),
        grid=(bsz, t // tm),
        in_specs=[
            tok(d),
            _mod_spec(layer, 0, d, row_of_batch),
            _mod_spec(layer, 1, d, row_of_batch),
            full(ng), full(win), full(wga), full(wgb), full(bg),
        ],
        out_specs=tuple(tok(w) for w in widths),
        compiler_params=_cparams(("arbitrary", "arbitrary")),
        name="gla_in",
    )(x, mods4, mods4, ng, win, wga, wgb, bg)


def _block_masks(rows, chunk):
    i = np.arange(rows)
    same = (i[:, None] // chunk) == (i[None, :] // chunk)
    lower = same & (i[:, None] >= i[None, :])
    upper = same & (i[:, None] <= i[None, :])
    return jnp.asarray(lower, BF16), jnp.asarray(upper, BF16)


def _per_chunk_row(a, chunk, r):
    rows, w = a.shape
    parts = [jnp.broadcast_to(a[c * chunk + r:c * chunk + r + 1, :], (chunk, w))
             for c in range(rows // chunk)]
    return jnp.concatenate(parts, axis=0)


def _block_terms(q, k, v, g2, lo, up, chunk, want_out):
    dk = g2.shape[1] // 2
    gh, gl = _split(g2)
    pre = _dot(lo, gh) + _dot(lo, gl)
    tot = _per_chunk_row(pre, chunk, chunk - 1)
    fwd = lax.broadcasted_iota(jnp.int32, g2.shape, 1) < dk
    cum = jnp.where(fwd, pre, tot - pre + g2)
    kf = k.astype(F32)
    kf2 = jnp.concatenate([kf, kf], axis=1)
    kl = (kf2 * jnp.exp(tot - cum)).astype(BF16)
    etot = jnp.exp(tot)
    if not want_out:
        return kl, etot, None, None
    mid = _per_chunk_row(cum, chunk, chunk // 2)
    qf = q.astype(F32)
    qf2 = jnp.concatenate([qf, qf], axis=1)
    qe = (qf2 * jnp.exp(cum)).astype(BF16)
    qi = (qf2 * jnp.exp(cum - mid)).astype(BF16)
    ki = (kf2 * jnp.exp(mid - cum)).astype(BF16)
    s = (jnp.where(lo > 0, _dot(qi[:, :dk], ki[:, :dk], _NT), 0.0)
         + jnp.where(up > 0, _dot(qi[:, dk:], ki[:, dk:], _NT), 0.0))
    return kl, etot, qe, _dot(s.astype(BF16), v)


def _store_chunk_states(v, kl, etot, ds_ref, e_ref, first_chunk, chunk):
    for c in range(v.shape[0] // chunk):
        rows = slice(c * chunk, (c + 1) * chunk)
        ds_ref[first_chunk + c] = _dot(v[rows, :], kl[rows, :], _TN)
        e_ref[first_chunk + c] = etot[c * chunk:c * chunk + SUBLANES, :]


def _scan_states(s0, ds_ref, e_ref, sc_ref, n, dk, first=0):
    def body(c, carry):
        sf, sb = carry
        r = n - 1 - c
        if sc_ref is not None:
            sc_ref[first + c, :, :dk] = sf.astype(BF16)
            sc_ref[first + r, :, dk:] = sb.astype(BF16)
        return (sf * e_ref[first + c, 0:1, :dk] + ds_ref[first + c, :, :dk],
                sb * e_ref[first + r, 0:1, dk:] + ds_ref[first + r, :, dk:])

    return lax.fori_loop(0, n, body, (s0[:, :dk], s0[:, dk:]))


def _gla_state_kernel(k_ref, v_ref, g_ref, lo_ref, up_ref, s_ref, ds_ref, e_ref, *, chunk, block):
    nh, dv = s_ref.shape[0], s_ref.shape[1]
    t = k_ref.shape[0]
    dk = k_ref.shape[1] // nh
    n = t // chunk
    per = block // chunk

    def terms(i, carry):
        rows = pl.ds(pl.multiple_of(i * block, block), block)
        for h in range(nh):
            v = v_ref[rows, h * dv:(h + 1) * dv]
            kl, etot, _, _ = _block_terms(None, k_ref[rows, h * dk:(h + 1) * dk], v,
                                          g_ref[rows, 2 * h * dk:2 * (h + 1) * dk],
                                          lo_ref[...], up_ref[...], chunk, False)
            _store_chunk_states(v, kl, etot, ds_ref, e_ref, h * n + i * per, chunk)
        return carry

    lax.fori_loop(0, t // block, terms, 0)
    for h in range(nh):
        sf, sb = _scan_states(jnp.zeros((dv, 2 * dk), F32), ds_ref, e_ref, None, n, dk, first=h * n)
        s_ref[h, :, :dk] = sf
        s_ref[h, :, dk:] = sb


def _gla_scan_kernel(q_ref, k_ref, v_ref, g_ref, s0_ref, ng_ref, lo_ref, up_ref,
                     o_ref, oi_ref, qe_ref, ds_ref, e_ref, sc_ref, *, chunk, block):
    t = q_ref.shape[0]
    dk = q_ref.shape[1]
    per = block // chunk

    group = GLA_BLOCKS_PER_TRIP if (t // block) % GLA_BLOCKS_PER_TRIP == 0 else 1

    def terms(i, carry):
        for u in range(group):
            blk = i * group + u
            rows = pl.ds(pl.multiple_of(blk * block, block), block)
            v = v_ref[rows, :]
            kl, etot, qe, oi = _block_terms(q_ref[rows, :], k_ref[rows, :], v, g_ref[rows, :],
                                            lo_ref[...], up_ref[...], chunk, True)
            _store_chunk_states(v, kl, etot, ds_ref, e_ref, blk * per, chunk)
            qe_ref[rows, :] = qe
            oi_ref[rows, :] = oi
        return carry

    lax.fori_loop(0, t // (block * group), terms, 0)
    _scan_states(s0_ref[...], ds_ref, e_ref, sc_ref, t // chunk, dk)

    fgroup = GLA_FINISH_BLOCKS_PER_TRIP if (t // block) % GLA_FINISH_BLOCKS_PER_TRIP == 0 else 1

    def finish(i, carry):
        for u in range(fgroup):
            blk = i * fgroup + u
            rows = pl.ds(pl.multiple_of(blk * block, block), block)
            inter = [_dot(qe_ref[pl.ds(pl.multiple_of(blk * block + c * chunk, chunk), chunk), :],
                          sc_ref[blk * per + c], _NT) for c in range(per)]
            o = oi_ref[rows, :] + jnp.concatenate(inter, axis=0)
            o_ref[rows, :] = _rmsnorm(o, ng_ref[...]).astype(o_ref.dtype)
        return carry

    lax.fori_loop(0, t // (block * fgroup), finish, 0)


def _gla_state(k, v, g, chunk, block):
    bsz, t, kdim = k.shape
    vdim = v.shape[-1]
    nh = GLA_HEADS
    dk, dv = kdim // nh, vdim // nh
    block = min(block, t)
    n = t // chunk
    lo, up = _block_masks(block, chunk)
    mask_spec = pl.BlockSpec((block, block), lambda b: (0, 0))
    seq = lambda w: pl.BlockSpec((None, t, w), lambda b: (b, 0, 0))
    return pl.pallas_call(
        functools.partial(_gla_state_kernel, chunk=chunk, block=block),
        out_shape=jax.ShapeDtypeStruct((bsz, nh, dv, 2 * dk), F32),
        grid=(bsz,),
        in_specs=[seq(kdim), seq(vdim), seq(2 * kdim), mask_spec, mask_spec],
        out_specs=pl.BlockSpec((None, nh, dv, 2 * dk), lambda b: (b, 0, 0, 0)),
        scratch_shapes=[pltpu.VMEM((nh * n, dv, 2 * dk), F32), pltpu.VMEM((nh * n, SUBLANES, 2 * dk), F32)],
        compiler_params=_cparams(("arbitrary",)),
        name="gla_state",
    )(k, v, g, lo, up)


def _gla_scan(q, k, v, g, s0, norm_g, chunk, block):
    bsz, t, kdim = k.shape
    vdim = v.shape[-1]
    nh = GLA_HEADS
    dk, dv = kdim // nh, vdim // nh
    block = min(block, t)
    n = t // chunk
    lo, up = _block_masks(block, chunk)
    kspec = pl.BlockSpec((None, t, dk), lambda b, h: (b, 0, h))
    vspec = pl.BlockSpec((None, t, dv), lambda b, h: (b, 0, h))
    mask_spec = pl.BlockSpec((block, block), lambda b, h: (0, 0))
    return pl.pallas_call(
        functools.partial(_gla_scan_kernel, chunk=chunk, block=block),
        out_shape=jax.ShapeDtypeStruct((bsz, t, vdim), BF16),
        grid=(bsz, nh),
        in_specs=[
            kspec, kspec, vspec,
            pl.BlockSpec((None, t, 2 * dk), lambda b, h: (b, 0, h)),
            pl.BlockSpec((None, None, dv, 2 * dk), lambda b, h: (b, h, 0, 0)),
            pl.BlockSpec((1, dv), lambda b, h: (0, 0)),
            mask_spec, mask_spec,
        ],
        out_specs=vspec,
        scratch_shapes=[
            pltpu.VMEM((t, dv), F32),
            pltpu.VMEM((t, 2 * dk), BF16),
            pltpu.VMEM((n, dv, 2 * dk), F32),
            pltpu.VMEM((n, SUBLANES, 2 * dk), F32),
            pltpu.VMEM((n, dv, 2 * dk), BF16),
        ],
        compiler_params=_cparams(("arbitrary", "arbitrary")),
        name="gla_scan",
    )(q, k, v, g, s0, norm_g, lo, up)


def _route(h2, wr_t, br, earlier_bf, count_ref):
    tm = h2.shape[0]
    logits = _dot3(wr_t, h2, _NT)
    scores = _sigmoid(logits)
    sel = scores + br
    row = [sel[e:e + 1, :] for e in range(N_EXPERTS)]
    picked = []
    for e in range(N_EXPERTS):
        g0 = (e // EXPERTS_PER_GROUP) * EXPERTS_PER_GROUP
        ahead_count = jnp.zeros(row[e].shape, jnp.int32)
        for j in range(g0, g0 + EXPERTS_PER_GROUP):
            if j == e:
                continue
            ahead = (row[j] >= row[e]) if j < e else (row[j] > row[e])
            ahead_count = ahead_count + ahead.astype(jnp.int32)
        picked.append(ahead_count < TOP_K)
    zero = jnp.zeros_like(row[0])
    gscore, pair, first_w, second_w = [], [], [], []
    for g in range(N_EXPERT_GROUPS):
        acc, pidx, fw, sw = zero, zero, zero, zero
        seen = None
        for a in range(EXPERTS_PER_GROUP):
            e = g * EXPERTS_PER_GROUP + a
            sc = scores[e:e + 1, :]
            acc = acc + jnp.where(picked[e], row[e], 0.0)
            if seen is None:
                is_first = picked[e]
            else:
                is_first = picked[e] & jnp.logical_not(seen)
                is_second = picked[e] & seen
                sw = sw + jnp.where(is_second, sc, 0.0)
                pidx = pidx + jnp.where(is_second, float(a), 0.0)
            fw = fw + jnp.where(is_first, sc, 0.0)
            if a < len(_PAIR_BASE):
                pidx = pidx + jnp.where(is_first, float(_PAIR_BASE[a] - a - 1), 0.0)
            seen = picked[e] if seen is None else (seen | picked[e])
        gscore.append(acc)
        pair.append(pidx)
        first_w.append(fw)
        second_w.append(sw)
    cls, wa, wb = zero, zero, zero
    for g in range(N_EXPERT_GROUPS):
        ok = None
        for j in range(N_EXPERT_GROUPS):
            if j == g:
                continue
            c = (gscore[g] > gscore[j]) if j < g else (gscore[g] >= gscore[j])
            ok = c if ok is None else (ok & c)
        cls = cls + jnp.where(ok, pair[g] + float(len(_PAIRS) * g), 0.0)
        wa = wa + jnp.where(ok, first_w[g], 0.0)
        wb = wb + jnp.where(ok, second_w[g], 0.0)
    denom = wa + wb
    wa = wa / denom
    wb = wb / denom

    cid = lax.broadcasted_iota(jnp.int32, (CLASS_ROWS, tm), 0).astype(F32)
    onehot = (cid == cls).astype(BF16)
    before = _dot(onehot, earlier_bf)
    oh = onehot.astype(F32)
    base = count_ref[...][:, 0:1]
    rank = jnp.sum(oh * (before + base), axis=0, keepdims=True)
    count_ref[...] = count_ref[...] + jnp.sum(oh, axis=1, keepdims=True)
    pad = jnp.zeros((INFO_ROWS - 4, tm), F32)
    return jnp.concatenate([cls, rank, wa, wb, pad], axis=0)


def _earlier_matrix(tm):
    i = np.arange(tm)
    return jnp.asarray(i[:, None] < i[None, :], BF16)


def _store_tiled_rows(ref, x):
    tm, d = x.shape
    per = d // LANES
    for c in range(per):
        ref[pl.ds(c, tm, stride=per), :] = x[:, c * LANES:(c + 1) * LANES]


def _load_tiled_rows(ref, tm):
    per = ref.shape[0] // tm
    return jnp.concatenate([ref[pl.ds(c, tm, stride=per), :] for c in range(per)], axis=1)


def _route_outputs(h2, wr_ref, br_ref, earlier_ref, count_ref, pay_ref, info_ref, counts_ref):
    first = (pl.program_id(0) == 0) & (pl.program_id(1) == 0)

    @pl.when(first)
    def _():
        count_ref[...] = jnp.zeros_like(count_ref)

    info = _route(h2, wr_ref[...], br_ref[...], earlier_ref[...], count_ref)
    _store_tiled_rows(pay_ref, h2)
    info_ref[...] = info
    counts_ref[...] = count_ref[...]


def _route_out_shapes(bsz, t, d):
    return (
        jax.ShapeDtypeStruct((bsz * t * (d // LANES), LANES), F32),
        jax.ShapeDtypeStruct((INFO_ROWS, bsz * t), F32),
        jax.ShapeDtypeStruct((CLASS_ROWS, LANES), F32),
    )


def _route_out_specs(tm, nt, d):
    return (
        pl.BlockSpec((tm * (d // LANES), LANES), lambda b, i, *_: (b * nt + i, 0)),
        pl.BlockSpec((INFO_ROWS, tm), lambda b, i, *_: (0, b * nt + i)),
        pl.BlockSpec((CLASS_ROWS, LANES), lambda b, i, *_: (0, 0)),
    )


def _plan(info, counts, n, tmoe):
    cls = info[0].astype(jnp.int32)
    rank = info[1].astype(jnp.int32)
    cnt = counts[:N_CLASSES, 0].astype(jnp.int32)
    padded = (cnt + tmoe - 1) // tmoe * tmoe
    ends = jnp.cumsum(padded)
    dest = (ends - padded)[cls] + rank
    n_tiles = n // tmoe + N_CLASSES
    rows = jnp.stack([jnp.arange(n, dtype=F32), info[2], info[3], jnp.zeros((n,), F32)], axis=1)
    srt = jnp.zeros((n_tiles * tmoe, 4), F32).at[dest].set(rows, unique_indices=True)
    tok = srt[:, 0].astype(jnp.int32)
    n_used = ends[-1] // tmoe
    tile = jnp.minimum(jnp.arange(n_tiles, dtype=jnp.int32), n_used - 1)
    tcls = jnp.sum((ends[None, :] <= (tile * tmoe)[:, None]).astype(jnp.int32), axis=1)
    group, pair = tcls // len(_PAIRS), tcls % len(_PAIRS)
    pa = jnp.asarray([p[0] for p in _PAIRS], jnp.int32)[pair]
    pb = jnp.asarray([p[1] for p in _PAIRS], jnp.int32)[pair]
    ea = group * EXPERTS_PER_GROUP + pa
    eb = group * EXPERTS_PER_GROUP + pb
    return dest, tok, srt, ea, eb, n_used.reshape(1).astype(jnp.int32)


def _post0_kernel(o_ref, r_ref, x_ref, g1_ref, sh2_ref, sc2_ref, n2_ref, wout_ref, wr_ref, br_ref,
                  earlier_ref, x1_ref, pay_ref, info_ref, counts_ref, count_ref):
    a = (o_ref[...].astype(F32) * _silu(r_ref[...].astype(F32))).astype(BF16)
    x1 = x_ref[...] + g1_ref[...] * _dot(a, wout_ref[...])
    x1_ref[...] = x1
    h2 = _norm_modulate(x1, n2_ref[...], sc2_ref[...], sh2_ref[...])
    _route_outputs(h2, wr_ref, br_ref, earlier_ref, count_ref, pay_ref, info_ref, counts_ref)


def _post0(o, r, x, mods4, layer, n2, wout, wr_t, br, tm):
    bsz, t, d = x.shape
    tm = min(tm, t)
    rb = lambda b: b
    tok = lambda w: pl.BlockSpec((None, tm, w), lambda b, i: (b, i, 0))
    full = lambda a: pl.BlockSpec(a.shape, lambda b, i: (0,) * a.ndim)
    nt = t // tm
    earlier = _earlier_matrix(tm)
    return pl.pallas_call(
        _post0_kernel,
        out_shape=(jax.ShapeDtypeStruct((bsz, t, d), F32),) + _route_out_shapes(bsz, t, d),
        grid=(bsz, nt),
        in_specs=[
            tok(d), tok(d), tok(d),
            _mod_spec(layer, 2, d, rb), _mod_spec(layer, 3, d, rb), _mod_spec(layer, 4, d, rb),
            full(n2), full(wout), full(wr_t), full(br), full(earlier),
        ],
        out_specs=(tok(d),) + _route_out_specs(tm, nt, d),
        scratch_shapes=[pltpu.VMEM((CLASS_ROWS, LANES), F32)],
        compiler_params=_cparams(("arbitrary", "arbitrary")),
        name="post0",
    )(o, r, x, mods4, mods4, mods4, n2, wout, wr_t, br, earlier)


def _row_group(i, per):
    return pl.ds(pl.multiple_of(i * per, per), per)


def _moe_kernel(tok_ref, ea_ref, eb_ref, used_ref, p_hbm, gw_ref, wa_ref, wb_ref, f_ref,
                pbuf0, pbuf1, gsem, *, d, tm):
    del ea_ref, eb_ref
    j = pl.program_id(0)
    n_tiles = pl.num_programs(0)
    used = used_ref[0]
    per = d // LANES
    pbufs = (pbuf0, pbuf1)

    def gather_row(tile, r, p, pred, lane):
        i = tok_ref[tile * tm + r]

        @pl.when(pred)
        def _():
            pltpu.make_async_copy(p_hbm.at[_row_group(i, per), :], pbufs[p].at[_row_group(r, per), :],
                                  gsem.at[p]).start(priority=lane)

    def gather_wait(p):
        pltpu.make_async_copy(p_hbm.at[pl.ds(0, tm * per), :], pbufs[p], gsem.at[p]).wait()

    @pl.when(j == 0)
    def _():
        def body(blk, carry):
            for u in range(DMA_UNROLL):
                gather_row(0, blk * DMA_UNROLL + u, 0, True, u % 2)
            return carry
        lax.fori_loop(0, tm // DMA_UNROLL, body, 0)

    def step(p, beside_matmuls):
        @pl.when(j < used)
        def _():
            has_next = j + 1 < used
            nxt = jnp.minimum(j + 1, n_tiles - 1)
            if beside_matmuls:
                gather_wait(p)
                for r in range(tm):
                    gather_row(nxt, r, 1 - p, has_next, r % 2)
            else:
                @pl.when(j >= 0)
                def _():
                    for r in range(tm):
                        gather_row(nxt, r, 1 - p, has_next, r % 2)
                gather_wait(p)
            h = _load_tiled_rows(pbufs[p], tm).astype(BF16)
            gw = gw_ref[...]

            def expert(w_ref):
                de = w_ref.shape[2]
                he = (_silu(_dot(h, w_ref[0])) * _dot(h, w_ref[1])).astype(BF16)
                return jnp.concatenate([_dot(he, w_ref[2, :de, :]), _dot(he, w_ref[2, de:, :])], axis=1)

            _store_tiled_rows(f_ref, gw[:, 1:2] * expert(wa_ref) + gw[:, 2:3] * expert(wb_ref))

    @pl.when(j % 2 == 0)
    def _():
        step(0, True)

    @pl.when(j % 2 == 1)
    def _():
        step(1, False)

    @pl.when(j >= used)
    def _():
        f_ref[...] = jnp.zeros(f_ref.shape, F32)


def _moe(payload, tok, srt, ea, eb, n_used, w_all, layer, tm):
    n_sorted = tok.shape[0]
    n_tiles = n_sorted // tm
    _, _, _, d, de = w_all.shape
    per = d // LANES
    wa = lambda j, tok, ea, eb, used: (layer, ea[j], 0, 0, 0)
    wb = lambda j, tok, ea, eb, used: (layer, eb[j], 0, 0, 0)
    return pl.pallas_call(
        functools.partial(_moe_kernel, d=d, tm=tm),
        out_shape=jax.ShapeDtypeStruct((n_sorted * per, LANES), F32),
        grid_spec=pltpu.PrefetchScalarGridSpec(
            num_scalar_prefetch=4,
            grid=(n_tiles,),
            in_specs=[
                pl.BlockSpec(memory_space=pl.ANY),
                pl.BlockSpec((tm, srt.shape[1]), lambda j, *_: (j, 0)),
                pl.BlockSpec((None, None, 3, d, de), wa), pl.BlockSpec((None, None, 3, d, de), wb),
            ],
            out_specs=pl.BlockSpec((tm * per, LANES), lambda j, *_: (j, 0)),
            scratch_shapes=[pltpu.VMEM((tm * per, LANES), F32), pltpu.VMEM((tm * per, LANES), F32),
                            pltpu.SemaphoreType.DMA((2,))],
        ),
        compiler_params=_cparams(("arbitrary",)),
        name="moe",
    )(tok, ea, eb, n_used, payload, srt, w_all, w_all)


def _expert_weights_kernel(w1_ref, w3_ref, w2_ref, o_ref):
    de = w2_ref.shape[0]
    o_ref[0] = w1_ref[...].astype(BF16)
    o_ref[1] = w3_ref[...].astype(BF16)
    o_ref[2, :de, :] = w2_ref[:, :de].astype(BF16)
    o_ref[2, de:, :] = w2_ref[:, de:].astype(BF16)


def _expert_weights(w1, w3, w2):
    nl, ne, d, de = w1.shape
    assert d == 2 * de
    up = pl.BlockSpec((None, None, d, de), lambda l, e: (l, e, 0, 0))
    return pl.pallas_call(
        _expert_weights_kernel,
        out_shape=jax.ShapeDtypeStruct((nl, ne, 3, d, de), BF16),
        grid=(nl, ne),
        in_specs=[up, up, pl.BlockSpec((None, None, de, d), lambda l, e: (l, e, 0, 0))],
        out_specs=pl.BlockSpec((None, None, 3, d, de), lambda l, e: (l, e, 0, 0, 0)),
        compiler_params=_cparams(("arbitrary", "arbitrary")),
        name="expert_weights",
    )(w1, w3, w2)


def _sparse_moe(payload, info, counts, w_all, layer):
    n = info.shape[1]
    tmoe = min(TM_MOE, n)
    dest, tok, srt, ea, eb, n_used = _plan(info, counts, n, tmoe)
    return _moe(payload, tok, srt, ea, eb, n_used, w_all, layer, tmoe), dest


def _gather_rows_start(dest_ref, f_hbm, fbuf, fsem, tile, tm, per, pred, unrolled):
    def one(r, lane):
        i = dest_ref[tile * tm + r]

        @pl.when(pred)
        def _():
            pltpu.make_async_copy(f_hbm.at[_row_group(i, per), :], fbuf.at[_row_group(r, per), :],
                                  fsem).start(priority=lane)

    if unrolled:
        for r in range(tm):
            one(r, r % 2)
    else:
        def body(blk, carry):
            for u in range(DMA_UNROLL):
                one(blk * DMA_UNROLL + u, u % 2)
            return carry
        lax.fori_loop(0, tm // DMA_UNROLL, body, 0)


def _gather_rows_wait(f_hbm, fbuf, fsem):
    pltpu.make_async_copy(f_hbm.at[pl.ds(0, fbuf.shape[0]), :], fbuf, fsem).wait()


def _pool_constants(tm, d):
    ng = len(POOL_WINDOWS)
    cg = d // ng
    pos = np.arange(tm)
    seg, off = pos // GRID_W, pos % GRID_W
    mats = np.zeros((ng, tm, tm), np.float32)
    inv = np.zeros((tm, d), np.float32)
    for gi, w in enumerate(POOL_WINDOWS):
        lo = np.clip(off - w // 2, 0, GRID_W)
        hi = np.clip(off - w // 2 + w, 0, GRID_W)
        same = seg[:, None] == seg[None, :]
        inside = (off[None, :] >= lo[:, None]) & (off[None, :] < hi[:, None])
        mats[gi] = (same & inside).astype(np.float32)
        inv[:, gi * cg:(gi + 1) * cg] = (1.0 / (hi - lo).astype(np.float32))[:, None]
    return jnp.asarray(mats, BF16), jnp.asarray(inv, F32)


def _layer1_kernel(dest_ref, x_ref, f_hbm, g2p_ref, sh1_ref, sc1_ref, g1_ref, sh2_ref, sc2_ref,
                   n1_ref, n2_ref, pm_ref, inv_ref, wp_ref, bp_ref, ps_ref, wr_ref, br_ref, earlier_ref,
                   x3_ref, pay_ref, info_ref, counts_ref, fbuf, fsem, x2_ref, count_ref):
    tm, d = x_ref.shape
    per = d // LANES
    nt = pl.num_programs(1)
    lin = pl.program_id(0) * nt + pl.program_id(1)

    @pl.when(lin == 0)
    def _():
        _gather_rows_start(dest_ref, f_hbm, fbuf, fsem, 0, tm, per, True, False)

    @pl.when(lin >= 0)
    def _():
        _gather_rows_wait(f_hbm, fbuf, fsem)
        x2_ref[...] = x_ref[...] + g2p_ref[...] * _load_tiled_rows(fbuf, tm)

    total = pl.num_programs(0) * nt
    _gather_rows_start(dest_ref, f_hbm, fbuf, fsem, jnp.minimum(lin + 1, total - 1), tm, per,
                       lin + 1 < total, True)
    x2 = x2_ref[...]
    h = _norm_modulate(x2, n1_ref[...], sc1_ref[...], sh1_ref[...])
    hb = h.astype(BF16)
    ng = pm_ref.shape[0]
    cg = h.shape[1] // ng
    ys = []
    for gi in range(ng):
        cs = slice(gi * cg, (gi + 1) * cg)
        wsum = _dot(pm_ref[gi], hb[:, cs])
        pooled = wsum * inv_ref[:, cs] - h[:, cs]
        ys.append(_dot(pooled.astype(BF16), wp_ref[gi]))
    y = (jnp.concatenate(ys, axis=1) + bp_ref[...]) * ps_ref[...]
    x3 = x2 + g1_ref[...] * y
    x3_ref[...] = x3
    h2 = _norm_modulate(x3, n2_ref[...], sc2_ref[...], sh2_ref[...])
    _route_outputs(h2, wr_ref, br_ref, earlier_ref, count_ref, pay_ref, info_ref, counts_ref)


def _layer1(x1, f_sorted, dest, mods4, n1, n2, wp, bp, ps, wr_t, br, tm):
    bsz, t, d = x1.shape
    tm = min(tm, t)
    pm, inv = _pool_constants(tm, d)
    earlier = _earlier_matrix(tm)
    rb = lambda b: b
    tok = lambda w: pl.BlockSpec((None, tm, w), lambda b, i, dest: (b, i, 0))
    full = lambda a: pl.BlockSpec(a.shape, lambda b, i, dest: (0,) * a.ndim)
    nt = t // tm
    return pl.pallas_call(
        _layer1_kernel,
        out_shape=(jax.ShapeDtypeStruct((bsz, t, d), F32),) + _route_out_shapes(bsz, t, d),
        grid_spec=pltpu.PrefetchScalarGridSpec(
            num_scalar_prefetch=1,
            grid=(bsz, nt),
            in_specs=[
                tok(d), pl.BlockSpec(memory_space=pl.ANY),
                _mod_spec(0, 5, d, rb),
                _mod_spec(1, 0, d, rb), _mod_spec(1, 1, d, rb), _mod_spec(1, 2, d, rb),
                _mod_spec(1, 3, d, rb), _mod_spec(1, 4, d, rb),
                full(n1), full(n2), full(pm), full(inv), full(wp), full(bp), full(ps),
                full(wr_t), full(br), full(earlier),
            ],
            out_specs=(tok(d),) + _route_out_specs(tm, nt, d),
            scratch_shapes=[pltpu.VMEM((tm * (d // LANES), LANES), F32), pltpu.SemaphoreType.DMA(()),
                            pltpu.VMEM((tm, d), F32), pltpu.VMEM((CLASS_ROWS, LANES), F32)],
        ),
        compiler_params=_cparams(("arbitrary", "arbitrary")),
        name="layer1",
    )(dest, x1, f_sorted, mods4, mods4, mods4, mods4, mods4, mods4, n1, n2, pm, inv, wp, bp, ps,
      wr_t, br, earlier)


def _final_kernel(dest_ref, x_ref, f_hbm, g2_ref, fg_ref, o_ref, fbuf, fsem):
    tm, d = x_ref.shape
    per = d // LANES
    nt = pl.num_programs(1)
    lin = pl.program_id(0) * nt + pl.program_id(1)

    @pl.when(lin == 0)
    def _():
        _gather_rows_start(dest_ref, f_hbm, fbuf, fsem, 0, tm, per, True, False)

    @pl.when(lin >= 0)
    def _():
        _gather_rows_wait(f_hbm, fbuf, fsem)
        f = _load_tiled_rows(fbuf, tm)
        o_ref[...] = _rmsnorm(x_ref[...] + g2_ref[...] * f, fg_ref[...])

    total = pl.num_programs(0) * nt
    _gather_rows_start(dest_ref, f_hbm, fbuf, fsem, jnp.minimum(lin + 1, total - 1), tm, per,
                       lin + 1 < total, False)


def _final(x3, f_sorted, dest, mods4, fg, tm):
    bsz, t, d = x3.shape
    tm = min(tm, t)
    tok = pl.BlockSpec((None, tm, d), lambda b, i, dest: (b, i, 0))
    return pl.pallas_call(
        _final_kernel,
        out_shape=jax.ShapeDtypeStruct((bsz, t, d), F32),
        grid_spec=pltpu.PrefetchScalarGridSpec(
            num_scalar_prefetch=1,
            grid=(bsz, t // tm),
            in_specs=[tok, pl.BlockSpec(memory_space=pl.ANY), _mod_spec(1, 5, d, lambda b: b),
                      pl.BlockSpec((1, d), lambda b, i, dest: (0, 0))],
            out_specs=tok,
            scratch_shapes=[pltpu.VMEM((tm * (d // LANES), LANES), F32), pltpu.SemaphoreType.DMA(())],
        ),
        compiler_params=_cparams(("arbitrary", "arbitrary")),
        name="final",
    )(dest, x3, f_sorted, mods4, fg)


def kernel(x, c, ctx, c_ctx, norm1_g, norm2_g, w_mod, b_mod, gla_w_in, gla_w_gate_a, gla_w_gate_b,
           gla_b_gate, gla_norm_g, gla_w_out, pool_w, pool_b, pool_scale, w_router, b_router,
           w_gate_e, w_up_e, w_down_e, final_g):
    bsz, t, d = x.shape
    depth = w_mod.shape[0]
    assert depth == 2 and t % GRID_W == 0
    kdim = d // 2
    row = lambda a: a.reshape(1, -1)

    c_rows = -(-(bsz + 1) // 8) * 8
    c_all = jnp.zeros((c_rows, d), F32).at[:bsz].set(c).at[bsz].set(c_ctx)
    mods = _modulation(c_all, w_mod, b_mod)
    mods4 = mods.reshape(depth, c_rows, 1, 6 * d)

    win = gla_w_in[0].astype(BF16)
    wga = jnp.concatenate([gla_w_gate_a[0, 0], gla_w_gate_a[0, 1]], axis=1).astype(BF16)
    dk = kdim // GLA_HEADS
    by_head = lambda a: a.reshape(a.shape[0], GLA_HEADS, 1, dk)
    zero = jnp.zeros((GLA_GATE_RANK, GLA_HEADS, 1, dk), F32)
    wgb = jnp.concatenate([
        jnp.concatenate([by_head(gla_w_gate_b[0, 0]), zero], axis=2),
        jnp.concatenate([zero, by_head(gla_w_gate_b[0, 1])], axis=2)], axis=0)
    wgb = wgb.reshape(2 * GLA_GATE_RANK, 2 * kdim).astype(BF16)
    bg = jnp.stack([gla_b_gate[0, 0].reshape(GLA_HEADS, dk), gla_b_gate[0, 1].reshape(GLA_HEADS, dk)],
                   axis=1).reshape(1, 2 * kdim)
    n1_0 = row(norm1_g[0])
    q, k, v, r, g = _gla_in(x, mods4, 0, lambda b: b, n1_0, win, wga, wgb, bg, TM_PROJ)
    kc, vc, gc = _gla_in(ctx, mods4, 0, lambda b: bsz, n1_0, win, wga, wgb, bg, TM_CTX, state_only=True)
    s0 = _gla_state(kc, vc, gc, GLA_CHUNK, GLA_BLOCK)
    o = _gla_scan(q, k, v, g, s0, row(gla_norm_g[0]), GLA_CHUNK, GLA_BLOCK)

    wr_t = jnp.transpose(w_router)
    br = b_router.reshape(N_EXPERTS, 1)
    x1, payload, info, counts = _post0(o, r, x, mods4, 0, row(norm2_g[0]),
                                       gla_w_out[0].astype(BF16), wr_t, br, TM_POST)
    w_experts = _expert_weights(w_gate_e, w_up_e, w_down_e)
    f0, dest0 = _sparse_moe(payload, info, counts, w_experts, 0)

    x3, payload, info, counts = _layer1(x1, f0, dest0, mods4, row(norm1_g[1]), row(norm2_g[1]),
                                        pool_w[0].astype(BF16), row(pool_b[0]), row(pool_scale[0]),
                                        wr_t, br, TM_POOL)
    f1, dest1 = _sparse_moe(payload, info, counts, w_experts, 1)
    return _final(x3, f1, dest1, mods4, row(final_g), TM_FINAL)
```

```python
import functools

import numpy as np
import jax
import jax.numpy as jnp
from jax import lax
from jax.experimental import pallas as pl
from jax.experimental.pallas import tpu as pltpu

EPS = 1e-6
GRID_W = 64
GLA_HEADS = 4
GLA_GATE_RANK = 16
GLA_GATE_NORM = 16.0
GLA_CHUNK = 128
GLA_BLOCK = 256
GLA_BLOCKS_PER_TRIP = 8
GLA_FINISH_BLOCKS_PER_TRIP = 8
POOL_WINDOWS = (2, 4, 8, 16)
N_EXPERTS = 16
N_EXPERT_GROUPS = 4
EXPERTS_PER_GROUP = N_EXPERTS // N_EXPERT_GROUPS
TOP_K = 2

_PAIRS = tuple((a, b) for a in range(EXPERTS_PER_GROUP) for b in range(a + 1, EXPERTS_PER_GROUP))
_PAIR_BASE = (0, 3, 5)
N_CLASSES = N_EXPERT_GROUPS * len(_PAIRS)
CLASS_ROWS = 32
INFO_ROWS = 8
LANES = 128
SUBLANES = 8

TM_PROJ = 1024
TM_CTX = 256
TM_POST = 1024
TM_POOL = 512
TM_MOE = 256
TM_FINAL = 1024
DMA_UNROLL = 8

F32 = jnp.float32
BF16 = jnp.bfloat16

_NT = (((1,), (1,)), ((), ()))
_TN = (((0,), (0,)), ((), ()))
_VMEM_LIMIT = 56 * 1024 * 1024


def _cparams(sem):
    return pltpu.CompilerParams(dimension_semantics=sem, vmem_limit_bytes=_VMEM_LIMIT)


def _dot(a, b, dims=None):
    if dims is None:
        return jnp.dot(a, b, preferred_element_type=F32)
    return lax.dot_general(a, b, dims, preferred_element_type=F32)


def _split(a):
    hi = a.astype(BF16)
    lo = (a - hi.astype(F32)).astype(BF16)
    return hi, lo


def _dot3(a, b, dims=None):
    ah, al = _split(a)
    bh, bl = _split(b)
    return _dot(ah, bh, dims) + _dot(ah, bl, dims) + _dot(al, bh, dims)


def _sigmoid(x):
    return 1.0 / (1.0 + jnp.exp(-x))


def _silu(x):
    return x * _sigmoid(x)


def _rmsnorm(xf, g):
    return xf * lax.rsqrt(jnp.mean(xf * xf, axis=-1, keepdims=True) + EPS) * g


def _norm_modulate(xf, g, scale, shift):
    return xf * lax.rsqrt(jnp.mean(xf * xf, axis=-1, keepdims=True) + EPS) * (g * (1.0 + scale)) + shift


def _mod_kernel(c_ref, w_ref, b_ref, o_ref):
    o_ref[...] = _dot3(_silu(c_ref[...]), w_ref[...]) + b_ref[...]


def _modulation(c_all, w_mod, b_mod):
    depth, d, d6 = w_mod.shape
    rows = c_all.shape[0]
    tn = 1536
    return pl.pallas_call(
        _mod_kernel,
        out_shape=jax.ShapeDtypeStruct((depth, rows, d6), F32),
        grid=(depth, d6 // tn),
        in_specs=[
            pl.BlockSpec((rows, d), lambda l, j: (0, 0)),
            pl.BlockSpec((None, d, tn), lambda l, j: (l, 0, j)),
            pl.BlockSpec((None, 1, tn), lambda l, j: (l, 0, j)),
        ],
        out_specs=pl.BlockSpec((None, rows, tn), lambda l, j: (l, 0, j)),
        compiler_params=_cparams(("arbitrary", "arbitrary")),
        name="mod",
    )(c_all, w_mod, b_mod.reshape(depth, 1, d6))


def _mod_spec(layer, chunk, d, row_of_batch):
    return pl.BlockSpec((None, None, 1, d), lambda b, t, *_: (layer, row_of_batch(b), 0, chunk))


def _gla_in_kernel(x_ref, sh_ref, sc_ref, ng_ref, win_ref, wga_ref, wgb_ref, bg_ref, *out_refs,
                   kdim, vdim, qscale, state_only):
    hb = _norm_modulate(x_ref[...], ng_ref[...], sc_ref[...], sh_ref[...]).astype(BF16)
    if state_only:
        k_ref, v_ref, g_ref = out_refs
        proj = _dot(hb, win_ref[:, kdim:2 * kdim + vdim])
        k_ref[...] = proj[:, :kdim].astype(BF16)
        v_ref[...] = proj[:, kdim:].astype(BF16)
    else:
        q_ref, k_ref, v_ref, r_ref, g_ref = out_refs
        proj = _dot(hb, win_ref[...])
        q_ref[...] = (proj[:, :kdim] * qscale).astype(BF16)
        k_ref[...] = proj[:, kdim:2 * kdim].astype(BF16)
        v_ref[...] = proj[:, 2 * kdim:2 * kdim + vdim].astype(BF16)
        r_ref[...] = proj[:, 2 * kdim + vdim:].astype(BF16)
    low = _dot(hb, wga_ref[...])
    z = _dot(low.astype(BF16), wgb_ref[...]) + bg_ref[...]
    logsig = jnp.minimum(z, 0.0) - jnp.log(1.0 + jnp.exp(-jnp.abs(z)))
    g_ref[...] = logsig * (1.0 / GLA_GATE_NORM)


def _gla_in(x, mods4, layer, row_of_batch, ng, win, wga, wgb, bg, tm, state_only=False):
    bsz, t, d = x.shape
    kdim, vdim = d // 2, d
    tm = min(tm, t)
    tok = lambda w: pl.BlockSpec((None, tm, w), lambda b, i: (b, i, 0))
    full = lambda a: pl.BlockSpec(a.shape, lambda b, i: (0,) * a.ndim)
    kern = functools.partial(_gla_in_kernel, kdim=kdim, vdim=vdim,
                             qscale=float((kdim // GLA_HEADS) ** -0.5), state_only=state_only)
    widths = (kdim, vdim, 2 * kdim) if state_only else (kdim, kdim, vdim, vdim, 2 * kdim)
    dtypes = (BF16,) * (len(widths) - 1) + (F32,)
    return pl.pallas_call(
        kern,
        out_shape=tuple(jax.ShapeDtypeStruct((bsz, t, w), dt) for w, dt in zip(widths, dtypes)),
        grid=(bsz, t // tm),
        in_specs=[
            tok(d),
            _mod_spec(layer, 0, d, row_of_batch),
            _mod_spec(layer, 1, d, row_of_batch),
            full(ng), full(win), full(wga), full(wgb), full(bg),
        ],
        out_specs=tuple(tok(w) for w in widths),
        compiler_params=_cparams(("arbitrary", "arbitrary")),
        name="gla_in",
    )(x, mods4, mods4, ng, win, wga, wgb, bg)


def _block_masks(rows, chunk):
    i = np.arange(rows)
    same = (i[:, None] // chunk) == (i[None, :] // chunk)
    lower = same & (i[:, None] >= i[None, :])
    upper = same & (i[:, None] <= i[None, :])
    return jnp.asarray(lower, BF16), jnp.asarray(upper, BF16)


def _per_chunk_row(a, chunk, r):
    rows, w = a.shape
    parts = [jnp.broadcast_to(a[c * chunk + r:c * chunk + r + 1, :], (chunk, w))
             for c in range(rows // chunk)]
    return jnp.concatenate(parts, axis=0)


def _block_terms(q, k, v, g2, lo, up, chunk, want_out):
    dk = g2.shape[1] // 2
    gh, gl = _split(g2)
    pre = _dot(lo, gh) + _dot(lo, gl)
    tot = _per_chunk_row(pre, chunk, chunk - 1)
    fwd = lax.broadcasted_iota(jnp.int32, g2.shape, 1) < dk
    cum = jnp.where(fwd, pre, tot - pre + g2)
    kf = k.astype(F32)
    kf2 = jnp.concatenate([kf, kf], axis=1)
    kl = (kf2 * jnp.exp(tot - cum)).astype(BF16)
    etot = jnp.exp(tot)
    if not want_out:
        return kl, etot, None, None
    mid = _per_chunk_row(cum, chunk, chunk // 2)
    qf = q.astype(F32)
    qf2 = jnp.concatenate([qf, qf], axis=1)
    qe = (qf2 * jnp.exp(cum)).astype(BF16)
    qi = (qf2 * jnp.exp(cum - mid)).astype(BF16)
    ki = (kf2 * jnp.exp(mid - cum)).astype(BF16)
    s = (jnp.where(lo > 0, _dot(qi[:, :dk], ki[:, :dk], _NT), 0.0)
         + jnp.where(up > 0, _dot(qi[:, dk:], ki[:, dk:], _NT), 0.0))
    return kl, etot, qe, _dot(s.astype(BF16), v)


def _store_chunk_states(v, kl, etot, ds_ref, e_ref, first_chunk, chunk):
    for c in range(v.shape[0] // chunk):
        rows = slice(c * chunk, (c + 1) * chunk)
        ds_ref[first_chunk + c] = _dot(v[rows, :], kl[rows, :], _TN)
        e_ref[first_chunk + c] = etot[c * chunk:c * chunk + SUBLANES, :]


def _scan_states(s0, ds_ref, e_ref, sc_ref, n, dk, first=0):
    def body(c, carry):
        sf, sb = carry
        r = n - 1 - c
        if sc_ref is not None:
            sc_ref[first + c, :, :dk] = sf.astype(BF16)
            sc_ref[first + r, :, dk:] = sb.astype(BF16)
        return (sf * e_ref[first + c, 0:1, :dk] + ds_ref[first + c, :, :dk],
                sb * e_ref[first + r, 0:1, dk:] + ds_ref[first + r, :, dk:])

    return lax.fori_loop(0, n, body, (s0[:, :dk], s0[:, dk:]))


def _gla_state_kernel(k_ref, v_ref, g_ref, lo_ref, up_ref, s_ref, ds_ref, e_ref, *, chunk, block):
    nh, dv = s_ref.shape[0], s_ref.shape[1]
    t = k_ref.shape[0]
    dk = k_ref.shape[1] // nh
    n = t // chunk
    per = block // chunk

    def terms(i, carry):
        rows = pl.ds(pl.multiple_of(i * block, block), block)
        for h in range(nh):
            v = v_ref[rows, h * dv:(h + 1) * dv]
            kl, etot, _, _ = _block_terms(None, k_ref[rows, h * dk:(h + 1) * dk], v,
                                          g_ref[rows, 2 * h * dk:2 * (h + 1) * dk],
                                          lo_ref[...], up_ref[...], chunk, False)
            _store_chunk_states(v, kl, etot, ds_ref, e_ref, h * n + i * per, chunk)
        return carry

    lax.fori_loop(0, t // block, terms, 0)
    for h in range(nh):
        sf, sb = _scan_states(jnp.zeros((dv, 2 * dk), F32), ds_ref, e_ref, None, n, dk, first=h * n)
        s_ref[h, :, :dk] = sf
        s_ref[h, :, dk:] = sb


def _gla_scan_kernel(q_ref, k_ref, v_ref, g_ref, s0_ref, ng_ref, lo_ref, up_ref,
                     o_ref, oi_ref, qe_ref, ds_ref, e_ref, sc_ref, *, chunk, block):
    t = q_ref.shape[0]
    dk = q_ref.shape[1]
    per = block // chunk

    group = GLA_BLOCKS_PER_TRIP if (t // block) % GLA_BLOCKS_PER_TRIP == 0 else 1

    def terms(i, carry):
        for u in range(group):
            blk = i * group + u
            rows = pl.ds(pl.multiple_of(blk * block, block), block)
            v = v_ref[rows, :]
            kl, etot, qe, oi = _block_terms(q_ref[rows, :], k_ref[rows, :], v, g_ref[rows, :],
                                            lo_ref[...], up_ref[...], chunk, True)
            _store_chunk_states(v, kl, etot, ds_ref, e_ref, blk * per, chunk)
            qe_ref[rows, :] = qe
            oi_ref[rows, :] = oi
        return carry

    lax.fori_loop(0, t // (block * group), terms, 0)
    _scan_states(s0_ref[...], ds_ref, e_ref, sc_ref, t // chunk, dk)

    fgroup = GLA_FINISH_BLOCKS_PER_TRIP if (t // block) % GLA_FINISH_BLOCKS_PER_TRIP == 0 else 1

    def finish(i, carry):
        for u in range(fgroup):
            blk = i * fgroup + u
            rows = pl.ds(pl.multiple_of(blk * block, block), block)
            inter = [_dot(qe_ref[pl.ds(pl.multiple_of(blk * block + c * chunk, chunk), chunk), :],
                          sc_ref[blk * per + c], _NT) for c in range(per)]
            o = oi_ref[rows, :] + jnp.concatenate(inter, axis=0)
            o_ref[rows, :] = _rmsnorm(o, ng_ref[...]).astype(o_ref.dtype)
        return carry

    lax.fori_loop(0, t // (block * fgroup), finish, 0)


def _gla_state(k, v, g, chunk, block):
    bsz, t, kdim = k.shape
    vdim = v.shape[-1]
    nh = GLA_HEADS
    dk, dv = kdim // nh, vdim // nh
    block = min(block, t)
    n = t // chunk
    lo, up = _block_masks(block, chunk)
    mask_spec = pl.BlockSpec((block, block), lambda b: (0, 0))
    seq = lambda w: pl.BlockSpec((None, t, w), lambda b: (b, 0, 0))
    return pl.pallas_call(
        functools.partial(_gla_state_kernel, chunk=chunk, block=block),
        out_shape=jax.ShapeDtypeStruct((bsz, nh, dv, 2 * dk), F32),
        grid=(bsz,),
        in_specs=[seq(kdim), seq(vdim), seq(2 * kdim), mask_spec, mask_spec],
        out_specs=pl.BlockSpec((None, nh, dv, 2 * dk), lambda b: (b, 0, 0, 0)),
        scratch_shapes=[pltpu.VMEM((nh * n, dv, 2 * dk), F32), pltpu.VMEM((nh * n, SUBLANES, 2 * dk), F32)],
        compiler_params=_cparams(("arbitrary",)),
        name="gla_state",
    )(k, v, g, lo, up)


def _gla_scan(q, k, v, g, s0, norm_g, chunk, block):
    bsz, t, kdim = k.shape
    vdim = v.shape[-1]
    nh = GLA_HEADS
    dk, dv = kdim // nh, vdim // nh
    block = min(block, t)
    n = t // chunk
    lo, up = _block_masks(block, chunk)
    kspec = pl.BlockSpec((None, t, dk), lambda b, h: (b, 0, h))
    vspec = pl.BlockSpec((None, t, dv), lambda b, h: (b, 0, h))
    mask_spec = pl.BlockSpec((block, block), lambda b, h: (0, 0))
    return pl.pallas_call(
        functools.partial(_gla_scan_kernel, chunk=chunk, block=block),
        out_shape=jax.ShapeDtypeStruct((bsz, t, vdim), BF16),
        grid=(bsz, nh),
        in_specs=[
            kspec, kspec, vspec,
            pl.BlockSpec((None, t, 2 * dk), lambda b, h: (b, 0, h)),
            pl.BlockSpec((None, None, dv, 2 * dk), lambda b, h: (b, h, 0, 0)),
            pl.BlockSpec((1, dv), lambda b, h: (0, 0)),
            mask_spec, mask_spec,
        ],
        out_specs=vspec,
        scratch_shapes=[
            pltpu.VMEM((t, dv), F32),
            pltpu.VMEM((t, 2 * dk), BF16),
            pltpu.VMEM((n, dv, 2 * dk), F32),
            pltpu.VMEM((n, SUBLANES, 2 * dk), F32),
            pltpu.VMEM((n, dv, 2 * dk), BF16),
        ],
        compiler_params=_cparams(("arbitrary", "arbitrary")),
        name="gla_scan",
    )(q, k, v, g, s0, norm_g, lo, up)


def _route(h2, wr_t, br, earlier_bf, count_ref):
    tm = h2.shape[0]
    logits = _dot3(wr_t, h2, _NT)
    scores = _sigmoid(logits)
    sel = scores + br
    row = [sel[e:e + 1, :] for e in range(N_EXPERTS)]
    picked = []
    for e in range(N_EXPERTS):
        g0 = (e // EXPERTS_PER_GROUP) * EXPERTS_PER_GROUP
        ahead_count = jnp.zeros(row[e].shape, jnp.int32)
        for j in range(g0, g0 + EXPERTS_PER_GROUP):
            if j == e:
                continue
            ahead = (row[j] >= row[e]) if j < e else (row[j] > row[e])
            ahead_count = ahead_count + ahead.astype(jnp.int32)
        picked.append(ahead_count < TOP_K)
    zero = jnp.zeros_like(row[0])
    gscore, pair, first_w, second_w = [], [], [], []
    for g in range(N_EXPERT_GROUPS):
        acc, pidx, fw, sw = zero, zero, zero, zero
        seen = None
        for a in range(EXPERTS_PER_GROUP):
            e = g * EXPERTS_PER_GROUP + a
            sc = scores[e:e + 1, :]
            acc = acc + jnp.where(picked[e], row[e], 0.0)
            if seen is None:
                is_first = picked[e]
            else:
                is_first = picked[e] & jnp.logical_not(seen)
                is_second = picked[e] & seen
                sw = sw + jnp.where(is_second, sc, 0.0)
                pidx = pidx + jnp.where(is_second, float(a), 0.0)
            fw = fw + jnp.where(is_first, sc, 0.0)
            if a < len(_PAIR_BASE):
                pidx = pidx + jnp.where(is_first, float(_PAIR_BASE[a] - a - 1), 0.0)
            seen = picked[e] if seen is None else (seen | picked[e])
        gscore.append(acc)
        pair.append(pidx)
        first_w.append(fw)
        second_w.append(sw)
    cls, wa, wb = zero, zero, zero
    for g in range(N_EXPERT_GROUPS):
        ok = None
        for j in range(N_EXPERT_GROUPS):
            if j == g:
                continue
            c = (gscore[g] > gscore[j]) if j < g else (gscore[g] >= gscore[j])
            ok = c if ok is None else (ok & c)
        cls = cls + jnp.where(ok, pair[g] + float(len(_PAIRS) * g), 0.0)
        wa = wa + jnp.where(ok, first_w[g], 0.0)
        wb = wb + jnp.where(ok, second_w[g], 0.0)
    denom = wa + wb
    wa = wa / denom
    wb = wb / denom

    cid = lax.broadcasted_iota(jnp.int32, (CLASS_ROWS, tm), 0).astype(F32)
    onehot = (cid == cls).astype(BF16)
    before = _dot(onehot, earlier_bf)
    oh = onehot.astype(F32)
    base = count_ref[...][:, 0:1]
    rank = jnp.sum(oh * (before + base), axis=0, keepdims=True)
    count_ref[...] = count_ref[...] + jnp.sum(oh, axis=1, keepdims=True)
    pad = jnp.zeros((INFO_ROWS - 4, tm), F32)
    return jnp.concatenate([cls, rank, wa, wb, pad], axis=0)


def _earlier_matrix(tm):
    i = np.arange(tm)
    return jnp.asarray(i[:, None] < i[None, :], BF16)


def _store_tiled_rows(ref, x):
    tm, d = x.shape
    per = d // LANES
    for c in range(per):
        ref[pl.ds(c, tm, stride=per), :] = x[:, c * LANES:(c + 1) * LANES]


def _load_tiled_rows(ref, tm):
    per = ref.shape[0] // tm
    return jnp.concatenate([ref[pl.ds(c, tm, stride=per), :] for c in range(per)], axis=1)


def _route_outputs(h2, wr_ref, br_ref, earlier_ref, count_ref, pay_ref, info_ref, counts_ref):
    first = (pl.program_id(0) == 0) & (pl.program_id(1) == 0)

    @pl.when(first)
    def _():
        count_ref[...] = jnp.zeros_like(count_ref)

    info = _route(h2, wr_ref[...], br_ref[...], earlier_ref[...], count_ref)
    _store_tiled_rows(pay_ref, h2)
    info_ref[...] = info
    counts_ref[...] = count_ref[...]


def _route_out_shapes(bsz, t, d):
    return (
        jax.ShapeDtypeStruct((bsz * t * (d // LANES), LANES), F32),
        jax.ShapeDtypeStruct((INFO_ROWS, bsz * t), F32),
        jax.ShapeDtypeStruct((CLASS_ROWS, LANES), F32),
    )


def _route_out_specs(tm, nt, d):
    return (
        pl.BlockSpec((tm * (d // LANES), LANES), lambda b, i, *_: (b * nt + i, 0)),
        pl.BlockSpec((INFO_ROWS, tm), lambda b, i, *_: (0, b * nt + i)),
        pl.BlockSpec((CLASS_ROWS, LANES), lambda b, i, *_: (0, 0)),
    )


def _plan(info, counts, n, tmoe):
    cls = info[0].astype(jnp.int32)
    rank = info[1].astype(jnp.int32)
    cnt = counts[:N_CLASSES, 0].astype(jnp.int32)
    padded = (cnt + tmoe - 1) // tmoe * tmoe
    ends = jnp.cumsum(padded)
    dest = (ends - padded)[cls] + rank
    n_tiles = n // tmoe + N_CLASSES
    rows = jnp.stack([jnp.arange(n, dtype=F32), info[2], info[3], jnp.zeros((n,), F32)], axis=1)
    srt = jnp.zeros((n_tiles * tmoe, 4), F32).at[dest].set(rows, unique_indices=True)
    tok = srt[:, 0].astype(jnp.int32)
    n_used = ends[-1] // tmoe
    tile = jnp.minimum(jnp.arange(n_tiles, dtype=jnp.int32), n_used - 1)
    tcls = jnp.sum((ends[None, :] <= (tile * tmoe)[:, None]).astype(jnp.int32), axis=1)
    group, pair = tcls // len(_PAIRS), tcls % len(_PAIRS)
    pa = jnp.asarray([p[0] for p in _PAIRS], jnp.int32)[pair]
    pb = jnp.asarray([p[1] for p in _PAIRS], jnp.int32)[pair]
    ea = group * EXPERTS_PER_GROUP + pa
    eb = group * EXPERTS_PER_GROUP + pb
    return dest, tok, srt, ea, eb, n_used.reshape(1).astype(jnp.int32)


def _post0_kernel(o_ref, r_ref, x_ref, g1_ref, sh2_ref, sc2_ref, n2_ref, wout_ref, wr_ref, br_ref,
                  earlier_ref, x1_ref, pay_ref, info_ref, counts_ref, count_ref):
    a = (o_ref[...].astype(F32) * _silu(r_ref[...].astype(F32))).astype(BF16)
    x1 = x_ref[...] + g1_ref[...] * _dot(a, wout_ref[...])
    x1_ref[...] = x1
    h2 = _norm_modulate(x1, n2_ref[...], sc2_ref[...], sh2_ref[...])
    _route_outputs(h2, wr_ref, br_ref, earlier_ref, count_ref, pay_ref, info_ref, counts_ref)


def _post0(o, r, x, mods4, layer, n2, wout, wr_t, br, tm):
    bsz, t, d = x.shape
    tm = min(tm, t)
    rb = lambda b: b
    tok = lambda w: pl.BlockSpec((None, tm, w), lambda b, i: (b, i, 0))
    full = lambda a: pl.BlockSpec(a.shape, lambda b, i: (0,) * a.ndim)
    nt = t // tm
    earlier = _earlier_matrix(tm)
    return pl.pallas_call(
        _post0_kernel,
        out_shape=(jax.ShapeDtypeStruct((bsz, t, d), F32),) + _route_out_shapes(bsz, t, d),
        grid=(bsz, nt),
        in_specs=[
            tok(d), tok(d), tok(d),
            _mod_spec(layer, 2, d, rb), _mod_spec(layer, 3, d, rb), _mod_spec(layer, 4, d, rb),
            full(n2), full(wout), full(wr_t), full(br), full(earlier),
        ],
        out_specs=(tok(d),) + _route_out_specs(tm, nt, d),
        scratch_shapes=[pltpu.VMEM((CLASS_ROWS, LANES), F32)],
        compiler_params=_cparams(("arbitrary", "arbitrary")),
        name="post0",
    )(o, r, x, mods4, mods4, mods4, n2, wout, wr_t, br, earlier)


def _row_group(i, per):
    return pl.ds(pl.multiple_of(i * per, per), per)


def _moe_kernel(tok_ref, ea_ref, eb_ref, used_ref, p_hbm, gw_ref, wa_ref, wb_ref, f_ref,
                pbuf0, pbuf1, gsem, *, d, tm):
    del ea_ref, eb_ref
    j = pl.program_id(0)
    n_tiles = pl.num_programs(0)
    used = used_ref[0]
    per = d // LANES
    pbufs = (pbuf0, pbuf1)

    def gather_row(tile, r, p, pred, lane):
        i = tok_ref[tile * tm + r]

        @pl.when(pred)
        def _():
            pltpu.make_async_copy(p_hbm.at[_row_group(i, per), :], pbufs[p].at[_row_group(r, per), :],
                                  gsem.at[p]).start(priority=lane)

    def gather_wait(p):
        pltpu.make_async_copy(p_hbm.at[pl.ds(0, tm * per), :], pbufs[p], gsem.at[p]).wait()

    @pl.when(j == 0)
    def _():
        def body(blk, carry):
            for u in range(DMA_UNROLL):
                gather_row(0, blk * DMA_UNROLL + u, 0, True, u % 2)
            return carry
        lax.fori_loop(0, tm // DMA_UNROLL, body, 0)

    def step(p, beside_matmuls):
        @pl.when(j < used)
        def _():
            has_next = j + 1 < used
            nxt = jnp.minimum(j + 1, n_tiles - 1)
            if beside_matmuls:
                gather_wait(p)
                for r in range(tm):
                    gather_row(nxt, r, 1 - p, has_next, r % 2)
            else:
                @pl.when(j >= 0)
                def _():
                    for r in range(tm):
                        gather_row(nxt, r, 1 - p, has_next, r % 2)
                gather_wait(p)
            h = _load_tiled_rows(pbufs[p], tm).astype(BF16)
            gw = gw_ref[...]

            def expert(w_ref):
                de = w_ref.shape[2]
                he = (_silu(_dot(h, w_ref[0])) * _dot(h, w_ref[1])).astype(BF16)
                return jnp.concatenate([_dot(he, w_ref[2, :de, :]), _dot(he, w_ref[2, de:, :])], axis=1)

            _store_tiled_rows(f_ref, gw[:, 1:2] * expert(wa_ref) + gw[:, 2:3] * expert(wb_ref))

    @pl.when(j % 2 == 0)
    def _():
        step(0, True)

    @pl.when(j % 2 == 1)
    def _():
        step(1, False)

    @pl.when(j >= used)
    def _():
        f_ref[...] = jnp.zeros(f_ref.shape, F32)


def _moe(payload, tok, srt, ea, eb, n_used, w_all, layer, tm):
    n_sorted = tok.shape[0]
    n_tiles = n_sorted // tm
    _, _, _, d, de = w_all.shape
    per = d // LANES
    wa = lambda j, tok, ea, eb, used: (layer, ea[j], 0, 0, 0)
    wb = lambda j, tok, ea, eb, used: (layer, eb[j], 0, 0, 0)
    return pl.pallas_call(
        functools.partial(_moe_kernel, d=d, tm=tm),
        out_shape=jax.ShapeDtypeStruct((n_sorted * per, LANES), F32),
        grid_spec=pltpu.PrefetchScalarGridSpec(
            num_scalar_prefetch=4,
            grid=(n_tiles,),
            in_specs=[
                pl.BlockSpec(memory_space=pl.ANY),
                pl.BlockSpec((tm, srt.shape[1]), lambda j, *_: (j, 0)),
                pl.BlockSpec((None, None, 3, d, de), wa), pl.BlockSpec((None, None, 3, d, de), wb),
            ],
            out_specs=pl.BlockSpec((tm * per, LANES), lambda j, *_: (j, 0)),
            scratch_shapes=[pltpu.VMEM((tm * per, LANES), F32), pltpu.VMEM((tm * per, LANES), F32),
                            pltpu.SemaphoreType.DMA((2,))],
        ),
        compiler_params=_cparams(("arbitrary",)),
        name="moe",
    )(tok, ea, eb, n_used, payload, srt, w_all, w_all)


def _expert_weights_kernel(w1_ref, w3_ref, w2_ref, o_ref):
    de = w2_ref.shape[0]
    o_ref[0] = w1_ref[...].astype(BF16)
    o_ref[1] = w3_ref[...].astype(BF16)
    o_ref[2, :de, :] = w2_ref[:, :de].astype(BF16)
    o_ref[2, de:, :] = w2_ref[:, de:].astype(BF16)


def _expert_weights(w1, w3, w2):
    nl, ne, d, de = w1.shape
    assert d == 2 * de
    up = pl.BlockSpec((None, None, d, de), lambda l, e: (l, e, 0, 0))
    return pl.pallas_call(
        _expert_weights_kernel,
        out_shape=jax.ShapeDtypeStruct((nl, ne, 3, d, de), BF16),
        grid=(nl, ne),
        in_specs=[up, up, pl.BlockSpec((None, None, de, d), lambda l, e: (l, e, 0, 0))],
        out_specs=pl.BlockSpec((None, None, 3, d, de), lambda l, e: (l, e, 0, 0, 0)),
        compiler_params=_cparams(("arbitrary", "arbitrary")),
        name="expert_weights",
    )(w1, w3, w2)


def _sparse_moe(payload, info, counts, w_all, layer):
    n = info.shape[1]
    tmoe = min(TM_MOE, n)
    dest, tok, srt, ea, eb, n_used = _plan(info, counts, n, tmoe)
    return _moe(payload, tok, srt, ea, eb, n_used, w_all, layer, tmoe), dest


def _gather_rows_start(dest_ref, f_hbm, fbuf, fsem, tile, tm, per, pred, unrolled):
    def one(r, lane):
        i = dest_ref[tile * tm + r]

        @pl.when(pred)
        def _():
            pltpu.make_async_copy(f_hbm.at[_row_group(i, per), :], fbuf.at[_row_group(r, per), :],
                                  fsem).start(priority=lane)

    if unrolled:
        for r in range(tm):
            one(r, r % 2)
    else:
        def body(blk, carry):
            for u in range(DMA_UNROLL):
                one(blk * DMA_UNROLL + u, u % 2)
            return carry
        lax.fori_loop(0, tm // DMA_UNROLL, body, 0)


def _gather_rows_wait(f_hbm, fbuf, fsem):
    pltpu.make_async_copy(f_hbm.at[pl.ds(0, fbuf.shape[0]), :], fbuf, fsem).wait()


def _pool_constants(tm, d):
    ng = len(POOL_WINDOWS)
    cg = d // ng
    pos = np.arange(tm)
    seg, off = pos // GRID_W, pos % GRID_W
    mats = np.zeros((ng, tm, tm), np.float32)
    inv = np.zeros((tm, d), np.float32)
    for gi, w in enumerate(POOL_WINDOWS):
        lo = np.clip(off - w // 2, 0, GRID_W)
        hi = np.clip(off - w // 2 + w, 0, GRID_W)
        same = seg[:, None] == seg[None, :]
        inside = (off[None, :] >= lo[:, None]) & (off[None, :] < hi[:, None])
        mats[gi] = (same & inside).astype(np.float32)
        inv[:, gi * cg:(gi + 1) * cg] = (1.0 / (hi - lo).astype(np.float32))[:, None]
    return jnp.asarray(mats, BF16), jnp.asarray(inv, F32)


def _layer1_kernel(dest_ref, x_ref, f_hbm, g2p_ref, sh1_ref, sc1_ref, g1_ref, sh2_ref, sc2_ref,
                   n1_ref, n2_ref, pm_ref, inv_ref, wp_ref, bp_ref, ps_ref, wr_ref, br_ref, earlier_ref,
                   x3_ref, pay_ref, info_ref, counts_ref, fbuf, fsem, x2_ref, count_ref):
    tm, d = x_ref.shape
    per = d // LANES
    nt = pl.num_programs(1)
    lin = pl.program_id(0) * nt + pl.program_id(1)

    @pl.when(lin == 0)
    def _():
        _gather_rows_start(dest_ref, f_hbm, fbuf, fsem, 0, tm, per, True, False)

    @pl.when(lin >= 0)
    def _():
        _gather_rows_wait(f_hbm, fbuf, fsem)
        x2_ref[...] = x_ref[...] + g2p_ref[...] * _load_tiled_rows(fbuf, tm)

    total = pl.num_programs(0) * nt
    _gather_rows_start(dest_ref, f_hbm, fbuf, fsem, jnp.minimum(lin + 1, total - 1), tm, per,
                       lin + 1 < total, True)
    x2 = x2_ref[...]
    h = _norm_modulate(x2, n1_ref[...], sc1_ref[...], sh1_ref[...])
    hb = h.astype(BF16)
    ng = pm_ref.shape[0]
    cg = h.shape[1] // ng
    ys = []
    for gi in range(ng):
        cs = slice(gi * cg, (gi + 1) * cg)
        wsum = _dot(pm_ref[gi], hb[:, cs])
        pooled = wsum * inv_ref[:, cs] - h[:, cs]
        ys.append(_dot(pooled.astype(BF16), wp_ref[gi]))
    y = (jnp.concatenate(ys, axis=1) + bp_ref[...]) * ps_ref[...]
    x3 = x2 + g1_ref[...] * y
    x3_ref[...] = x3
    h2 = _norm_modulate(x3, n2_ref[...], sc2_ref[...], sh2_ref[...])
    _route_outputs(h2, wr_ref, br_ref, earlier_ref, count_ref, pay_ref, info_ref, counts_ref)


def _layer1(x1, f_sorted, dest, mods4, n1, n2, wp, bp, ps, wr_t, br, tm):
    bsz, t, d = x1.shape
    tm = min(tm, t)
    pm, inv = _pool_constants(tm, d)
    earlier = _earlier_matrix(tm)
    rb = lambda b: b
    tok = lambda w: pl.BlockSpec((None, tm, w), lambda b, i, dest: (b, i, 0))
    full = lambda a: pl.BlockSpec(a.shape, lambda b, i, dest: (0,) * a.ndim)
    nt = t // tm
    return pl.pallas_call(
        _layer1_kernel,
        out_shape=(jax.ShapeDtypeStruct((bsz, t, d), F32),) + _route_out_shapes(bsz, t, d),
        grid_spec=pltpu.PrefetchScalarGridSpec(
            num_scalar_prefetch=1,
            grid=(bsz, nt),
            in_specs=[
                tok(d), pl.BlockSpec(memory_space=pl.ANY),
                _mod_spec(0, 5, d, rb),
                _mod_spec(1, 0, d, rb), _mod_spec(1, 1, d, rb), _mod_spec(1, 2, d, rb),
                _mod_spec(1, 3, d, rb), _mod_spec(1, 4, d, rb),
                full(n1), full(n2), full(pm), full(inv), full(wp), full(bp), full(ps),
                full(wr_t), full(br), full(earlier),
            ],
            out_specs=(tok(d),) + _route_out_specs(tm, nt, d),
            scratch_shapes=[pltpu.VMEM((tm * (d // LANES), LANES), F32), pltpu.SemaphoreType.DMA(()),
                            pltpu.VMEM((tm, d), F32), pltpu.VMEM((CLASS_ROWS, LANES), F32)],
        ),
        compiler_params=_cparams(("arbitrary", "arbitrary")),
        name="layer1",
    )(dest, x1, f_sorted, mods4, mods4, mods4, mods4, mods4, mods4, n1, n2, pm, inv, wp, bp, ps,
      wr_t, br, earlier)


def _final_kernel(dest_ref, x_ref, f_hbm, g2_ref, fg_ref, o_ref, fbuf, fsem):
    tm, d = x_ref.shape
    per = d // LANES
    nt = pl.num_programs(1)
    lin = pl.program_id(0) * nt + pl.program_id(1)

    @pl.when(lin == 0)
    def _():
        _gather_rows_start(dest_ref, f_hbm, fbuf, fsem, 0, tm, per, True, False)

    @pl.when(lin >= 0)
    def _():
        _gather_rows_wait(f_hbm, fbuf, fsem)
        f = _load_tiled_rows(fbuf, tm)
        o_ref[...] = _rmsnorm(x_ref[...] + g2_ref[...] * f, fg_ref[...])

    total = pl.num_programs(0) * nt
    _gather_rows_start(dest_ref, f_hbm, fbuf, fsem, jnp.minimum(lin + 1, total - 1), tm, per,
                       lin + 1 < total, False)


def _final(x3, f_sorted, dest, mods4, fg, tm):
    bsz, t, d = x3.shape
    tm = min(tm, t)
    tok = pl.BlockSpec((None, tm, d), lambda b, i, dest: (b, i, 0))
    return pl.pallas_call(
        _final_kernel,
        out_shape=jax.ShapeDtypeStruct((bsz, t, d), F32),
        grid_spec=pltpu.PrefetchScalarGridSpec(
            num_scalar_prefetch=1,
            grid=(bsz, t // tm),
            in_specs=[tok, pl.BlockSpec(memory_space=pl.ANY), _mod_spec(1, 5, d, lambda b: b),
                      pl.BlockSpec((1, d), lambda b, i, dest: (0, 0))],
            out_specs=tok,
            scratch_shapes=[pltpu.VMEM((tm * (d // LANES), LANES), F32), pltpu.SemaphoreType.DMA(())],
        ),
        compiler_params=_cparams(("arbitrary", "arbitrary")),
        name="final",
    )(dest, x3, f_sorted, mods4, fg)


def kernel(x, c, ctx, c_ctx, norm1_g, norm2_g, w_mod, b_mod, gla_w_in, gla_w_gate_a, gla_w_gate_b,
           gla_b_gate, gla_norm_g, gla_w_out, pool_w, pool_b, pool_scale, w_router, b_router,
           w_gate_e, w_up_e, w_down_e, final_g):
    bsz, t, d = x.shape
    depth = w_mod.shape[0]
    assert depth == 2 and t % GRID_W == 0
    kdim = d // 2
    row = lambda a: a.reshape(1, -1)

    c_rows = -(-(bsz + 1) // 8) * 8
    c_all = jnp.zeros((c_rows, d), F32).at[:bsz].set(c).at[bsz].set(c_ctx)
    mods = _modulation(c_all, w_mod, b_mod)
    mods4 = mods.reshape(depth, c_rows, 1, 6 * d)

    win = gla_w_in[0].astype(BF16)
    wga = jnp.concatenate([gla_w_gate_a[0, 0], gla_w_gate_a[0, 1]], axis=1).astype(BF16)
    dk = kdim // GLA_HEADS
    by_head = lambda a: a.reshape(a.shape[0], GLA_HEADS, 1, dk)
    zero = jnp.zeros((GLA_GATE_RANK, GLA_HEADS, 1, dk), F32)
    wgb = jnp.concatenate([
        jnp.concatenate([by_head(gla_w_gate_b[0, 0]), zero], axis=2),
        jnp.concatenate([zero, by_head(gla_w_gate_b[0, 1])], axis=2)], axis=0)
    wgb = wgb.reshape(2 * GLA_GATE_RANK, 2 * kdim).astype(BF16)
    bg = jnp.stack([gla_b_gate[0, 0].reshape(GLA_HEADS, dk), gla_b_gate[0, 1].reshape(GLA_HEADS, dk)],
                   axis=1).reshape(1, 2 * kdim)
    n1_0 = row(norm1_g[0])
    q, k, v, r, g = _gla_in(x, mods4, 0, lambda b: b, n1_0, win, wga, wgb, bg, TM_PROJ)
    kc, vc, gc = _gla_in(ctx, mods4, 0, lambda b: bsz, n1_0, win, wga, wgb, bg, TM_CTX, state_only=True)
    s0 = _gla_state(kc, vc, gc, GLA_CHUNK, GLA_BLOCK)
    o = _gla_scan(q, k, v, g, s0, row(gla_norm_g[0]), GLA_CHUNK, GLA_BLOCK)

    wr_t = jnp.transpose(w_router)
    br = b_router.reshape(N_EXPERTS, 1)
    x1, payload, info, counts = _post0(o, r, x, mods4, 0, row(norm2_g[0]),
                                       gla_w_out[0].astype(BF16), wr_t, br, TM_POST)
    w_experts = _expert_weights(w_gate_e, w_up_e, w_down_e)
    f0, dest0 = _sparse_moe(payload, info, counts, w_experts, 0)

    x3, payload, info, counts = _layer1(x1, f0, dest0, mods4, row(norm1_g[1]), row(norm2_g[1]),
                                        pool_w[0].astype(BF16), row(pool_b[0]), row(pool_scale[0]),
                                        wr_t, br, TM_POOL)
    f1, dest1 = _sparse_moe(payload, info, counts, w_experts, 1)
    return _final(x3, f1, dest1, mods4, row(final_g), TM_FINAL)
```

```python
import functools

import numpy as np
import jax
import jax.numpy as jnp
from jax import lax
from jax.experimental import pallas as pl
from jax.experimental.pallas import tpu as pltpu

EPS = 1e-6
GRID_W = 64
GLA_HEADS = 4
GLA_GATE_RANK = 16
GLA_GATE_NORM = 16.0
GLA_CHUNK = 128
GLA_BLOCK = 256
GLA_BLOCKS_PER_TRIP = 8
GLA_FINISH_BLOCKS_PER_TRIP = 8
POOL_WINDOWS = (2, 4, 8, 16)
N_EXPERTS = 16
N_EXPERT_GROUPS = 4
EXPERTS_PER_GROUP = N_EXPERTS // N_EXPERT_GROUPS
TOP_K = 2

_PAIRS = tuple((a, b) for a in range(EXPERTS_PER_GROUP) for b in range(a + 1, EXPERTS_PER_GROUP))
_PAIR_BASE = (0, 3, 5)
N_CLASSES = N_EXPERT_GROUPS * len(_PAIRS)
CLASS_ROWS = 32
INFO_ROWS = 8
LANES = 128
SUBLANES = 8

TM_PROJ = 1024
TM_CTX = 256
TM_POST = 1024
TM_POOL = 512
TM_MOE = 256
TM_FINAL = 2048
DMA_UNROLL = 8

F32 = jnp.float32
BF16 = jnp.bfloat16

_NT = (((1,), (1,)), ((), ()))
_TN = (((0,), (0,)), ((), ()))
_VMEM_LIMIT = 56 * 1024 * 1024


def _cparams(sem):
    return pltpu.CompilerParams(dimension_semantics=sem, vmem_limit_bytes=_VMEM_LIMIT)


def _dot(a, b, dims=None):
    if dims is None:
        return jnp.dot(a, b, preferred_element_type=F32)
    return lax.dot_general(a, b, dims, preferred_element_type=F32)


def _split(a):
    hi = a.astype(BF16)
    lo = (a - hi.astype(F32)).astype(BF16)
    return hi, lo


def _dot3(a, b, dims=None):
    ah, al = _split(a)
    bh, bl = _split(b)
    return _dot(ah, bh, dims) + _dot(ah, bl, dims) + _dot(al, bh, dims)


def _sigmoid(x):
    return 1.0 / (1.0 + jnp.exp(-x))


def _silu(x):
    return x * _sigmoid(x)


def _rmsnorm(xf, g):
    return xf * lax.rsqrt(jnp.mean(xf * xf, axis=-1, keepdims=True) + EPS) * g


def _norm_modulate(xf, g, scale, shift):
    return xf * lax.rsqrt(jnp.mean(xf * xf, axis=-1, keepdims=True) + EPS) * (g * (1.0 + scale)) + shift


def _mod_kernel(c_ref, w_ref, b_ref, o_ref):
    o_ref[...] = _dot3(_silu(c_ref[...]), w_ref[...]) + b_ref[...]


def _modulation(c_all, w_mod, b_mod):
    depth, d, d6 = w_mod.shape
    rows = c_all.shape[0]
    tn = 1536
    return pl.pallas_call(
        _mod_kernel,
        out_shape=jax.ShapeDtypeStruct((depth, rows, d6), F32),
        grid=(depth, d6 // tn),
        in_specs=[
            pl.BlockSpec((rows, d), lambda l, j: (0, 0)),
            pl.BlockSpec((None, d, tn), lambda l, j: (l, 0, j)),
            pl.BlockSpec((None, 1, tn), lambda l, j: (l, 0, j)),
        ],
        out_specs=pl.BlockSpec((None, rows, tn), lambda l, j: (l, 0, j)),
        compiler_params=_cparams(("arbitrary", "arbitrary")),
        name="mod",
    )(c_all, w_mod, b_mod.reshape(depth, 1, d6))


def _mod_spec(layer, chunk, d, row_of_batch):
    return pl.BlockSpec((None, None, 1, d), lambda b, t, *_: (layer, row_of_batch(b), 0, chunk))


def _gla_in_kernel(x_ref, sh_ref, sc_ref, ng_ref, win_ref, wga_ref, wgb_ref, bg_ref, *out_refs,
                   kdim, vdim, qscale, state_only):
    hb = _norm_modulate(x_ref[...], ng_ref[...], sc_ref[...], sh_ref[...]).astype(BF16)
    if state_only:
        k_ref, v_ref, g_ref = out_refs
        proj = _dot(hb, win_ref[:, kdim:2 * kdim + vdim])
        k_ref[...] = proj[:, :kdim].astype(BF16)
        v_ref[...] = proj[:, kdim:].astype(BF16)
    else:
        q_ref, k_ref, v_ref, r_ref, g_ref = out_refs
        proj = _dot(hb, win_ref[...])
        q_ref[...] = (proj[:, :kdim] * qscale).astype(BF16)
        k_ref[...] = proj[:, kdim:2 * kdim].astype(BF16)
        v_ref[...] = proj[:, 2 * kdim:2 * kdim + vdim].astype(BF16)
        r_ref[...] = proj[:, 2 * kdim + vdim:].astype(BF16)
    low = _dot(hb, wga_ref[...])
    z = _dot(low.astype(BF16), wgb_ref[...]) + bg_ref[...]
    logsig = jnp.minimum(z, 0.0) - jnp.log(1.0 + jnp.exp(-jnp.abs(z)))
    g_ref[...] = logsig * (1.0 / GLA_GATE_NORM)


def _gla_in(x, mods4, layer, row_of_batch, ng, win, wga, wgb, bg, tm, state_only=False):
    bsz, t, d = x.shape
    kdim, vdim = d // 2, d
    tm = min(tm, t)
    tok = lambda w: pl.BlockSpec((None, tm, w), lambda b, i: (b, i, 0))
    full = lambda a: pl.BlockSpec(a.shape, lambda b, i: (0,) * a.ndim)
    kern = functools.partial(_gla_in_kernel, kdim=kdim, vdim=vdim,
                             qscale=float((kdim // GLA_HEADS) ** -0.5), state_only=state_only)
    widths = (kdim, vdim, 2 * kdim) if state_only else (kdim, kdim, vdim, vdim, 2 * kdim)
    dtypes = (BF16,) * (len(widths) - 1) + (F32,)
    return pl.pallas_call(
        kern,
        out_shape=tuple(jax.ShapeDtypeStruct((bsz, t, w), dt) for w, dt in zip(widths, dtypes)),
        grid=(bsz, t // tm),
        in_specs=[
            tok(d),
            _mod_spec(layer, 0, d, row_of_batch),
            _mod_spec(layer, 1, d, row_of_batch),
            full(ng), full(win), full(wga), full(wgb), full(bg),
        ],
        out_specs=tuple(tok(w) for w in widths),
        compiler_params=_cparams(("arbitrary", "arbitrary")),
        name="gla_in",
    )(x, mods4, mods4, ng, win, wga, wgb, bg)


def _block_masks(rows, chunk):
    i = np.arange(rows)
    same = (i[:, None] // chunk) == (i[None, :] // chunk)
    lower = same & (i[:, None] >= i[None, :])
    upper = same & (i[:, None] <= i[None, :])
    return jnp.asarray(lower, BF16), jnp.asarray(upper, BF16)


def _per_chunk_row(a, chunk, r):
    rows, w = a.shape
    parts = [jnp.broadcast_to(a[c * chunk + r:c * chunk + r + 1, :], (chunk, w))
             for c in range(rows // chunk)]
    return jnp.concatenate(parts, axis=0)


def _block_terms(q, k, v, g2, lo, up, chunk, want_out):
    dk = g2.shape[1] // 2
    gh, gl = _split(g2)
    pre = _dot(lo, gh) + _dot(lo, gl)
    tot = _per_chunk_row(pre, chunk, chunk - 1)
    fwd = lax.broadcasted_iota(jnp.int32, g2.shape, 1) < dk
    cum = jnp.where(fwd, pre, tot - pre + g2)
    kf = k.astype(F32)
    kf2 = jnp.concatenate([kf, kf], axis=1)
    kl = (kf2 * jnp.exp(tot - cum)).astype(BF16)
    etot = jnp.exp(tot)
    if not want_out:
        return kl, etot, None, None
    mid = _per_chunk_row(cum, chunk, chunk // 2)
    qf = q.astype(F32)
    qf2 = jnp.concatenate([qf, qf], axis=1)
    qe = (qf2 * jnp.exp(cum)).astype(BF16)
    qi = (qf2 * jnp.exp(cum - mid)).astype(BF16)
    ki = (kf2 * jnp.exp(mid - cum)).astype(BF16)
    s = (jnp.where(lo > 0, _dot(qi[:, :dk], ki[:, :dk], _NT), 0.0)
         + jnp.where(up > 0, _dot(qi[:, dk:], ki[:, dk:], _NT), 0.0))
    return kl, etot, qe, _dot(s.astype(BF16), v)


def _store_chunk_states(v, kl, etot, ds_ref, e_ref, first_chunk, chunk):
    for c in range(v.shape[0] // chunk):
        rows = slice(c * chunk, (c + 1) * chunk)
        ds_ref[first_chunk + c] = _dot(v[rows, :], kl[rows, :], _TN)
        e_ref[first_chunk + c] = etot[c * chunk:c * chunk + SUBLANES, :]


def _scan_states(s0, ds_ref, e_ref, sc_ref, n, dk, first=0):
    def body(c, carry):
        sf, sb = carry
        r = n - 1 - c
        if sc_ref is not None:
            sc_ref[first + c, :, :dk] = sf.astype(BF16)
            sc_ref[first + r, :, dk:] = sb.astype(BF16)
        return (sf * e_ref[first + c, 0:1, :dk] + ds_ref[first + c, :, :dk],
                sb * e_ref[first + r, 0:1, dk:] + ds_ref[first + r, :, dk:])

    return lax.fori_loop(0, n, body, (s0[:, :dk], s0[:, dk:]))


def _gla_state_kernel(k_ref, v_ref, g_ref, lo_ref, up_ref, s_ref, ds_ref, e_ref, *, chunk, block):
    nh, dv = s_ref.shape[0], s_ref.shape[1]
    t = k_ref.shape[0]
    dk = k_ref.shape[1] // nh
    n = t // chunk
    per = block // chunk

    def terms(i, carry):
        rows = pl.ds(pl.multiple_of(i * block, block), block)
        for h in range(nh):
            v = v_ref[rows, h * dv:(h + 1) * dv]
            kl, etot, _, _ = _block_terms(None, k_ref[rows, h * dk:(h + 1) * dk], v,
                                          g_ref[rows, 2 * h * dk:2 * (h + 1) * dk],
                                          lo_ref[...], up_ref[...], chunk, False)
            _store_chunk_states(v, kl, etot, ds_ref, e_ref, h * n + i * per, chunk)
        return carry

    lax.fori_loop(0, t // block, terms, 0)
    for h in range(nh):
        sf, sb = _scan_states(jnp.zeros((dv, 2 * dk), F32), ds_ref, e_ref, None, n, dk, first=h * n)
        s_ref[h, :, :dk] = sf
        s_ref[h, :, dk:] = sb


def _gla_scan_kernel(q_ref, k_ref, v_ref, g_ref, s0_ref, ng_ref, lo_ref, up_ref,
                     o_ref, oi_ref, qe_ref, ds_ref, e_ref, sc_ref, *, chunk, block):
    t = q_ref.shape[0]
    dk = q_ref.shape[1]
    per = block // chunk

    group = GLA_BLOCKS_PER_TRIP if (t // block) % GLA_BLOCKS_PER_TRIP == 0 else 1

    def terms(i, carry):
        for u in range(group):
            blk = i * group + u
            rows = pl.ds(pl.multiple_of(blk * block, block), block)
            v = v_ref[rows, :]
            kl, etot, qe, oi = _block_terms(q_ref[rows, :], k_ref[rows, :], v, g_ref[rows, :],
                                            lo_ref[...], up_ref[...], chunk, True)
            _store_chunk_states(v, kl, etot, ds_ref, e_ref, blk * per, chunk)
            qe_ref[rows, :] = qe
            oi_ref[rows, :] = oi
        return carry

    lax.fori_loop(0, t // (block * group), terms, 0)
    _scan_states(s0_ref[...], ds_ref, e_ref, sc_ref, t // chunk, dk)

    fgroup = GLA_FINISH_BLOCKS_PER_TRIP if (t // block) % GLA_FINISH_BLOCKS_PER_TRIP == 0 else 1

    def finish(i, carry):
        for u in range(fgroup):
            blk = i * fgroup + u
            rows = pl.ds(pl.multiple_of(blk * block, block), block)
            inter = [_dot(qe_ref[pl.ds(pl.multiple_of(blk * block + c * chunk, chunk), chunk), :],
                          sc_ref[blk * per + c], _NT) for c in range(per)]
            o = oi_ref[rows, :] + jnp.concatenate(inter, axis=0)
            o_ref[rows, :] = _rmsnorm(o, ng_ref[...]).astype(o_ref.dtype)
        return carry

    lax.fori_loop(0, t // (block * fgroup), finish, 0)


def _gla_state(k, v, g, chunk, block):
    bsz, t, kdim = k.shape
    vdim = v.shape[-1]
    nh = GLA_HEADS
    dk, dv = kdim // nh, vdim // nh
    block = min(block, t)
    n = t // chunk
    lo, up = _block_masks(block, chunk)
    mask_spec = pl.BlockSpec((block, block), lambda b: (0, 0))
    seq = lambda w: pl.BlockSpec((None, t, w), lambda b: (b, 0, 0))
    return pl.pallas_call(
        functools.partial(_gla_state_kernel, chunk=chunk, block=block),
        out_shape=jax.ShapeDtypeStruct((bsz, nh, dv, 2 * dk), F32),
        grid=(bsz,),
        in_specs=[seq(kdim), seq(vdim), seq(2 * kdim), mask_spec, mask_spec],
        out_specs=pl.BlockSpec((None, nh, dv, 2 * dk), lambda b: (b, 0, 0, 0)),
        scratch_shapes=[pltpu.VMEM((nh * n, dv, 2 * dk), F32), pltpu.VMEM((nh * n, SUBLANES, 2 * dk), F32)],
        compiler_params=_cparams(("arbitrary",)),
        name="gla_state",
    )(k, v, g, lo, up)


def _gla_scan(q, k, v, g, s0, norm_g, chunk, block):
    bsz, t, kdim = k.shape
    vdim = v.shape[-1]
    nh = GLA_HEADS
    dk, dv = kdim // nh, vdim // nh
    block = min(block, t)
    n = t // chunk
    lo, up = _block_masks(block, chunk)
    kspec = pl.BlockSpec((None, t, dk), lambda b, h: (b, 0, h))
    vspec = pl.BlockSpec((None, t, dv), lambda b, h: (b, 0, h))
    mask_spec = pl.BlockSpec((block, block), lambda b, h: (0, 0))
    return pl.pallas_call(
        functools.partial(_gla_scan_kernel, chunk=chunk, block=block),
        out_shape=jax.ShapeDtypeStruct((bsz, t, vdim), BF16),
        grid=(bsz, nh),
        in_specs=[
            kspec, kspec, vspec,
            pl.BlockSpec((None, t, 2 * dk), lambda b, h: (b, 0, h)),
            pl.BlockSpec((None, None, dv, 2 * dk), lambda b, h: (b, h, 0, 0)),
            pl.BlockSpec((1, dv), lambda b, h: (0, 0)),
            mask_spec, mask_spec,
        ],
        out_specs=vspec,
        scratch_shapes=[
            pltpu.VMEM((t, dv), F32),
            pltpu.VMEM((t, 2 * dk), BF16),
            pltpu.VMEM((n, dv, 2 * dk), F32),
            pltpu.VMEM((n, SUBLANES, 2 * dk), F32),
            pltpu.VMEM((n, dv, 2 * dk), BF16),
        ],
        compiler_params=_cparams(("arbitrary", "arbitrary")),
        name="gla_scan",
    )(q, k, v, g, s0, norm_g, lo, up)


def _route(h2, wr_t, br, earlier_bf, count_ref):
    tm = h2.shape[0]
    logits = _dot3(wr_t, h2, _NT)
    scores = _sigmoid(logits)
    sel = scores + br
    row = [sel[e:e + 1, :] for e in range(N_EXPERTS)]
    picked = []
    for e in range(N_EXPERTS):
        g0 = (e // EXPERTS_PER_GROUP) * EXPERTS_PER_GROUP
        ahead_count = jnp.zeros(row[e].shape, jnp.int32)
        for j in range(g0, g0 + EXPERTS_PER_GROUP):
            if j == e:
                continue
            ahead = (row[j] >= row[e]) if j < e else (row[j] > row[e])
            ahead_count = ahead_count + ahead.astype(jnp.int32)
        picked.append(ahead_count < TOP_K)
    zero = jnp.zeros_like(row[0])
    gscore, pair, first_w, second_w = [], [], [], []
    for g in range(N_EXPERT_GROUPS):
        acc, pidx, fw, sw = zero, zero, zero, zero
        seen = None
        for a in range(EXPERTS_PER_GROUP):
            e = g * EXPERTS_PER_GROUP + a
            sc = scores[e:e + 1, :]
            acc = acc + jnp.where(picked[e], row[e], 0.0)
            if seen is None:
                is_first = picked[e]
            else:
                is_first = picked[e] & jnp.logical_not(seen)
                is_second = picked[e] & seen
                sw = sw + jnp.where(is_second, sc, 0.0)
                pidx = pidx + jnp.where(is_second, float(a), 0.0)
            fw = fw + jnp.where(is_first, sc, 0.0)
            if a < len(_PAIR_BASE):
                pidx = pidx + jnp.where(is_first, float(_PAIR_BASE[a] - a - 1), 0.0)
            seen = picked[e] if seen is None else (seen | picked[e])
        gscore.append(acc)
        pair.append(pidx)
        first_w.append(fw)
        second_w.append(sw)
    cls, wa, wb = zero, zero, zero
    for g in range(N_EXPERT_GROUPS):
        ok = None
        for j in range(N_EXPERT_GROUPS):
            if j == g:
                continue
            c = (gscore[g] > gscore[j]) if j < g else (gscore[g] >= gscore[j])
            ok = c if ok is None else (ok & c)
        cls = cls + jnp.where(ok, pair[g] + float(len(_PAIRS) * g), 0.0)
        wa = wa + jnp.where(ok, first_w[g], 0.0)
        wb = wb + jnp.where(ok, second_w[g], 0.0)
    denom = wa + wb
    wa = wa / denom
    wb = wb / denom

    cid = lax.broadcasted_iota(jnp.int32, (CLASS_ROWS, tm), 0).astype(F32)
    onehot = (cid == cls).astype(BF16)
    before = _dot(onehot, earlier_bf)
    oh = onehot.astype(F32)
    base = count_ref[...][:, 0:1]
    rank = jnp.sum(oh * (before + base), axis=0, keepdims=True)
    count_ref[...] = count_ref[...] + jnp.sum(oh, axis=1, keepdims=True)
    pad = jnp.zeros((INFO_ROWS - 4, tm), F32)
    return jnp.concatenate([cls, rank, wa, wb, pad], axis=0)


def _earlier_matrix(tm):
    i = np.arange(tm)
    return jnp.asarray(i[:, None] < i[None, :], BF16)


def _store_tiled_rows(ref, x):
    tm, d = x.shape
    per = d // LANES
    for c in range(per):
        ref[pl.ds(c, tm, stride=per), :] = x[:, c * LANES:(c + 1) * LANES]


def _load_tiled_rows(ref, tm):
    per = ref.shape[0] // tm
    return jnp.concatenate([ref[pl.ds(c, tm, stride=per), :] for c in range(per)], axis=1)


def _route_outputs(h2, wr_ref, br_ref, earlier_ref, count_ref, pay_ref, info_ref, counts_ref):
    first = (pl.program_id(0) == 0) & (pl.program_id(1) == 0)

    @pl.when(first)
    def _():
        count_ref[...] = jnp.zeros_like(count_ref)

    info = _route(h2, wr_ref[...], br_ref[...], earlier_ref[...], count_ref)
    _store_tiled_rows(pay_ref, h2)
    info_ref[...] = info
    counts_ref[...] = count_ref[...]


def _route_out_shapes(bsz, t, d):
    return (
        jax.ShapeDtypeStruct((bsz * t * (d // LANES), LANES), F32),
        jax.ShapeDtypeStruct((INFO_ROWS, bsz * t), F32),
        jax.ShapeDtypeStruct((CLASS_ROWS, LANES), F32),
    )


def _route_out_specs(tm, nt, d):
    return (
        pl.BlockSpec((tm * (d // LANES), LANES), lambda b, i, *_: (b * nt + i, 0)),
        pl.BlockSpec((INFO_ROWS, tm), lambda b, i, *_: (0, b * nt + i)),
        pl.BlockSpec((CLASS_ROWS, LANES), lambda b, i, *_: (0, 0)),
    )


def _plan(info, counts, n, tmoe):
    cls = info[0].astype(jnp.int32)
    rank = info[1].astype(jnp.int32)
    cnt = counts[:N_CLASSES, 0].astype(jnp.int32)
    padded = (cnt + tmoe - 1) // tmoe * tmoe
    ends = jnp.cumsum(padded)
    dest = (ends - padded)[cls] + rank
    n_tiles = n // tmoe + N_CLASSES
    rows = jnp.stack([jnp.arange(n, dtype=F32), info[2], info[3], jnp.zeros((n,), F32)], axis=1)
    srt = jnp.zeros((n_tiles * tmoe, 4), F32).at[dest].set(rows, unique_indices=True)
    tok = srt[:, 0].astype(jnp.int32)
    n_used = ends[-1] // tmoe
    tile = jnp.minimum(jnp.arange(n_tiles, dtype=jnp.int32), n_used - 1)
    tcls = jnp.sum((ends[None, :] <= (tile * tmoe)[:, None]).astype(jnp.int32), axis=1)
    group, pair = tcls // len(_PAIRS), tcls % len(_PAIRS)
    pa = jnp.asarray([p[0] for p in _PAIRS], jnp.int32)[pair]
    pb = jnp.asarray([p[1] for p in _PAIRS], jnp.int32)[pair]
    ea = group * EXPERTS_PER_GROUP + pa
    eb = group * EXPERTS_PER_GROUP + pb
    return dest, tok, srt, ea, eb, n_used.reshape(1).astype(jnp.int32)


def _post0_kernel(o_ref, r_ref, x_ref, g1_ref, sh2_ref, sc2_ref, n2_ref, wout_ref, wr_ref, br_ref,
                  earlier_ref, x1_ref, pay_ref, info_ref, counts_ref, count_ref):
    a = (o_ref[...].astype(F32) * _silu(r_ref[...].astype(F32))).astype(BF16)
    x1 = x_ref[...] + g1_ref[...] * _dot(a, wout_ref[...])
    x1_ref[...] = x1
    h2 = _norm_modulate(x1, n2_ref[...], sc2_ref[...], sh2_ref[...])
    _route_outputs(h2, wr_ref, br_ref, earlier_ref, count_ref, pay_ref, info_ref, counts_ref)


def _post0(o, r, x, mods4, layer, n2, wout, wr_t, br, tm):
    bsz, t, d = x.shape
    tm = min(tm, t)
    rb = lambda b: b
    tok = lambda w: pl.BlockSpec((None, tm, w), lambda b, i: (b, i, 0))
    full = lambda a: pl.BlockSpec(a.shape, lambda b, i: (0,) * a.ndim)
    nt = t // tm
    earlier = _earlier_matrix(tm)
    return pl.pallas_call(
        _post0_kernel,
        out_shape=(jax.ShapeDtypeStruct((bsz, t, d), F32),) + _route_out_shapes(bsz, t, d),
        grid=(bsz, nt),
        in_specs=[
            tok(d), tok(d), tok(d),
            _mod_spec(layer, 2, d, rb), _mod_spec(layer, 3, d, rb), _mod_spec(layer, 4, d, rb),
            full(n2), full(wout), full(wr_t), full(br), full(earlier),
        ],
        out_specs=(tok(d),) + _route_out_specs(tm, nt, d),
        scratch_shapes=[pltpu.VMEM((CLASS_ROWS, LANES), F32)],
        compiler_params=_cparams(("arbitrary", "arbitrary")),
        name="post0",
    )(o, r, x, mods4, mods4, mods4, n2, wout, wr_t, br, earlier)


def _row_group(i, per):
    return pl.ds(pl.multiple_of(i * per, per), per)


def _moe_kernel(tok_ref, ea_ref, eb_ref, used_ref, p_hbm, gw_ref, wa_ref, wb_ref, f_ref,
                pbuf0, pbuf1, gsem, *, d, tm):
    del ea_ref, eb_ref
    j = pl.program_id(0)
    n_tiles = pl.num_programs(0)
    used = used_ref[0]
    per = d // LANES
    pbufs = (pbuf0, pbuf1)

    def gather_row(tile, r, p, pred, lane):
        i = tok_ref[tile * tm + r]

        @pl.when(pred)
        def _():
            pltpu.make_async_copy(p_hbm.at[_row_group(i, per), :], pbufs[p].at[_row_group(r, per), :],
                                  gsem.at[p]).start(priority=lane)

    def gather_wait(p):
        pltpu.make_async_copy(p_hbm.at[pl.ds(0, tm * per), :], pbufs[p], gsem.at[p]).wait()

    @pl.when(j == 0)
    def _():
        def body(blk, carry):
            for u in range(DMA_UNROLL):
                gather_row(0, blk * DMA_UNROLL + u, 0, True, u % 2)
            return carry
        lax.fori_loop(0, tm // DMA_UNROLL, body, 0)

    def step(p, beside_matmuls):
        @pl.when(j < used)
        def _():
            has_next = j + 1 < used
            nxt = jnp.minimum(j + 1, n_tiles - 1)
            if beside_matmuls:
                gather_wait(p)
                for r in range(tm):
                    gather_row(nxt, r, 1 - p, has_next, r % 2)
            else:
                @pl.when(j >= 0)
                def _():
                    for r in range(tm):
                        gather_row(nxt, r, 1 - p, has_next, r % 2)
                gather_wait(p)
            h = _load_tiled_rows(pbufs[p], tm).astype(BF16)
            gw = gw_ref[...]

            def expert(w_ref):
                de = w_ref.shape[2]
                he = (_silu(_dot(h, w_ref[0])) * _dot(h, w_ref[1])).astype(BF16)
                return jnp.concatenate([_dot(he, w_ref[2, :de, :]), _dot(he, w_ref[2, de:, :])], axis=1)

            _store_tiled_rows(f_ref, gw[:, 1:2] * expert(wa_ref) + gw[:, 2:3] * expert(wb_ref))

    @pl.when(j % 2 == 0)
    def _():
        step(0, True)

    @pl.when(j % 2 == 1)
    def _():
        step(1, False)

    @pl.when(j >= used)
    def _():
        f_ref[...] = jnp.zeros(f_ref.shape, F32)


def _moe(payload, tok, srt, ea, eb, n_used, w_all, layer, tm):
    n_sorted = tok.shape[0]
    n_tiles = n_sorted // tm
    _, _, _, d, de = w_all.shape
    per = d // LANES
    wa = lambda j, tok, ea, eb, used: (layer, ea[j], 0, 0, 0)
    wb = lambda j, tok, ea, eb, used: (layer, eb[j], 0, 0, 0)
    return pl.pallas_call(
        functools.partial(_moe_kernel, d=d, tm=tm),
        out_shape=jax.ShapeDtypeStruct((n_sorted * per, LANES), F32),
        grid_spec=pltpu.PrefetchScalarGridSpec(
            num_scalar_prefetch=4,
            grid=(n_tiles,),
            in_specs=[
                pl.BlockSpec(memory_space=pl.ANY),
                pl.BlockSpec((tm, srt.shape[1]), lambda j, *_: (j, 0)),
                pl.BlockSpec((None, None, 3, d, de), wa), pl.BlockSpec((None, None, 3, d, de), wb),
            ],
            out_specs=pl.BlockSpec((tm * per, LANES), lambda j, *_: (j, 0)),
            scratch_shapes=[pltpu.VMEM((tm * per, LANES), F32), pltpu.VMEM((tm * per, LANES), F32),
                            pltpu.SemaphoreType.DMA((2,))],
        ),
        compiler_params=_cparams(("arbitrary",)),
        name="moe",
    )(tok, ea, eb, n_used, payload, srt, w_all, w_all)


def _expert_weights_kernel(w1_ref, w3_ref, w2_ref, o_ref):
    de = w2_ref.shape[0]
    o_ref[0] = w1_ref[...].astype(BF16)
    o_ref[1] = w3_ref[...].astype(BF16)
    o_ref[2, :de, :] = w2_ref[:, :de].astype(BF16)
    o_ref[2, de:, :] = w2_ref[:, de:].astype(BF16)


def _expert_weights(w1, w3, w2):
    nl, ne, d, de = w1.shape
    assert d == 2 * de
    up = pl.BlockSpec((None, None, d, de), lambda l, e: (l, e, 0, 0))
    return pl.pallas_call(
        _expert_weights_kernel,
        out_shape=jax.ShapeDtypeStruct((nl, ne, 3, d, de), BF16),
        grid=(nl, ne),
        in_specs=[up, up, pl.BlockSpec((None, None, de, d), lambda l, e: (l, e, 0, 0))],
        out_specs=pl.BlockSpec((None, None, 3, d, de), lambda l, e: (l, e, 0, 0, 0)),
        compiler_params=_cparams(("arbitrary", "arbitrary")),
        name="expert_weights",
    )(w1, w3, w2)


def _sparse_moe(payload, info, counts, w_all, layer):
    n = info.shape[1]
    tmoe = min(TM_MOE, n)
    dest, tok, srt, ea, eb, n_used = _plan(info, counts, n, tmoe)
    return _moe(payload, tok, srt, ea, eb, n_used, w_all, layer, tmoe), dest


def _gather_rows_start(dest_ref, f_hbm, fbuf, fsem, tile, tm, per, pred, unrolled):
    def one(r, lane):
        i = dest_ref[tile * tm + r]

        @pl.when(pred)
        def _():
            pltpu.make_async_copy(f_hbm.at[_row_group(i, per), :], fbuf.at[_row_group(r, per), :],
                                  fsem).start(priority=lane)

    if unrolled:
        for r in range(tm):
            one(r, r % 2)
    else:
        def body(blk, carry):
            for u in range(DMA_UNROLL):
                one(blk * DMA_UNROLL + u, u % 2)
            return carry
        lax.fori_loop(0, tm // DMA_UNROLL, body, 0)


def _gather_rows_wait(f_hbm, fbuf, fsem):
    pltpu.make_async_copy(f_hbm.at[pl.ds(0, fbuf.shape[0]), :], fbuf, fsem).wait()


def _pool_constants(tm, d):
    ng = len(POOL_WINDOWS)
    cg = d // ng
    pos = np.arange(tm)
    seg, off = pos // GRID_W, pos % GRID_W
    mats = np.zeros((ng, tm, tm), np.float32)
    inv = np.zeros((tm, d), np.float32)
    for gi, w in enumerate(POOL_WINDOWS):
        lo = np.clip(off - w // 2, 0, GRID_W)
        hi = np.clip(off - w // 2 + w, 0, GRID_W)
        same = seg[:, None] == seg[None, :]
        inside = (off[None, :] >= lo[:, None]) & (off[None, :] < hi[:, None])
        mats[gi] = (same & inside).astype(np.float32)
        inv[:, gi * cg:(gi + 1) * cg] = (1.0 / (hi - lo).astype(np.float32))[:, None]
    return jnp.asarray(mats, BF16), jnp.asarray(inv, F32)


def _layer1_kernel(dest_ref, x_ref, f_hbm, g2p_ref, sh1_ref, sc1_ref, g1_ref, sh2_ref, sc2_ref,
                   n1_ref, n2_ref, pm_ref, inv_ref, wp_ref, bp_ref, ps_ref, wr_ref, br_ref, earlier_ref,
                   x3_ref, pay_ref, info_ref, counts_ref, fbuf, fsem, x2_ref, count_ref):
    tm, d = x_ref.shape
    per = d // LANES
    nt = pl.num_programs(1)
    lin = pl.program_id(0) * nt + pl.program_id(1)

    @pl.when(lin == 0)
    def _():
        _gather_rows_start(dest_ref, f_hbm, fbuf, fsem, 0, tm, per, True, False)

    @pl.when(lin >= 0)
    def _():
        _gather_rows_wait(f_hbm, fbuf, fsem)
        x2_ref[...] = x_ref[...] + g2p_ref[...] * _load_tiled_rows(fbuf, tm)

    total = pl.num_programs(0) * nt
    _gather_rows_start(dest_ref, f_hbm, fbuf, fsem, jnp.minimum(lin + 1, total - 1), tm, per,
                       lin + 1 < total, True)
    x2 = x2_ref[...]
    h = _norm_modulate(x2, n1_ref[...], sc1_ref[...], sh1_ref[...])
    hb = h.astype(BF16)
    ng = pm_ref.shape[0]
    cg = h.shape[1] // ng
    ys = []
    for gi in range(ng):
        cs = slice(gi * cg, (gi + 1) * cg)
        wsum = _dot(pm_ref[gi], hb[:, cs])
        pooled = wsum * inv_ref[:, cs] - h[:, cs]
        ys.append(_dot(pooled.astype(BF16), wp_ref[gi]))
    y = (jnp.concatenate(ys, axis=1) + bp_ref[...]) * ps_ref[...]
    x3 = x2 + g1_ref[...] * y
    x3_ref[...] = x3
    h2 = _norm_modulate(x3, n2_ref[...], sc2_ref[...], sh2_ref[...])
    _route_outputs(h2, wr_ref, br_ref, earlier_ref, count_ref, pay_ref, info_ref, counts_ref)


def _layer1(x1, f_sorted, dest, mods4, n1, n2, wp, bp, ps, wr_t, br, tm):
    bsz, t, d = x1.shape
    tm = min(tm, t)
    pm, inv = _pool_constants(tm, d)
    earlier = _earlier_matrix(tm)
    rb = lambda b: b
    tok = lambda w: pl.BlockSpec((None, tm, w), lambda b, i, dest: (b, i, 0))
    full = lambda a: pl.BlockSpec(a.shape, lambda b, i, dest: (0,) * a.ndim)
    nt = t // tm
    return pl.pallas_call(
        _layer1_kernel,
        out_shape=(jax.ShapeDtypeStruct((bsz, t, d), F32),) + _route_out_shapes(bsz, t, d),
        grid_spec=pltpu.PrefetchScalarGridSpec(
            num_scalar_prefetch=1,
            grid=(bsz, nt),
            in_specs=[
                tok(d), pl.BlockSpec(memory_space=pl.ANY),
                _mod_spec(0, 5, d, rb),
                _mod_spec(1, 0, d, rb), _mod_spec(1, 1, d, rb), _mod_spec(1, 2, d, rb),
                _mod_spec(1, 3, d, rb), _mod_spec(1, 4, d, rb),
                full(n1), full(n2), full(pm), full(inv), full(wp), full(bp), full(ps),
                full(wr_t), full(br), full(earlier),
            ],
            out_specs=(tok(d),) + _route_out_specs(tm, nt, d),
            scratch_shapes=[pltpu.VMEM((tm * (d // LANES), LANES), F32), pltpu.SemaphoreType.DMA(()),
                            pltpu.VMEM((tm, d), F32), pltpu.VMEM((CLASS_ROWS, LANES), F32)],
        ),
        compiler_params=_cparams(("arbitrary", "arbitrary")),
        name="layer1",
    )(dest, x1, f_sorted, mods4, mods4, mods4, mods4, mods4, mods4, n1, n2, pm, inv, wp, bp, ps,
      wr_t, br, earlier)


def _final_kernel(dest_ref, x_ref, f_hbm, g2_ref, fg_ref, o_ref, fbuf, fsem):
    tm, d = x_ref.shape
    per = d // LANES
    nt = pl.num_programs(1)
    lin = pl.program_id(0) * nt + pl.program_id(1)

    @pl.when(lin == 0)
    def _():
        _gather_rows_start(dest_ref, f_hbm, fbuf, fsem, 0, tm, per, True, False)

    @pl.when(lin >= 0)
    def _():
        _gather_rows_wait(f_hbm, fbuf, fsem)
        f = _load_tiled_rows(fbuf, tm)
        o_ref[...] = _rmsnorm(x_ref[...] + g2_ref[...] * f, fg_ref[...])

    total = pl.num_programs(0) * nt
    _gather_rows_start(dest_ref, f_hbm, fbuf, fsem, jnp.minimum(lin + 1, total - 1), tm, per,
                       lin + 1 < total, False)


def _final(x3, f_sorted, dest, mods4, fg, tm):
    bsz, t, d = x3.shape
    tm = min(tm, t)
    tok = pl.BlockSpec((None, tm, d), lambda b, i, dest: (b, i, 0))
    return pl.pallas_call(
        _final_kernel,
        out_shape=jax.ShapeDtypeStruct((bsz, t, d), F32),
        grid_spec=pltpu.PrefetchScalarGridSpec(
            num_scalar_prefetch=1,
            grid=(bsz, t // tm),
            in_specs=[tok, pl.BlockSpec(memory_space=pl.ANY), _mod_spec(1, 5, d, lambda b: b),
                      pl.BlockSpec((1, d), lambda b, i, dest: (0, 0))],
            out_specs=tok,
            scratch_shapes=[pltpu.VMEM((tm * (d // LANES), LANES), F32), pltpu.SemaphoreType.DMA(())],
        ),
        compiler_params=_cparams(("arbitrary", "arbitrary")),
        name="final",
    )(dest, x3, f_sorted, mods4, fg)


def kernel(x, c, ctx, c_ctx, norm1_g, norm2_g, w_mod, b_mod, gla_w_in, gla_w_gate_a, gla_w_gate_b,
           gla_b_gate, gla_norm_g, gla_w_out, pool_w, pool_b, pool_scale, w_router, b_router,
           w_gate_e, w_up_e, w_down_e, final_g):
    bsz, t, d = x.shape
    depth = w_mod.shape[0]
    assert depth == 2 and t % GRID_W == 0
    kdim = d // 2
    row = lambda a: a.reshape(1, -1)

    c_rows = -(-(bsz + 1) // 8) * 8
    c_all = jnp.zeros((c_rows, d), F32).at[:bsz].set(c).at[bsz].set(c_ctx)
    mods = _modulation(c_all, w_mod, b_mod)
    mods4 = mods.reshape(depth, c_rows, 1, 6 * d)

    win = gla_w_in[0].astype(BF16)
    wga = jnp.concatenate([gla_w_gate_a[0, 0], gla_w_gate_a[0, 1]], axis=1).astype(BF16)
    dk = kdim // GLA_HEADS
    by_head = lambda a: a.reshape(a.shape[0], GLA_HEADS, 1, dk)
    zero = jnp.zeros((GLA_GATE_RANK, GLA_HEADS, 1, dk), F32)
    wgb = jnp.concatenate([
        jnp.concatenate([by_head(gla_w_gate_b[0, 0]), zero], axis=2),
        jnp.concatenate([zero, by_head(gla_w_gate_b[0, 1])], axis=2)], axis=0)
    wgb = wgb.reshape(2 * GLA_GATE_RANK, 2 * kdim).astype(BF16)
    bg = jnp.stack([gla_b_gate[0, 0].reshape(GLA_HEADS, dk), gla_b_gate[0, 1].reshape(GLA_HEADS, dk)],
                   axis=1).reshape(1, 2 * kdim)
    n1_0 = row(norm1_g[0])
    q, k, v, r, g = _gla_in(x, mods4, 0, lambda b: b, n1_0, win, wga, wgb, bg, TM_PROJ)
    kc, vc, gc = _gla_in(ctx, mods4, 0, lambda b: bsz, n1_0, win, wga, wgb, bg, TM_CTX, state_only=True)
    s0 = _gla_state(kc, vc, gc, GLA_CHUNK, GLA_BLOCK)
    o = _gla_scan(q, k, v, g, s0, row(gla_norm_g[0]), GLA_CHUNK, GLA_BLOCK)

    wr_t = jnp.transpose(w_router)
    br = b_router.reshape(N_EXPERTS, 1)
    x1, payload, info, counts = _post0(o, r, x, mods4, 0, row(norm2_g[0]),
                                       gla_w_out[0].astype(BF16), wr_t, br, TM_POST)
    w_experts = _expert_weights(w_gate_e, w_up_e, w_down_e)
    f0, dest0 = _sparse_moe(payload, info, counts, w_experts, 0)

    x3, payload, info, counts = _layer1(x1, f0, dest0, mods4, row(norm1_g[1]), row(norm2_g[1]),
                                        pool_w[0].astype(BF16), row(pool_b[0]), row(pool_scale[0]),
                                        wr_t, br, TM_POOL)
    f1, dest1 = _sparse_moe(payload, info, counts, w_experts, 1)
    return _final(x3, f1, dest1, mods4, row(final_g), TM_FINAL)
```

```python
import functools

import numpy as np
import jax
import jax.numpy as jnp
from jax import lax
from jax.experimental import pallas as pl
from jax.experimental.pallas import tpu as pltpu

EPS = 1e-6
GRID_W = 64
GLA_HEADS = 4
GLA_GATE_RANK = 16
GLA_GATE_NORM = 16.0
GLA_CHUNK = 128
GLA_BLOCK = 256
GLA_BLOCKS_PER_TRIP = 8
GLA_FINISH_BLOCKS_PER_TRIP = 8
POOL_WINDOWS = (2, 4, 8, 16)
N_EXPERTS = 16
N_EXPERT_GROUPS = 4
EXPERTS_PER_GROUP = N_EXPERTS // N_EXPERT_GROUPS
TOP_K = 2

_PAIRS = tuple((a, b) for a in range(EXPERTS_PER_GROUP) for b in range(a + 1, EXPERTS_PER_GROUP))
_PAIR_BASE = (0, 3, 5)
N_CLASSES = N_EXPERT_GROUPS * len(_PAIRS)
CLASS_ROWS = 32
INFO_ROWS = 8
LANES = 128
SUBLANES = 8

TM_PROJ = 1024
TM_CTX = 256
TM_POST = 1024
TM_POOL = 1024
TM_MOE = 256
TM_FINAL = 2048
DMA_UNROLL = 8

F32 = jnp.float32
BF16 = jnp.bfloat16

_NT = (((1,), (1,)), ((), ()))
_TN = (((0,), (0,)), ((), ()))
_VMEM_LIMIT = 56 * 1024 * 1024


def _cparams(sem):
    return pltpu.CompilerParams(dimension_semantics=sem, vmem_limit_bytes=_VMEM_LIMIT)


def _dot(a, b, dims=None):
    if dims is None:
        return jnp.dot(a, b, preferred_element_type=F32)
    return lax.dot_general(a, b, dims, preferred_element_type=F32)


def _split(a):
    hi = a.astype(BF16)
    lo = (a - hi.astype(F32)).astype(BF16)
    return hi, lo


def _dot3(a, b, dims=None):
    ah, al = _split(a)
    bh, bl = _split(b)
    return _dot(ah, bh, dims) + _dot(ah, bl, dims) + _dot(al, bh, dims)


def _sigmoid(x):
    return 1.0 / (1.0 + jnp.exp(-x))


def _silu(x):
    return x * _sigmoid(x)


def _rmsnorm(xf, g):
    return xf * lax.rsqrt(jnp.mean(xf * xf, axis=-1, keepdims=True) + EPS) * g


def _norm_modulate(xf, g, scale, shift):
    return xf * lax.rsqrt(jnp.mean(xf * xf, axis=-1, keepdims=True) + EPS) * (g * (1.0 + scale)) + shift


def _mod_kernel(c_ref, w_ref, b_ref, o_ref):
    o_ref[...] = _dot3(_silu(c_ref[...]), w_ref[...]) + b_ref[...]


def _modulation(c_all, w_mod, b_mod):
    depth, d, d6 = w_mod.shape
    rows = c_all.shape[0]
    tn = 1536
    return pl.pallas_call(
        _mod_kernel,
        out_shape=jax.ShapeDtypeStruct((depth, rows, d6), F32),
        grid=(depth, d6 // tn),
        in_specs=[
            pl.BlockSpec((rows, d), lambda l, j: (0, 0)),
            pl.BlockSpec((None, d, tn), lambda l, j: (l, 0, j)),
            pl.BlockSpec((None, 1, tn), lambda l, j: (l, 0, j)),
        ],
        out_specs=pl.BlockSpec((None, rows, tn), lambda l, j: (l, 0, j)),
        compiler_params=_cparams(("arbitrary", "arbitrary")),
        name="mod",
    )(c_all, w_mod, b_mod.reshape(depth, 1, d6))


def _mod_spec(layer, chunk, d, row_of_batch):
    return pl.BlockSpec((None, None, 1, d), lambda b, t, *_: (layer, row_of_batch(b), 0, chunk))


def _gla_in_kernel(x_ref, sh_ref, sc_ref, ng_ref, win_ref, wga_ref, wgb_ref, bg_ref, *out_refs,
                   kdim, vdim, qscale, state_only):
    hb = _norm_modulate(x_ref[...], ng_ref[...], sc_ref[...], sh_ref[...]).astype(BF16)
    if state_only:
        k_ref, v_ref, g_ref = out_refs
        proj = _dot(hb, win_ref[:, kdim:2 * kdim + vdim])
        k_ref[...] = proj[:, :kdim].astype(BF16)
        v_ref[...] = proj[:, kdim:].astype(BF16)
    else:
        q_ref, k_ref, v_ref, r_ref, g_ref = out_refs
        proj = _dot(hb, win_ref[...])
        q_ref[...] = (proj[:, :kdim] * qscale).astype(BF16)
        k_ref[...] = proj[:, kdim:2 * kdim].astype(BF16)
        v_ref[...] = proj[:, 2 * kdim:2 * kdim + vdim].astype(BF16)
        r_ref[...] = proj[:, 2 * kdim + vdim:].astype(BF16)
    low = _dot(hb, wga_ref[...])
    z = _dot(low.astype(BF16), wgb_ref[...]) + bg_ref[...]
    logsig = jnp.minimum(z, 0.0) - jnp.log(1.0 + jnp.exp(-jnp.abs(z)))
    g_ref[...] = logsig * (1.0 / GLA_GATE_NORM)


def _gla_in(x, mods4, layer, row_of_batch, ng, win, wga, wgb, bg, tm, state_only=False):
    bsz, t, d = x.shape
    kdim, vdim = d // 2, d
    tm = min(tm, t)
    tok = lambda w: pl.BlockSpec((None, tm, w), lambda b, i: (b, i, 0))
    full = lambda a: pl.BlockSpec(a.shape, lambda b, i: (0,) * a.ndim)
    kern = functools.partial(_gla_in_kernel, kdim=kdim, vdim=vdim,
                             qscale=float((kdim // GLA_HEADS) ** -0.5), state_only=state_only)
    widths = (kdim, vdim, 2 * kdim) if state_only else (kdim, kdim, vdim, vdim, 2 * kdim)
    dtypes = (BF16,) * (len(widths) - 1) + (F32,)
    return pl.pallas_call(
        kern,
        out_shape=tuple(jax.ShapeDtypeStruct((bsz, t, w), dt) for w, dt in zip(widths, dtypes)),
        grid=(bsz, t // tm),
        in_specs=[
            tok(d),
            _mod_spec(layer, 0, d, row_of_batch),
            _mod_spec(layer, 1, d, row_of_batch),
            full(ng), full(win), full(wga), full(wgb), full(bg),
        ],
        out_specs=tuple(tok(w) for w in widths),
        compiler_params=_cparams(("arbitrary", "arbitrary")),
        name="gla_in",
    )(x, mods4, mods4, ng, win, wga, wgb, bg)


def _block_masks(rows, chunk):
    i = np.arange(rows)
    same = (i[:, None] // chunk) == (i[None, :] // chunk)
    lower = same & (i[:, None] >= i[None, :])
    upper = same & (i[:, None] <= i[None, :])
    return jnp.asarray(lower, BF16), jnp.asarray(upper, BF16)


def _per_chunk_row(a, chunk, r):
    rows, w = a.shape
    parts = [jnp.broadcast_to(a[c * chunk + r:c * chunk + r + 1, :], (chunk, w))
             for c in range(rows // chunk)]
    return jnp.concatenate(parts, axis=0)


def _block_terms(q, k, v, g2, lo, up, chunk, want_out):
    dk = g2.shape[1] // 2
    gh, gl = _split(g2)
    pre = _dot(lo, gh) + _dot(lo, gl)
    tot = _per_chunk_row(pre, chunk, chunk - 1)
    fwd = lax.broadcasted_iota(jnp.int32, g2.shape, 1) < dk
    cum = jnp.where(fwd, pre, tot - pre + g2)
    kf = k.astype(F32)
    kf2 = jnp.concatenate([kf, kf], axis=1)
    kl = (kf2 * jnp.exp(tot - cum)).astype(BF16)
    etot = jnp.exp(tot)
    if not want_out:
        return kl, etot, None, None
    mid = _per_chunk_row(cum, chunk, chunk // 2)
    qf = q.astype(F32)
    qf2 = jnp.concatenate([qf, qf], axis=1)
    qe = (qf2 * jnp.exp(cum)).astype(BF16)
    qi = (qf2 * jnp.exp(cum - mid)).astype(BF16)
    ki = (kf2 * jnp.exp(mid - cum)).astype(BF16)
    s = (jnp.where(lo > 0, _dot(qi[:, :dk], ki[:, :dk], _NT), 0.0)
         + jnp.where(up > 0, _dot(qi[:, dk:], ki[:, dk:], _NT), 0.0))
    return kl, etot, qe, _dot(s.astype(BF16), v)


def _store_chunk_states(v, kl, etot, ds_ref, e_ref, first_chunk, chunk):
    for c in range(v.shape[0] // chunk):
        rows = slice(c * chunk, (c + 1) * chunk)
        ds_ref[first_chunk + c] = _dot(v[rows, :], kl[rows, :], _TN)
        e_ref[first_chunk + c] = etot[c * chunk:c * chunk + SUBLANES, :]


def _scan_states(s0, ds_ref, e_ref, sc_ref, n, dk, first=0):
    def body(c, carry):
        sf, sb = carry
        r = n - 1 - c
        if sc_ref is not None:
            sc_ref[first + c, :, :dk] = sf.astype(BF16)
            sc_ref[first + r, :, dk:] = sb.astype(BF16)
        return (sf * e_ref[first + c, 0:1, :dk] + ds_ref[first + c, :, :dk],
                sb * e_ref[first + r, 0:1, dk:] + ds_ref[first + r, :, dk:])

    return lax.fori_loop(0, n, body, (s0[:, :dk], s0[:, dk:]))


def _gla_state_kernel(k_ref, v_ref, g_ref, lo_ref, up_ref, s_ref, ds_ref, e_ref, *, chunk, block):
    nh, dv = s_ref.shape[0], s_ref.shape[1]
    t = k_ref.shape[0]
    dk = k_ref.shape[1] // nh
    n = t // chunk
    per = block // chunk

    def terms(i, carry):
        rows = pl.ds(pl.multiple_of(i * block, block), block)
        for h in range(nh):
            v = v_ref[rows, h * dv:(h + 1) * dv]
            kl, etot, _, _ = _block_terms(None, k_ref[rows, h * dk:(h + 1) * dk], v,
                                          g_ref[rows, 2 * h * dk:2 * (h + 1) * dk],
                                          lo_ref[...], up_ref[...], chunk, False)
            _store_chunk_states(v, kl, etot, ds_ref, e_ref, h * n + i * per, chunk)
        return carry

    lax.fori_loop(0, t // block, terms, 0)
    for h in range(nh):
        sf, sb = _scan_states(jnp.zeros((dv, 2 * dk), F32), ds_ref, e_ref, None, n, dk, first=h * n)
        s_ref[h, :, :dk] = sf
        s_ref[h, :, dk:] = sb


def _gla_scan_kernel(q_ref, k_ref, v_ref, g_ref, s0_ref, ng_ref, lo_ref, up_ref,
                     o_ref, oi_ref, qe_ref, ds_ref, e_ref, sc_ref, *, chunk, block):
    t = q_ref.shape[0]
    dk = q_ref.shape[1]
    per = block // chunk

    group = GLA_BLOCKS_PER_TRIP if (t // block) % GLA_BLOCKS_PER_TRIP == 0 else 1

    def terms(i, carry):
        for u in range(group):
            blk = i * group + u
            rows = pl.ds(pl.multiple_of(blk * block, block), block)
            v = v_ref[rows, :]
            kl, etot, qe, oi = _block_terms(q_ref[rows, :], k_ref[rows, :], v, g_ref[rows, :],
                                            lo_ref[...], up_ref[...], chunk, True)
            _store_chunk_states(v, kl, etot, ds_ref, e_ref, blk * per, chunk)
            qe_ref[rows, :] = qe
            oi_ref[rows, :] = oi
        return carry

    lax.fori_loop(0, t // (block * group), terms, 0)
    _scan_states(s0_ref[...], ds_ref, e_ref, sc_ref, t // chunk, dk)

    fgroup = GLA_FINISH_BLOCKS_PER_TRIP if (t // block) % GLA_FINISH_BLOCKS_PER_TRIP == 0 else 1

    def finish(i, carry):
        for u in range(fgroup):
            blk = i * fgroup + u
            rows = pl.ds(pl.multiple_of(blk * block, block), block)
            inter = [_dot(qe_ref[pl.ds(pl.multiple_of(blk * block + c * chunk, chunk), chunk), :],
                          sc_ref[blk * per + c], _NT) for c in range(per)]
            o = oi_ref[rows, :] + jnp.concatenate(inter, axis=0)
            o_ref[rows, :] = _rmsnorm(o, ng_ref[...]).astype(o_ref.dtype)
        return carry

    lax.fori_loop(0, t // (block * fgroup), finish, 0)


def _gla_state(k, v, g, chunk, block):
    bsz, t, kdim = k.shape
    vdim = v.shape[-1]
    nh = GLA_HEADS
    dk, dv = kdim // nh, vdim // nh
    block = min(block, t)
    n = t // chunk
    lo, up = _block_masks(block, chunk)
    mask_spec = pl.BlockSpec((block, block), lambda b: (0, 0))
    seq = lambda w: pl.BlockSpec((None, t, w), lambda b: (b, 0, 0))
    return pl.pallas_call(
        functools.partial(_gla_state_kernel, chunk=chunk, block=block),
        out_shape=jax.ShapeDtypeStruct((bsz, nh, dv, 2 * dk), F32),
        grid=(bsz,),
        in_specs=[seq(kdim), seq(vdim), seq(2 * kdim), mask_spec, mask_spec],
        out_specs=pl.BlockSpec((None, nh, dv, 2 * dk), lambda b: (b, 0, 0, 0)),
        scratch_shapes=[pltpu.VMEM((nh * n, dv, 2 * dk), F32), pltpu.VMEM((nh * n, SUBLANES, 2 * dk), F32)],
        compiler_params=_cparams(("arbitrary",)),
        name="gla_state",
    )(k, v, g, lo, up)


def _gla_scan(q, k, v, g, s0, norm_g, chunk, block):
    bsz, t, kdim = k.shape
    vdim = v.shape[-1]
    nh = GLA_HEADS
    dk, dv = kdim // nh, vdim // nh
    block = min(block, t)
    n = t // chunk
    lo, up = _block_masks(block, chunk)
    kspec = pl.BlockSpec((None, t, dk), lambda b, h: (b, 0, h))
    vspec = pl.BlockSpec((None, t, dv), lambda b, h: (b, 0, h))
    mask_spec = pl.BlockSpec((block, block), lambda b, h: (0, 0))
    return pl.pallas_call(
        functools.partial(_gla_scan_kernel, chunk=chunk, block=block),
        out_shape=jax.ShapeDtypeStruct((bsz, t, vdim), BF16),
        grid=(bsz, nh),
        in_specs=[
            kspec, kspec, vspec,
            pl.BlockSpec((None, t, 2 * dk), lambda b, h: (b, 0, h)),
            pl.BlockSpec((None, None, dv, 2 * dk), lambda b, h: (b, h, 0, 0)),
            pl.BlockSpec((1, dv), lambda b, h: (0, 0)),
            mask_spec, mask_spec,
        ],
        out_specs=vspec,
        scratch_shapes=[
            pltpu.VMEM((t, dv), F32),
            pltpu.VMEM((t, 2 * dk), BF16),
            pltpu.VMEM((n, dv, 2 * dk), F32),
            pltpu.VMEM((n, SUBLANES, 2 * dk), F32),
            pltpu.VMEM((n, dv, 2 * dk), BF16),
        ],
        compiler_params=_cparams(("arbitrary", "arbitrary")),
        name="gla_scan",
    )(q, k, v, g, s0, norm_g, lo, up)


def _route(h2, wr_t, br, earlier_bf, count_ref):
    tm = h2.shape[0]
    logits = _dot3(wr_t, h2, _NT)
    scores = _sigmoid(logits)
    sel = scores + br
    row = [sel[e:e + 1, :] for e in range(N_EXPERTS)]
    picked = []
    for e in range(N_EXPERTS):
        g0 = (e // EXPERTS_PER_GROUP) * EXPERTS_PER_GROUP
        ahead_count = jnp.zeros(row[e].shape, jnp.int32)
        for j in range(g0, g0 + EXPERTS_PER_GROUP):
            if j == e:
                continue
            ahead = (row[j] >= row[e]) if j < e else (row[j] > row[e])
            ahead_count = ahead_count + ahead.astype(jnp.int32)
        picked.append(ahead_count < TOP_K)
    zero = jnp.zeros_like(row[0])
    gscore, pair, first_w, second_w = [], [], [], []
    for g in range(N_EXPERT_GROUPS):
        acc, pidx, fw, sw = zero, zero, zero, zero
        seen = None
        for a in range(EXPERTS_PER_GROUP):
            e = g * EXPERTS_PER_GROUP + a
            sc = scores[e:e + 1, :]
            acc = acc + jnp.where(picked[e], row[e], 0.0)
            if seen is None:
                is_first = picked[e]
            else:
                is_first = picked[e] & jnp.logical_not(seen)
                is_second = picked[e] & seen
                sw = sw + jnp.where(is_second, sc, 0.0)
                pidx = pidx + jnp.where(is_second, float(a), 0.0)
            fw = fw + jnp.where(is_first, sc, 0.0)
            if a < len(_PAIR_BASE):
                pidx = pidx + jnp.where(is_first, float(_PAIR_BASE[a] - a - 1), 0.0)
            seen = picked[e] if seen is None else (seen | picked[e])
        gscore.append(acc)
        pair.append(pidx)
        first_w.append(fw)
        second_w.append(sw)
    cls, wa, wb = zero, zero, zero
    for g in range(N_EXPERT_GROUPS):
        ok = None
        for j in range(N_EXPERT_GROUPS):
            if j == g:
                continue
            c = (gscore[g] > gscore[j]) if j < g else (gscore[g] >= gscore[j])
            ok = c if ok is None else (ok & c)
        cls = cls + jnp.where(ok, pair[g] + float(len(_PAIRS) * g), 0.0)
        wa = wa + jnp.where(ok, first_w[g], 0.0)
        wb = wb + jnp.where(ok, second_w[g], 0.0)
    denom = wa + wb
    wa = wa / denom
    wb = wb / denom

    cid = lax.broadcasted_iota(jnp.int32, (CLASS_ROWS, tm), 0).astype(F32)
    onehot = (cid == cls).astype(BF16)
    before = _dot(onehot, earlier_bf)
    oh = onehot.astype(F32)
    base = count_ref[...][:, 0:1]
    rank = jnp.sum(oh * (before + base), axis=0, keepdims=True)
    count_ref[...] = count_ref[...] + jnp.sum(oh, axis=1, keepdims=True)
    pad = jnp.zeros((INFO_ROWS - 4, tm), F32)
    return jnp.concatenate([cls, rank, wa, wb, pad], axis=0)


def _earlier_matrix(tm):
    i = np.arange(tm)
    return jnp.asarray(i[:, None] < i[None, :], BF16)


def _store_tiled_rows(ref, x):
    tm, d = x.shape
    per = d // LANES
    for c in range(per):
        ref[pl.ds(c, tm, stride=per), :] = x[:, c * LANES:(c + 1) * LANES]


def _load_tiled_rows(ref, tm):
    per = ref.shape[0] // tm
    return jnp.concatenate([ref[pl.ds(c, tm, stride=per), :] for c in range(per)], axis=1)


def _route_outputs(h2, wr_ref, br_ref, earlier_ref, count_ref, pay_ref, info_ref, counts_ref):
    first = (pl.program_id(0) == 0) & (pl.program_id(1) == 0)

    @pl.when(first)
    def _():
        count_ref[...] = jnp.zeros_like(count_ref)

    info = _route(h2, wr_ref[...], br_ref[...], earlier_ref[...], count_ref)
    _store_tiled_rows(pay_ref, h2)
    info_ref[...] = info
    counts_ref[...] = count_ref[...]


def _route_out_shapes(bsz, t, d):
    return (
        jax.ShapeDtypeStruct((bsz * t * (d // LANES), LANES), F32),
        jax.ShapeDtypeStruct((INFO_ROWS, bsz * t), F32),
        jax.ShapeDtypeStruct((CLASS_ROWS, LANES), F32),
    )


def _route_out_specs(tm, nt, d):
    return (
        pl.BlockSpec((tm * (d // LANES), LANES), lambda b, i, *_: (b * nt + i, 0)),
        pl.BlockSpec((INFO_ROWS, tm), lambda b, i, *_: (0, b * nt + i)),
        pl.BlockSpec((CLASS_ROWS, LANES), lambda b, i, *_: (0, 0)),
    )


def _plan(info, counts, n, tmoe):
    cls = info[0].astype(jnp.int32)
    rank = info[1].astype(jnp.int32)
    cnt = counts[:N_CLASSES, 0].astype(jnp.int32)
    padded = (cnt + tmoe - 1) // tmoe * tmoe
    ends = jnp.cumsum(padded)
    dest = (ends - padded)[cls] + rank
    n_tiles = n // tmoe + N_CLASSES
    rows = jnp.stack([jnp.arange(n, dtype=F32), info[2], info[3], jnp.zeros((n,), F32)], axis=1)
    srt = jnp.zeros((n_tiles * tmoe, 4), F32).at[dest].set(rows, unique_indices=True)
    tok = srt[:, 0].astype(jnp.int32)
    n_used = ends[-1] // tmoe
    tile = jnp.minimum(jnp.arange(n_tiles, dtype=jnp.int32), n_used - 1)
    tcls = jnp.sum((ends[None, :] <= (tile * tmoe)[:, None]).astype(jnp.int32), axis=1)
    group, pair = tcls // len(_PAIRS), tcls % len(_PAIRS)
    pa = jnp.asarray([p[0] for p in _PAIRS], jnp.int32)[pair]
    pb = jnp.asarray([p[1] for p in _PAIRS], jnp.int32)[pair]
    ea = group * EXPERTS_PER_GROUP + pa
    eb = group * EXPERTS_PER_GROUP + pb
    return dest, tok, srt, ea, eb, n_used.reshape(1).astype(jnp.int32)


def _post0_kernel(o_ref, r_ref, x_ref, g1_ref, sh2_ref, sc2_ref, n2_ref, wout_ref, wr_ref, br_ref,
                  earlier_ref, x1_ref, pay_ref, info_ref, counts_ref, count_ref):
    a = (o_ref[...].astype(F32) * _silu(r_ref[...].astype(F32))).astype(BF16)
    x1 = x_ref[...] + g1_ref[...] * _dot(a, wout_ref[...])
    x1_ref[...] = x1
    h2 = _norm_modulate(x1, n2_ref[...], sc2_ref[...], sh2_ref[...])
    _route_outputs(h2, wr_ref, br_ref, earlier_ref, count_ref, pay_ref, info_ref, counts_ref)


def _post0(o, r, x, mods4, layer, n2, wout, wr_t, br, tm):
    bsz, t, d = x.shape
    tm = min(tm, t)
    rb = lambda b: b
    tok = lambda w: pl.BlockSpec((None, tm, w), lambda b, i: (b, i, 0))
    full = lambda a: pl.BlockSpec(a.shape, lambda b, i: (0,) * a.ndim)
    nt = t // tm
    earlier = _earlier_matrix(tm)
    return pl.pallas_call(
        _post0_kernel,
        out_shape=(jax.ShapeDtypeStruct((bsz, t, d), F32),) + _route_out_shapes(bsz, t, d),
        grid=(bsz, nt),
        in_specs=[
            tok(d), tok(d), tok(d),
            _mod_spec(layer, 2, d, rb), _mod_spec(layer, 3, d, rb), _mod_spec(layer, 4, d, rb),
            full(n2), full(wout), full(wr_t), full(br), full(earlier),
        ],
        out_specs=(tok(d),) + _route_out_specs(tm, nt, d),
        scratch_shapes=[pltpu.VMEM((CLASS_ROWS, LANES), F32)],
        compiler_params=_cparams(("arbitrary", "arbitrary")),
        name="post0",
    )(o, r, x, mods4, mods4, mods4, n2, wout, wr_t, br, earlier)


def _row_group(i, per):
    return pl.ds(pl.multiple_of(i * per, per), per)


def _moe_kernel(tok_ref, ea_ref, eb_ref, used_ref, p_hbm, gw_ref, wa_ref, wb_ref, f_ref,
                pbuf0, pbuf1, gsem, *, d, tm):
    del ea_ref, eb_ref
    j = pl.program_id(0)
    n_tiles = pl.num_programs(0)
    used = used_ref[0]
    per = d // LANES
    pbufs = (pbuf0, pbuf1)

    def gather_row(tile, r, p, pred, lane):
        i = tok_ref[tile * tm + r]

        @pl.when(pred)
        def _():
            pltpu.make_async_copy(p_hbm.at[_row_group(i, per), :], pbufs[p].at[_row_group(r, per), :],
                                  gsem.at[p]).start(priority=lane)

    def gather_wait(p):
        pltpu.make_async_copy(p_hbm.at[pl.ds(0, tm * per), :], pbufs[p], gsem.at[p]).wait()

    @pl.when(j == 0)
    def _():
        def body(blk, carry):
            for u in range(DMA_UNROLL):
                gather_row(0, blk * DMA_UNROLL + u, 0, True, u % 2)
            return carry
        lax.fori_loop(0, tm // DMA_UNROLL, body, 0)

    def step(p, beside_matmuls):
        @pl.when(j < used)
        def _():
            has_next = j + 1 < used
            nxt = jnp.minimum(j + 1, n_tiles - 1)
            if beside_matmuls:
                gather_wait(p)
                for r in range(tm):
                    gather_row(nxt, r, 1 - p, has_next, r % 2)
            else:
                @pl.when(j >= 0)
                def _():
                    for r in range(tm):
                        gather_row(nxt, r, 1 - p, has_next, r % 2)
                gather_wait(p)
            h = _load_tiled_rows(pbufs[p], tm).astype(BF16)
            gw = gw_ref[...]

            def expert(w_ref):
                de = w_ref.shape[2]
                he = (_silu(_dot(h, w_ref[0])) * _dot(h, w_ref[1])).astype(BF16)
                return jnp.concatenate([_dot(he, w_ref[2, :de, :]), _dot(he, w_ref[2, de:, :])], axis=1)

            _store_tiled_rows(f_ref, gw[:, 1:2] * expert(wa_ref) + gw[:, 2:3] * expert(wb_ref))

    @pl.when(j % 2 == 0)
    def _():
        step(0, True)

    @pl.when(j % 2 == 1)
    def _():
        step(1, False)

    @pl.when(j >= used)
    def _():
        f_ref[...] = jnp.zeros(f_ref.shape, F32)


def _moe(payload, tok, srt, ea, eb, n_used, w_all, layer, tm):
    n_sorted = tok.shape[0]
    n_tiles = n_sorted // tm
    _, _, _, d, de = w_all.shape
    per = d // LANES
    wa = lambda j, tok, ea, eb, used: (layer, ea[j], 0, 0, 0)
    wb = lambda j, tok, ea, eb, used: (layer, eb[j], 0, 0, 0)
    return pl.pallas_call(
        functools.partial(_moe_kernel, d=d, tm=tm),
        out_shape=jax.ShapeDtypeStruct((n_sorted * per, LANES), F32),
        grid_spec=pltpu.PrefetchScalarGridSpec(
            num_scalar_prefetch=4,
            grid=(n_tiles,),
            in_specs=[
                pl.BlockSpec(memory_space=pl.ANY),
                pl.BlockSpec((tm, srt.shape[1]), lambda j, *_: (j, 0)),
                pl.BlockSpec((None, None, 3, d, de), wa), pl.BlockSpec((None, None, 3, d, de), wb),
            ],
            out_specs=pl.BlockSpec((tm * per, LANES), lambda j, *_: (j, 0)),
            scratch_shapes=[pltpu.VMEM((tm * per, LANES), F32), pltpu.VMEM((tm * per, LANES), F32),
                            pltpu.SemaphoreType.DMA((2,))],
        ),
        compiler_params=_cparams(("arbitrary",)),
        name="moe",
    )(tok, ea, eb, n_used, payload, srt, w_all, w_all)


def _expert_weights_kernel(w1_ref, w3_ref, w2_ref, o_ref):
    de = w2_ref.shape[0]
    o_ref[0] = w1_ref[...].astype(BF16)
    o_ref[1] = w3_ref[...].astype(BF16)
    o_ref[2, :de, :] = w2_ref[:, :de].astype(BF16)
    o_ref[2, de:, :] = w2_ref[:, de:].astype(BF16)


def _expert_weights(w1, w3, w2):
    nl, ne, d, de = w1.shape
    assert d == 2 * de
    up = pl.BlockSpec((None, None, d, de), lambda l, e: (l, e, 0, 0))
    return pl.pallas_call(
        _expert_weights_kernel,
        out_shape=jax.ShapeDtypeStruct((nl, ne, 3, d, de), BF16),
        grid=(nl, ne),
        in_specs=[up, up, pl.BlockSpec((None, None, de, d), lambda l, e: (l, e, 0, 0))],
        out_specs=pl.BlockSpec((None, None, 3, d, de), lambda l, e: (l, e, 0, 0, 0)),
        compiler_params=_cparams(("arbitrary", "arbitrary")),
        name="expert_weights",
    )(w1, w3, w2)


def _sparse_moe(payload, info, counts, w_all, layer):
    n = info.shape[1]
    tmoe = min(TM_MOE, n)
    dest, tok, srt, ea, eb, n_used = _plan(info, counts, n, tmoe)
    return _moe(payload, tok, srt, ea, eb, n_used, w_all, layer, tmoe), dest


def _gather_rows_start(dest_ref, f_hbm, fbuf, fsem, tile, tm, per, pred, unrolled):
    def one(r, lane):
        i = dest_ref[tile * tm + r]

        @pl.when(pred)
        def _():
            pltpu.make_async_copy(f_hbm.at[_row_group(i, per), :], fbuf.at[_row_group(r, per), :],
                                  fsem).start(priority=lane)

    if unrolled:
        for r in range(tm):
            one(r, r % 2)
    else:
        def body(blk, carry):
            for u in range(DMA_UNROLL):
                one(blk * DMA_UNROLL + u, u % 2)
            return carry
        lax.fori_loop(0, tm // DMA_UNROLL, body, 0)


def _gather_rows_wait(f_hbm, fbuf, fsem):
    pltpu.make_async_copy(f_hbm.at[pl.ds(0, fbuf.shape[0]), :], fbuf, fsem).wait()


def _pool_constants(tm, d):
    ng = len(POOL_WINDOWS)
    cg = d // ng
    pos = np.arange(tm)
    seg, off = pos // GRID_W, pos % GRID_W
    mats = np.zeros((ng, tm, tm), np.float32)
    inv = np.zeros((tm, d), np.float32)
    for gi, w in enumerate(POOL_WINDOWS):
        lo = np.clip(off - w // 2, 0, GRID_W)
        hi = np.clip(off - w // 2 + w, 0, GRID_W)
        same = seg[:, None] == seg[None, :]
        inside = (off[None, :] >= lo[:, None]) & (off[None, :] < hi[:, None])
        mats[gi] = (same & inside).astype(np.float32)
        inv[:, gi * cg:(gi + 1) * cg] = (1.0 / (hi - lo).astype(np.float32))[:, None]
    return jnp.asarray(mats, BF16), jnp.asarray(inv, F32)


def _layer1_kernel(dest_ref, x_ref, f_hbm, g2p_ref, sh1_ref, sc1_ref, g1_ref, sh2_ref, sc2_ref,
                   n1_ref, n2_ref, pm_ref, inv_ref, wp_ref, bp_ref, ps_ref, wr_ref, br_ref, earlier_ref,
                   x3_ref, pay_ref, info_ref, counts_ref, fbuf, fsem, x2_ref, count_ref):
    tm, d = x_ref.shape
    per = d // LANES
    nt = pl.num_programs(1)
    lin = pl.program_id(0) * nt + pl.program_id(1)

    @pl.when(lin == 0)
    def _():
        _gather_rows_start(dest_ref, f_hbm, fbuf, fsem, 0, tm, per, True, False)

    @pl.when(lin >= 0)
    def _():
        _gather_rows_wait(f_hbm, fbuf, fsem)
        x2_ref[...] = x_ref[...] + g2p_ref[...] * _load_tiled_rows(fbuf, tm)

    total = pl.num_programs(0) * nt
    _gather_rows_start(dest_ref, f_hbm, fbuf, fsem, jnp.minimum(lin + 1, total - 1), tm, per,
                       lin + 1 < total, True)
    x2 = x2_ref[...]
    h = _norm_modulate(x2, n1_ref[...], sc1_ref[...], sh1_ref[...])
    hb = h.astype(BF16)
    ng = pm_ref.shape[0]
    cg = h.shape[1] // ng
    ys = []
    for gi in range(ng):
        cs = slice(gi * cg, (gi + 1) * cg)
        wsum = _dot(pm_ref[gi], hb[:, cs])
        pooled = wsum * inv_ref[:, cs] - h[:, cs]
        ys.append(_dot(pooled.astype(BF16), wp_ref[gi]))
    y = (jnp.concatenate(ys, axis=1) + bp_ref[...]) * ps_ref[...]
    x3 = x2 + g1_ref[...] * y
    x3_ref[...] = x3
    h2 = _norm_modulate(x3, n2_ref[...], sc2_ref[...], sh2_ref[...])
    _route_outputs(h2, wr_ref, br_ref, earlier_ref, count_ref, pay_ref, info_ref, counts_ref)


def _layer1(x1, f_sorted, dest, mods4, n1, n2, wp, bp, ps, wr_t, br, tm):
    bsz, t, d = x1.shape
    tm = min(tm, t)
    pm, inv = _pool_constants(tm, d)
    earlier = _earlier_matrix(tm)
    rb = lambda b: b
    tok = lambda w: pl.BlockSpec((None, tm, w), lambda b, i, dest: (b, i, 0))
    full = lambda a: pl.BlockSpec(a.shape, lambda b, i, dest: (0,) * a.ndim, pipeline_mode=pl.Buffered(1))
    nt = t // tm
    return pl.pallas_call(
        _layer1_kernel,
        out_shape=(jax.ShapeDtypeStruct((bsz, t, d), F32),) + _route_out_shapes(bsz, t, d),
        grid_spec=pltpu.PrefetchScalarGridSpec(
            num_scalar_prefetch=1,
            grid=(bsz, nt),
            in_specs=[
                tok(d), pl.BlockSpec(memory_space=pl.ANY),
                _mod_spec(0, 5, d, rb),
                _mod_spec(1, 0, d, rb), _mod_spec(1, 1, d, rb), _mod_spec(1, 2, d, rb),
                _mod_spec(1, 3, d, rb), _mod_spec(1, 4, d, rb),
                full(n1), full(n2), full(pm), full(inv), full(wp), full(bp), full(ps),
                full(wr_t), full(br), full(earlier),
            ],
            out_specs=(tok(d),) + _route_out_specs(tm, nt, d),
            scratch_shapes=[pltpu.VMEM((tm * (d // LANES), LANES), F32), pltpu.SemaphoreType.DMA(()),
                            pltpu.VMEM((tm, d), F32), pltpu.VMEM((CLASS_ROWS, LANES), F32)],
        ),
        compiler_params=_cparams(("arbitrary", "arbitrary")),
        name="layer1",
    )(dest, x1, f_sorted, mods4, mods4, mods4, mods4, mods4, mods4, n1, n2, pm, inv, wp, bp, ps,
      wr_t, br, earlier)


def _final_kernel(dest_ref, x_ref, f_hbm, g2_ref, fg_ref, o_ref, fbuf, fsem):
    tm, d = x_ref.shape
    per = d // LANES
    nt = pl.num_programs(1)
    lin = pl.program_id(0) * nt + pl.program_id(1)

    @pl.when(lin == 0)
    def _():
        _gather_rows_start(dest_ref, f_hbm, fbuf, fsem, 0, tm, per, True, False)

    @pl.when(lin >= 0)
    def _():
        _gather_rows_wait(f_hbm, fbuf, fsem)
        f = _load_tiled_rows(fbuf, tm)
        o_ref[...] = _rmsnorm(x_ref[...] + g2_ref[...] * f, fg_ref[...])

    total = pl.num_programs(0) * nt
    _gather_rows_start(dest_ref, f_hbm, fbuf, fsem, jnp.minimum(lin + 1, total - 1), tm, per,
                       lin + 1 < total, False)


def _final(x3, f_sorted, dest, mods4, fg, tm):
    bsz, t, d = x3.shape
    tm = min(tm, t)
    tok = pl.BlockSpec((None, tm, d), lambda b, i, dest: (b, i, 0))
    return pl.pallas_call(
        _final_kernel,
        out_shape=jax.ShapeDtypeStruct((bsz, t, d), F32),
        grid_spec=pltpu.PrefetchScalarGridSpec(
            num_scalar_prefetch=1,
            grid=(bsz, t // tm),
            in_specs=[tok, pl.BlockSpec(memory_space=pl.ANY), _mod_spec(1, 5, d, lambda b: b),
                      pl.BlockSpec((1, d), lambda b, i, dest: (0, 0))],
            out_specs=tok,
            scratch_shapes=[pltpu.VMEM((tm * (d // LANES), LANES), F32), pltpu.SemaphoreType.DMA(())],
        ),
        compiler_params=_cparams(("arbitrary", "arbitrary")),
        name="final",
    )(dest, x3, f_sorted, mods4, fg)


def kernel(x, c, ctx, c_ctx, norm1_g, norm2_g, w_mod, b_mod, gla_w_in, gla_w_gate_a, gla_w_gate_b,
           gla_b_gate, gla_norm_g, gla_w_out, pool_w, pool_b, pool_scale, w_router, b_router,
           w_gate_e, w_up_e, w_down_e, final_g):
    bsz, t, d = x.shape
    depth = w_mod.shape[0]
    assert depth == 2 and t % GRID_W == 0
    kdim = d // 2
    row = lambda a: a.reshape(1, -1)

    c_rows = -(-(bsz + 1) // 8) * 8
    c_all = jnp.zeros((c_rows, d), F32).at[:bsz].set(c).at[bsz].set(c_ctx)
    mods = _modulation(c_all, w_mod, b_mod)
    mods4 = mods.reshape(depth, c_rows, 1, 6 * d)

    win = gla_w_in[0].astype(BF16)
    wga = jnp.concatenate([gla_w_gate_a[0, 0], gla_w_gate_a[0, 1]], axis=1).astype(BF16)
    dk = kdim // GLA_HEADS
    by_head = lambda a: a.reshape(a.shape[0], GLA_HEADS, 1, dk)
    zero = jnp.zeros((GLA_GATE_RANK, GLA_HEADS, 1, dk), F32)
    wgb = jnp.concatenate([
        jnp.concatenate([by_head(gla_w_gate_b[0, 0]), zero], axis=2),
        jnp.concatenate([zero, by_head(gla_w_gate_b[0, 1])], axis=2)], axis=0)
    wgb = wgb.reshape(2 * GLA_GATE_RANK, 2 * kdim).astype(BF16)
    bg = jnp.stack([gla_b_gate[0, 0].reshape(GLA_HEADS, dk), gla_b_gate[0, 1].reshape(GLA_HEADS, dk)],
                   axis=1).reshape(1, 2 * kdim)
    n1_0 = row(norm1_g[0])
    q, k, v, r, g = _gla_in(x, mods4, 0, lambda b: b, n1_0, win, wga, wgb, bg, TM_PROJ)
    kc, vc, gc = _gla_in(ctx, mods4, 0, lambda b: bsz, n1_0, win, wga, wgb, bg, TM_CTX, state_only=True)
    s0 = _gla_state(kc, vc, gc, GLA_CHUNK, GLA_BLOCK)
    o = _gla_scan(q, k, v, g, s0, row(gla_norm_g[0]), GLA_CHUNK, GLA_BLOCK)

    wr_t = jnp.transpose(w_router)
    br = b_router.reshape(N_EXPERTS, 1)
    x1, payload, info, counts = _post0(o, r, x, mods4, 0, row(norm2_g[0]),
                                       gla_w_out[0].astype(BF16), wr_t, br, TM_POST)
    w_experts = _expert_weights(w_gate_e, w_up_e, w_down_e)
    f0, dest0 = _sparse_moe(payload, info, counts, w_experts, 0)

    x3, payload, info, counts = _layer1(x1, f0, dest0, mods4, row(norm1_g[1]), row(norm2_g[1]),
                                        pool_w[0].astype(BF16), row(pool_b[0]), row(pool_scale[0]),
                                        wr_t, br, TM_POOL)
    f1, dest1 = _sparse_moe(payload, info, counts, w_experts, 1)
    return _final(x3, f1, dest1, mods4, row(final_g), TM_FINAL)
```
